```python
import jax, jax.numpy as jnp
from jax import lax
import numpy as np

D_MODEL = 1024
BATCH = 2
SEQ = 16384
DEPTH = 2
DEC_BATCH = 8
DEC_SEQ = 8192
PAST_LEN = 128

GLA_HEADS = 4
GLA_HEAD_K = D_MODEL // 8
GLA_HEAD_V = D_MODEL // 4
GLA_DK = GLA_HEADS * GLA_HEAD_K
GLA_DV = GLA_HEADS * GLA_HEAD_V
GATE_RANK = 16
GATE_NORM = 16.0
CHUNK = 64
FNET_GROUPS = 4
FNET_GROUP_W = D_MODEL // 8
FNET_W = FNET_GROUPS * FNET_GROUP_W
IN_COLS = 2 * GLA_DK + 2 * GLA_DV + 2 * GATE_RANK + FNET_W + 2 * D_MODEL
D_FF = 256 * ((8 * D_MODEL // 3 + 255) // 256)
N_EXPERTS = 8
TOP_K = 2
D_EXPERT = 7 * D_MODEL // 2
MOE_BLOCK = 256
N_DENSE = (DEPTH + 1) // 2
N_MOE = DEPTH // 2
EPS = 1e-6

kernel_name = 'hybrid_gla_fnet_moe_adaln_encoder'


def _rmsnorm(x, g):
    xf = x.astype(jnp.float32)
    y = xf * lax.rsqrt(jnp.mean(xf * xf, axis=-1, keepdims=True) + EPS)
    return (y * g.astype(jnp.float32)).astype(x.dtype)


def _gla_scan(q, k, v, g):
    bn, h, s, dk = q.shape
    dv = v.shape[-1]
    n = s // CHUNK
    q = q.reshape(bn, h, n, CHUNK, dk)
    k = k.reshape(bn, h, n, CHUNK, dk)
    v = v.reshape(bn, h, n, CHUNK, dv)
    b = jnp.cumsum(g.reshape(bn, h, n, CHUNK, dk), axis=3)
    b_last = b[:, :, :, -1:, :]
    b_ref = b[:, :, :, CHUNK // 2 - 1:CHUNK // 2, :]
    qi = q * jnp.exp(b - b_ref)
    ki = k * jnp.exp(b_ref - b)
    mask = jnp.tril(jnp.ones((CHUNK, CHUNK), dtype=bool))
    att = jnp.where(mask, jnp.einsum('bhnid,bhnjd->bhnij', qi, ki), 0.0)
    o_intra = jnp.einsum('bhnij,bhnjv->bhniv', att, v)
    q_in = q * jnp.exp(b)
    k_out = k * jnp.exp(b_last - b)
    decay = jnp.exp(b_last[:, :, :, 0, :])

    def step(state, xs):
        qc, kc, vc, dc = xs
        o = jnp.einsum('bhid,bhdv->bhiv', qc, state)
        state = dc[..., None] * state + jnp.einsum('bhjd,bhjv->bhdv', kc, vc)
        return state, o

    xs = (jnp.moveaxis(q_in, 2, 0), jnp.moveaxis(k_out, 2, 0),
          jnp.moveaxis(v, 2, 0), jnp.moveaxis(decay, 2, 0))
    _, o_inter = lax.scan(step, jnp.zeros((bn, h, dk, dv), jnp.float32), xs)
    o = o_intra + jnp.moveaxis(o_inter, 0, 2)
    return o.reshape(bn, h, s, dv)


def _gla_branch(zq, zk, zv, zr, za_f, za_b, w_af2, b_af, w_ab2, b_ab, norm_g, w_o):
    bn, s, _ = zq.shape

    def heads(t, d):
        return jnp.transpose(t.astype(jnp.float32).reshape(bn, s, GLA_HEADS, d), (0, 2, 1, 3))

    q = heads(zq, GLA_HEAD_K) * (GLA_HEAD_K ** -0.5)
    k = heads(zk, GLA_HEAD_K)
    v = heads(zv, GLA_HEAD_V)
    g_f = heads(jax.nn.log_sigmoid((za_f @ w_af2 + b_af).astype(jnp.float32)) / GATE_NORM, GLA_HEAD_K)
    g_b = heads(jax.nn.log_sigmoid((za_b @ w_ab2 + b_ab).astype(jnp.float32)) / GATE_NORM, GLA_HEAD_K)
    flip = lambda t: jnp.flip(t, axis=2)
    o = _gla_scan(q, k, v, g_f) + flip(_gla_scan(flip(q), flip(k), flip(v), flip(g_b)))
    o = jnp.transpose(o, (0, 2, 1, 3))
    o = o * lax.rsqrt(jnp.mean(o * o, axis=-1, keepdims=True) + EPS) * norm_g.astype(jnp.float32)
    r = jax.nn.silu(zr.astype(jnp.float32)).reshape(bn, s, GLA_HEADS, GLA_HEAD_V)
    o = (o * r).reshape(bn, s, GLA_DV).astype(zq.dtype)
    return o @ w_o


def _fnet_branch(zf, w_o):
    bn, s, _ = zf.shape
    u = zf.astype(jnp.float32).reshape(bn, s, FNET_GROUPS, FNET_GROUP_W)
    y = jnp.real(jnp.fft.fft2(u, axes=(1, 3), norm='ortho'))
    return y.reshape(bn, s, FNET_W).astype(zf.dtype) @ w_o


def _token_mixer(h, w_in, w_af2, b_af, w_ab2, b_ab, gla_norm_g, w_o_gla, w_o_fnet, w_out):
    z = h @ w_in
    sizes = [GLA_DK, GLA_DK, GLA_DV, GLA_DV, GATE_RANK, GATE_RANK, FNET_W, D_MODEL, D_MODEL]
    offs = np.cumsum(sizes)[:-1].tolist()
    zq, zk, zv, zr, za_f, za_b, zf, zg_a, zg_b = jnp.split(z, offs, axis=-1)
    y_a = _gla_branch(zq, zk, zv, zr, za_f, za_b, w_af2, b_af, w_ab2, b_ab, gla_norm_g, w_o_gla)
    y_b = _fnet_branch(zf, w_o_fnet)
    m = jax.nn.sigmoid(zg_a) * y_a + jax.nn.sigmoid(zg_b) * y_b
    return m @ w_out


def _swiglu(h, w_gate, w_up, w_down):
    return (jax.nn.silu(h @ w_gate) * (h @ w_up)) @ w_down


def _moe_swiglu(h, w_router, w_g, w_u, w_d):
    bn, s, dm = h.shape
    t = bn * s
    xt = h.reshape(t, dm)
    logits = (xt @ w_router).astype(jnp.float32)
    top_v, top_i = lax.top_k(logits, TOP_K)
    wts = jax.nn.softmax(top_v, axis=-1)
    a = t * TOP_K
    e_flat = top_i.reshape(a)
    tok_flat = jnp.repeat(jnp.arange(t, dtype=jnp.int32), TOP_K)
    w_flat = wts.reshape(a)
    order = jnp.argsort(e_flat)
    e_s, tok_s, w_s = e_flat[order], tok_flat[order], w_flat[order]
    counts = jnp.bincount(e_flat, length=N_EXPERTS)
    starts = jnp.cumsum(counts) - counts
    padded = (counts + MOE_BLOCK - 1) // MOE_BLOCK * MOE_BLOCK
    pends = jnp.cumsum(padded)
    pstarts = pends - padded
    dest = pstarts[e_s] + (jnp.arange(a, dtype=jnp.int32) - starts[e_s])
    p = a + N_EXPERTS * MOE_BLOCK
    nb = p // MOE_BLOCK
    xp = jnp.zeros((p, dm), h.dtype).at[dest].set(xt[tok_s])
    block_e = jnp.minimum(jnp.searchsorted(pends, jnp.arange(nb) * MOE_BLOCK, side='right'),
                          N_EXPERTS - 1)

    def expert_block(args):
        xb, e = args
        return (jax.nn.silu(xb @ w_g[e]) * (xb @ w_u[e])) @ w_d[e]

    yp = lax.map(expert_block, (xp.reshape(nb, MOE_BLOCK, dm), block_e)).reshape(p, dm)
    y = jnp.zeros((t, dm), jnp.float32).at[tok_s].add(yp[dest].astype(jnp.float32) * w_s[:, None])
    return y.astype(h.dtype).reshape(bn, s, dm)


def _encoder(x, c, norm1_g, norm2_g, w_ada, b_ada, w_in, w_af2, b_af, w_ab2, b_ab, gla_norm_g,
             w_o_gla, w_o_fnet, w_out, w_ff_gate, w_ff_up, w_ff_down, w_router, w_e_gate,
             w_e_up, w_e_down, final_norm_g):
    cs = jax.nn.silu(c)
    for l in range(DEPTH):
        mod = cs @ w_ada[l] + b_ada[l]
        sh1, sc1, gt1, sh2, sc2, gt2 = [m[:, None, :] for m in jnp.split(mod, 6, axis=-1)]
        h = _rmsnorm(x, norm1_g[l]) * (1.0 + sc1) + sh1
        x = x + gt1 * _token_mixer(h, w_in[l], w_af2[l], b_af[l], w_ab2[l], b_ab[l], gla_norm_g[l],
                                   w_o_gla[l], w_o_fnet[l], w_out[l])
        h = _rmsnorm(x, norm2_g[l]) * (1.0 + sc2) + sh2
        if l % 2 == 0:
            f = _swiglu(h, w_ff_gate[l // 2], w_ff_up[l // 2], w_ff_down[l // 2])
        else:
            f = _moe_swiglu(h, w_router[l // 2], w_e_gate[l // 2], w_e_up[l // 2], w_e_down[l // 2])
        x = x + gt2 * f
    return _rmsnorm(x, final_norm_g)


def setup_inputs(seed: int = 0) -> dict:
    key = jax.random.key(seed)
    ks = jax.random.split(key, 32)
    nrm = lambda k, shape, scale: jax.random.normal(k, shape, jnp.float32) * scale
    D = D_MODEL
    return {
        'x_prompt': nrm(ks[0], (BATCH, SEQ, D), 1.0),
        'x_sample': nrm(ks[1], (DEC_BATCH, DEC_SEQ, D), 1.0),
        'c_prompt': nrm(ks[2], (BATCH, D), 1.0),
        'c_sample': nrm(ks[3], (DEC_BATCH, D), 1.0),
        'norm1_g': 1.0 + nrm(ks[4], (DEPTH, D), 0.02),
        'norm2_g': 1.0 + nrm(ks[5], (DEPTH, D), 0.02),
        'w_ada': nrm(ks[6], (DEPTH, D, 6 * D), D ** -0.5),
        'b_ada': nrm(ks[7], (DEPTH, 6 * D), 0.02),
        'w_in': nrm(ks[8], (DEPTH, D, IN_COLS), D ** -0.5),
        'w_af2': nrm(ks[9], (DEPTH, GATE_RANK, GLA_DK), GATE_RANK ** -0.5),
        'b_af': nrm(ks[10], (DEPTH, GLA_DK), 0.01),
        'w_ab2': nrm(ks[11], (DEPTH, GATE_RANK, GLA_DK), GATE_RANK ** -0.5),
        'b_ab': nrm(ks[12], (DEPTH, GLA_DK), 0.01),
        'gla_norm_g': 1.0 + nrm(ks[13], (DEPTH, GLA_HEAD_V), 0.02),
        'w_o_gla': nrm(ks[14], (DEPTH, GLA_DV, D), GLA_DV ** -0.5),
        'w_o_fnet': nrm(ks[15], (DEPTH, FNET_W, D), FNET_W ** -0.5),
        'w_out': nrm(ks[16], (DEPTH, D, D), D ** -0.5),
        'w_ff_gate': nrm(ks[17], (N_DENSE, D, D_FF), D ** -0.5),
        'w_ff_up': nrm(ks[18], (N_DENSE, D, D_FF), D ** -0.5),
        'w_ff_down': nrm(ks[19], (N_DENSE, D_FF, D), D_FF ** -0.5),
        'w_router': nrm(ks[20], (N_MOE, D, N_EXPERTS), D ** -0.5),
        'w_e_gate': nrm(ks[21], (N_MOE, N_EXPERTS, D, D_EXPERT), D ** -0.5),
        'w_e_up': nrm(ks[22], (N_MOE, N_EXPERTS, D, D_EXPERT), D ** -0.5),
        'w_e_down': nrm(ks[23], (N_MOE, N_EXPERTS, D_EXPERT, D), D_EXPERT ** -0.5),
        'final_norm_g': 1.0 + nrm(ks[24], (D,), 0.02),
    }


def reference(x_prompt, x_sample, c_prompt, c_sample, norm1_g, norm2_g, w_ada, b_ada, w_in, w_af2,
              b_af, w_ab2, b_ab, gla_norm_g, w_o_gla, w_o_fnet, w_out, w_ff_gate, w_ff_up, w_ff_down,
              w_router, w_e_gate, w_e_up, w_e_down, final_norm_g):
    y_prompt = _encoder(x_prompt, c_prompt, norm1_g, norm2_g, w_ada, b_ada, w_in, w_af2, b_af, w_ab2,
                        b_ab, gla_norm_g, w_o_gla, w_o_fnet, w_out, w_ff_gate, w_ff_up, w_ff_down,
                        w_router, w_e_gate, w_e_up, w_e_down, final_norm_g)
    y_sample = _encoder(x_sample, c_sample, norm1_g, norm2_g, w_ada, b_ada, w_in, w_af2, b_af, w_ab2,
                        b_ab, gla_norm_g, w_o_gla, w_o_fnet, w_out, w_ff_gate, w_ff_up, w_ff_down,
                        w_router, w_e_gate, w_e_up, w_e_down, final_norm_g)
    return (y_prompt, y_sample)
```

```python
import functools

import numpy as np
import jax
import jax.numpy as jnp
from jax import lax
from jax.experimental import pallas as pl
from jax.experimental.pallas import tpu as pltpu

D = 1024
HEADS = 4
DK = 128
DV = 256
GLA_DK = HEADS * DK
GLA_DV = HEADS * DV
RANK = 16
GATE_NORM = 16.0
CHUNK = 64
FN_GROUPS = 4
FN_GW = 128
FNET_W = FN_GROUPS * FN_GW
D_FF = 2816
N_EXP = 8
D_EXP = 3584
EPS = 1e-6
N_MOD = 6

LANE = 128
SUBLANE = 8
VMEM_LIMIT = 56 * 1024 * 1024
TM = 512
GLA_BLK = 512
MOE_TILE = 256
DISP_BLK = 512
FF_CHUNK_DENSE = 1408
FF_CHUNK_MOE = 512
FN_COLS = 4096
FN_GROUPS_PER_STEP = 4
N1 = 128

F32 = jnp.float32
BF16 = jnp.bfloat16
_HI = lax.Precision.HIGHEST


def _cparams(sem):
    return pltpu.CompilerParams(dimension_semantics=sem, vmem_limit_bytes=VMEM_LIMIT)


def _dot(a, b):
    return jnp.dot(a, b, preferred_element_type=F32)


def _dot_hi(a, b):
    return jnp.dot(a, b, precision=_HI, preferred_element_type=F32)


def _split_bf16(x):
    hi = x.astype(BF16)
    lo = (x - hi.astype(F32)).astype(BF16)
    return hi, lo


def _dot3(a_hi, a_lo, b_hi, b_lo):
    return _dot(a_hi, b_hi) + (_dot(a_hi, b_lo) + _dot(a_lo, b_hi))


class _Stream:
    def __init__(self, b, s, db, ds):
        self.b, self.s, self.db, self.ds = b, s, db, ds
        self.t0 = b * s
        self.t = b * s + db * ds
        self.nbatch = b + db

    def batch_of_block(self, i, rows):
        nb0, p0, p1 = self.t0 // rows, self.s // rows, self.ds // rows
        return jnp.where(i < nb0, i // p0, self.b + (i - nb0) // p1)

    def block_in_seq(self, i, rows):
        nb0, p0, p1 = self.t0 // rows, self.s // rows, self.ds // rows
        return jnp.where(i < nb0, i % p0, (i - nb0) % p1), jnp.where(i < nb0, p0, p1)


def _mod_kernel(c_ref, w_ref, b_ref, o_ref):
    c = c_ref[...]
    cs = c * jax.nn.sigmoid(c)
    o_ref[0] = _dot_hi(cs, w_ref[0]) + b_ref[0]


def _modulation(c_pad, w_ada, b_ada):
    depth = w_ada.shape[0]
    nb = c_pad.shape[0]
    return pl.pallas_call(
        _mod_kernel,
        grid=(depth, N_MOD),
        in_specs=[
            pl.BlockSpec((nb, D), lambda l, j: (0, 0)),
            pl.BlockSpec((1, D, D), lambda l, j: (l, 0, j)),
            pl.BlockSpec((1, 1, D), lambda l, j: (l, 0, j)),
        ],
        out_specs=pl.BlockSpec((1, nb, D), lambda l, j: (l, 0, j)),
        out_shape=jax.ShapeDtypeStruct((depth, nb, N_MOD * D), F32),
        compiler_params=_cparams(("arbitrary", "arbitrary")),
        name="adaln_mod",
    )(c_pad, w_ada, b_ada.reshape(depth, 1, N_MOD * D))


def _rms_mod(x, g, sc, sh):
    ms = jnp.mean(x * x, axis=-1, keepdims=True)
    return (x * lax.rsqrt(ms + EPS) * g) * (1.0 + sc) + sh


_C_Q, _C_K, _C_V, _C_R, _C_A, _C_F, _C_GA, _C_GB, _C_END = np.cumsum(
    [0, GLA_DK, GLA_DK, GLA_DV, GLA_DV, LANE, FNET_W, D, D]).tolist()


def _inproj_kernel(x_ref, g_ref, sc_ref, sh_ref, w_ref,
                   q_ref, k_ref, v_ref, r_ref, a_ref, f_ref, ga_ref, gb_ref):
    hb = _rms_mod(x_ref[...], g_ref[...], sc_ref[0], sh_ref[0]).astype(BF16)

    def proj(lo, hi):
        return _dot(hb, w_ref[:, lo:hi])

    q_ref[...] = (proj(_C_Q, _C_K) * (DK ** -0.5)).astype(BF16)
    k_ref[...] = proj(_C_K, _C_V).astype(BF16)
    v_ref[...] = proj(_C_V, _C_R).astype(BF16)
    zr = proj(_C_R, _C_A)
    r_ref[...] = (zr * jax.nn.sigmoid(zr)).astype(BF16)
    a_ref[...] = proj(_C_A, _C_F)
    f_ref[...] = proj(_C_F, _C_GA)
    ga_ref[...] = jax.nn.sigmoid(proj(_C_GA, _C_GB)).astype(BF16)
    gb_ref[...] = jax.nn.sigmoid(proj(_C_GB, _C_END)).astype(BF16)


def _inproj(st, x, g, sc, sh, w):
    t = st.t
    row = lambda i: (i, 0)
    bat = lambda i: (st.batch_of_block(i, TM), 0, 0)
    widths = [(GLA_DK, BF16), (GLA_DK, BF16), (GLA_DV, BF16), (GLA_DV, BF16),
              (LANE, F32), (FNET_W, F32), (D, BF16), (D, BF16)]
    return pl.pallas_call(
        _inproj_kernel,
        grid=(t // TM,),
        in_specs=[
            pl.BlockSpec((TM, D), row),
            pl.BlockSpec((1, D), lambda i: (0, 0)),
            pl.BlockSpec((1, 1, D), bat),
            pl.BlockSpec((1, 1, D), bat),
            pl.BlockSpec((D, _C_END), lambda i: (0, 0)),
        ],
        out_specs=[pl.BlockSpec((TM, w_), row) for w_, _ in widths],
        out_shape=[jax.ShapeDtypeStruct((t, w_), dt) for w_, dt in widths],
        compiler_params=_cparams(("parallel",)),
        name="inproj",
    )(x, g, sc, sh, w)


def _log_sigmoid(x):
    return jnp.minimum(x, 0.0) - jnp.log1p(jnp.exp(-jnp.abs(x)))


def _gla_kernel(*refs, bwd, st):
    if bwd:
        (q_ref, k_ref, v_ref, za_ref, wa_ref, ba_ref, of_ref, r_ref, ng_ref,
         o_ref, s_scr, g_scr) = refs
    else:
        q_ref, k_ref, v_ref, za_ref, wa_ref, ba_ref, o_ref, s_scr, g_scr = refs
    i = pl.program_id(0)
    nblk = pl.num_programs(0)
    blk = (nblk - 1 - i) if bwd else i
    local, per = st.block_in_seq(blk, GLA_BLK)
    at_boundary = (local == per - 1) if bwd else (local == 0)

    @pl.when(at_boundary)
    def _():
        s_scr[...] = jnp.zeros_like(s_scr)

    g_scr[...] = _log_sigmoid(_dot_hi(za_ref[...], wa_ref[...]) + ba_ref[...]) * (1.0 / GATE_NORM)

    rr = lax.broadcasted_iota(jnp.int32, (CHUNK, CHUNK), 0)
    cc = lax.broadcasted_iota(jnp.int32, (CHUNK, CHUNK), 1)
    keep = (rr <= cc) if bwd else (rr >= cc)
    tri = keep.astype(F32)
    last = 0 if bwd else CHUNK - 1
    mid = CHUNK // 2 if bwd else CHUNK // 2 - 1
    nch = GLA_BLK // CHUNK

    def chunk_step(ci, carry):
        c = (nch - 1 - ci) if bwd else ci
        rows = pl.ds(pl.multiple_of(c * CHUNK, CHUNK), CHUNK)
        for h in range(HEADS):
            ks = slice(h * DK, (h + 1) * DK)
            vs = slice(h * DV, (h + 1) * DV)
            b = _dot_hi(tri, g_scr[rows, ks])
            b_last = b[last:last + 1, :]
            b_ref = b[mid:mid + 1, :]
            q = q_ref[rows, ks].astype(F32)
            k = k_ref[rows, ks].astype(F32)
            v = v_ref[rows, vs]
            qi = (q * jnp.exp(b - b_ref)).astype(BF16)
            ki = (k * jnp.exp(b_ref - b)).astype(BF16)
            att = lax.dot_general(qi, ki, (((1,), (1,)), ((), ())), preferred_element_type=F32)
            att = jnp.where(keep, att, 0.0).astype(BF16)
            o = _dot(att, v)
            s_t = s_scr[h]
            q_in = (q * jnp.exp(b)).astype(BF16)
            o = o + lax.dot_general(q_in, s_t.astype(BF16), (((1,), (1,)), ((), ())),
                                    preferred_element_type=F32)
            k_out = (k * jnp.exp(b_last - b)).astype(BF16)
            s_scr[h] = s_t * jnp.exp(b_last) + lax.dot_general(
                v, k_out, (((0,), (0,)), ((), ())), preferred_element_type=F32)
            if bwd:
                o = o + of_ref[rows, vs]
                o = o * lax.rsqrt(jnp.mean(o * o, axis=-1, keepdims=True) + EPS) * ng_ref[...]
                o_ref[rows, vs] = (o * r_ref[rows, vs].astype(F32)).astype(o_ref.dtype)
            else:
                o_ref[rows, vs] = o
        return carry

    lax.fori_loop(0, nch, chunk_step, 0)


def _gla_direction(st, bwd, q, k, v, za, wa, ba, extra=()):
    t = st.t
    nblk = t // GLA_BLK
    row = (lambda i: (nblk - 1 - i, 0)) if bwd else (lambda i: (i, 0))
    const = lambda i: (0, 0)
    in_specs = [
        pl.BlockSpec((GLA_BLK, GLA_DK), row),
        pl.BlockSpec((GLA_BLK, GLA_DK), row),
        pl.BlockSpec((GLA_BLK, GLA_DV), row),
        pl.BlockSpec((GLA_BLK, LANE), row),
        pl.BlockSpec((LANE, GLA_DK), const),
        pl.BlockSpec((1, GLA_DK), const),
    ]
    if bwd:
        in_specs += [pl.BlockSpec((GLA_BLK, GLA_DV), row), pl.BlockSpec((GLA_BLK, GLA_DV), row),
                     pl.BlockSpec((1, DV), const)]
    return pl.pallas_call(
        functools.partial(_gla_kernel, bwd=bwd, st=st),
        grid=(nblk,),
        in_specs=in_specs,
        out_specs=pl.BlockSpec((GLA_BLK, GLA_DV), row),
        out_shape=jax.ShapeDtypeStruct((t, GLA_DV), BF16 if bwd else F32),
        scratch_shapes=[pltpu.VMEM((HEADS, DV, DK), F32), pltpu.VMEM((GLA_BLK, GLA_DK), F32)],
        compiler_params=_cparams(("arbitrary",)),
        name="gla_bwd" if bwd else "gla_fwd",
    )(q, k, v, za, wa, ba, *extra)


def _np_split(m):
    m = np.asarray(m, np.float32)
    hi = m.astype(BF16)
    lo = (m - hi.astype(np.float32)).astype(BF16)
    return jnp.asarray(hi), jnp.asarray(lo)


def _dft_cos_sin(n):
    idx = np.arange(n, dtype=np.float64)
    ang = 2.0 * np.pi * np.outer(idx, idx) / n
    return np.cos(ang) / np.sqrt(n), np.sin(ang) / np.sqrt(n)


def _fnet_stage1_kernel(x_ref, mh_ref, ml_ref, y_ref):
    xh, xl = _split_bf16(x_ref[0])
    y_ref[0] = _dot3(mh_ref[...], ml_ref[...], xh, xl)


def _fnet_stage1(x2, mh, ml):
    bn, n2, cols = x2.shape
    return pl.pallas_call(
        _fnet_stage1_kernel,
        grid=(bn, cols // FN_COLS),
        in_specs=[
            pl.BlockSpec((1, n2, FN_COLS), lambda b, j: (b, 0, j)),
            pl.BlockSpec((2 * n2, n2), lambda b, j: (0, 0)),
            pl.BlockSpec((2 * n2, n2), lambda b, j: (0, 0)),
        ],
        out_specs=pl.BlockSpec((1, 2 * n2, FN_COLS), lambda b, j: (b, 0, j)),
        out_shape=jax.ShapeDtypeStruct((bn, 2 * n2, cols), F32),
        compiler_params=_cparams(("parallel", "parallel")),
        name="fnet_stage1",
    )(x2, mh, ml)


def _fnet_stage2_kernel(y_ref, tc_ref, ts_ref, m2h_ref, m2l_ref, m3h_ref, m3l_ref, o_ref):
    for g in range(FN_GROUPS_PER_STEP):
        rows = slice(g * N1, (g + 1) * N1)
        yr = y_ref[0, 0, rows, :]
        yi = y_ref[0, 1, rows, :]
        tc = tc_ref[g]
        ts = ts_ref[g]
        stack = jnp.concatenate([yr * tc + yi * ts, yi * tc - yr * ts], axis=0)
        sh, sl = _split_bf16(stack)
        z = _dot3(m2h_ref[...], m2l_ref[...], sh, sl)
        zcat = jnp.concatenate([z[:N1], z[N1:]], axis=1)
        zh, zl = _split_bf16(zcat)
        o_ref[0, rows, :] = _dot3(zh, zl, m3h_ref[...], m3l_ref[...]).astype(o_ref.dtype)


def _fnet_stage2(y4, tc, ts, m2, m3):
    bn, _, s, _ = y4.shape
    rows = FN_GROUPS_PER_STEP * N1
    const = lambda b, j: (0, 0)
    return pl.pallas_call(
        _fnet_stage2_kernel,
        grid=(bn, s // rows),
        in_specs=[
            pl.BlockSpec((1, 2, rows, FNET_W), lambda b, j: (b, 0, j, 0)),
            pl.BlockSpec((FN_GROUPS_PER_STEP, N1, 1), lambda b, j: (j, 0, 0)),
            pl.BlockSpec((FN_GROUPS_PER_STEP, N1, 1), lambda b, j: (j, 0, 0)),
            pl.BlockSpec((2 * N1, 2 * N1), const),
            pl.BlockSpec((2 * N1, 2 * N1), const),
            pl.BlockSpec((2 * FNET_W, FNET_W), const),
            pl.BlockSpec((2 * FNET_W, FNET_W), const),
        ],
        out_specs=pl.BlockSpec((1, rows, FNET_W), lambda b, j: (b, j, 0)),
        out_shape=jax.ShapeDtypeStruct((bn, s, FNET_W), BF16),
        compiler_params=_cparams(("parallel", "parallel")),
        name="fnet_stage2",
    )(y4, tc, ts, *m2, *m3)


def _fnet_group(zf, bn, s):
    n2 = s // N1
    c2, s2 = _dft_cos_sin(n2)
    m1 = _np_split(np.concatenate([c2, -s2], axis=0))
    c1, s1 = _dft_cos_sin(N1)
    m2 = _np_split(np.block([[c1, s1], [-s1, c1]]))
    cc, sc = _dft_cos_sin(FN_GW)
    eye = np.eye(FN_GROUPS)
    m3 = _np_split(np.concatenate([np.kron(eye, cc), np.kron(eye, sc)], axis=0))
    ang = 2.0 * np.pi * np.outer(np.arange(n2), np.arange(N1)) / s
    tc = jnp.asarray(np.cos(ang)[:, :, None], F32)
    ts = jnp.asarray(np.sin(ang)[:, :, None], F32)

    x2 = zf.reshape(bn, n2, N1 * FNET_W)
    y = _fnet_stage1(x2, *m1)
    y4 = y.reshape(bn, 2, n2 * N1, FNET_W)
    o = _fnet_stage2(y4, tc, ts, m2, m3)
    o = o.reshape(bn, n2, N1, FNET_W).transpose(0, 2, 1, 3)
    return o.reshape(bn * s, FNET_W)


def _merge_kernel(*refs, router):
    (og_ref, yf_ref, ga_ref, gb_ref, x_ref, gt_ref, g2_ref, sc_ref, sh_ref,
     wog_ref, wof_ref, wout_ref) = refs[:12]
    rest = refs[12:]
    ya = _dot(og_ref[...], wog_ref[...])
    yb = _dot(yf_ref[...], wof_ref[...])
    m = ga_ref[...].astype(F32) * ya + gb_ref[...].astype(F32) * yb
    x1 = x_ref[...] + gt_ref[0] * _dot(m.astype(BF16), wout_ref[...])
    h2 = _rms_mod(x1, g2_ref[...], sc_ref[0], sh_ref[0])
    if router:
        wr_ref, x1_ref, h2_ref, lg_ref = rest
        lg_ref[...] = _dot_hi(h2, wr_ref[...])
    else:
        x1_ref, h2_ref = rest
    x1_ref[...] = x1
    h2_ref[...] = h2.astype(BF16)


def _merge(st, og, yf, ga, gb, x, gt, g2, sc, sh, wog, wof, wout, wr=None):
    t = st.t
    row = lambda i: (i, 0)
    bat = lambda i: (st.batch_of_block(i, TM), 0, 0)
    const = lambda i: (0, 0)
    in_specs = [
        pl.BlockSpec((TM, GLA_DV), row), pl.BlockSpec((TM, FNET_W), row),
        pl.BlockSpec((TM, D), row), pl.BlockSpec((TM, D), row), pl.BlockSpec((TM, D), row),
        pl.BlockSpec((1, 1, D), bat), pl.BlockSpec((1, D), const),
        pl.BlockSpec((1, 1, D), bat), pl.BlockSpec((1, 1, D), bat),
        pl.BlockSpec((GLA_DV, D), const), pl.BlockSpec((FNET_W, D), const), pl.BlockSpec((D, D), const),
    ]
    out_specs = [pl.BlockSpec((TM, D), row), pl.BlockSpec((TM, D), row)]
    out_shape = [jax.ShapeDtypeStruct((t, D), F32), jax.ShapeDtypeStruct((t, D), BF16)]
    args = [og, yf, ga, gb, x, gt, g2, sc, sh, wog, wof, wout]
    if wr is not None:
        in_specs.append(pl.BlockSpec((D, LANE), const))
        out_specs.append(pl.BlockSpec((TM, LANE), row))
        out_shape.append(jax.ShapeDtypeStruct((t, LANE), F32))
        args.append(wr)
    return pl.pallas_call(
        functools.partial(_merge_kernel, router=wr is not None),
        grid=(t // TM,),
        in_specs=in_specs, out_specs=out_specs, out_shape=out_shape,
        compiler_params=_cparams(("parallel",)),
        name="merge_router" if wr is not None else "merge",
    )(*args)


def _swiglu_into(acc_ref, h, wg_ref, wu_ref, wd_ref, width, chunk, lead=()):
    for c in range(width // chunk):
        cols = slice(c * chunk, (c + 1) * chunk)
        g = _dot(h, wg_ref[lead + (slice(None), cols)])
        u = _dot(h, wu_ref[lead + (slice(None), cols)])
        a = (g * jax.nn.sigmoid(g) * u).astype(BF16)
        part = _dot(a, wd_ref[lead + (cols, slice(None))])
        if c == 0:
            acc_ref[...] = part
        else:
            acc_ref[...] += part


def _ffn_dense_kernel(h_ref, x_ref, gt_ref, wg_ref, wu_ref, wd_ref, o_ref, acc_ref):
    _swiglu_into(acc_ref, h_ref[...], wg_ref, wu_ref, wd_ref, D_FF, FF_CHUNK_DENSE)
    o_ref[...] = x_ref[...] + gt_ref[0] * acc_ref[...]


def _ffn_dense(st, h, x, gt, wg, wu, wd):
    t = st.t
    row = lambda i: (i, 0)
    const = lambda i: (0, 0)
    once = pl.Buffered(1)
    return pl.pallas_call(
        _ffn_dense_kernel,
        grid=(t // TM,),
        in_specs=[
            pl.BlockSpec((TM, D), row), pl.BlockSpec((TM, D), row),
            pl.BlockSpec((1, 1, D), lambda i: (st.batch_of_block(i, TM), 0, 0)),
            pl.BlockSpec((D, D_FF), const, pipeline_mode=once),
            pl.BlockSpec((D, D_FF), const, pipeline_mode=once),
            pl.BlockSpec((D_FF, D), const, pipeline_mode=once),
        ],
        out_specs=pl.BlockSpec((TM, D), row),
        out_shape=jax.ShapeDtypeStruct((t, D), F32),
        scratch_shapes=[pltpu.VMEM((TM, D), F32)],
        compiler_params=_cparams(("parallel",)),
        name="ffn_dense",
    )(h, x, gt, wg, wu, wd)


def _ffn_expert_kernel(te_ref, tv_ref, x_ref, wg_ref, wu_ref, wd_ref, o_ref, acc_ref):
    k = pl.program_id(0)

    @pl.when(tv_ref[k] == 1)
    def _():
        _swiglu_into(acc_ref, x_ref[...], wg_ref, wu_ref, wd_ref, D_EXP, FF_CHUNK_MOE, lead=(0,))
        o_ref[...] = acc_ref[...].astype(o_ref.dtype)

    @pl.when(tv_ref[k] == 0)
    def _():
        o_ref[...] = jnp.zeros_like(o_ref)


def _ffn_experts(tile_e, tile_v, xp, wg, wu, wd):
    ntiles = xp.shape[0] // MOE_TILE
    once = pl.Buffered(1)
    wmap = lambda k, te, tv: (te[k], 0, 0)
    xmap = lambda k, te, tv: (k * tv[k], 0)
    grid_spec = pltpu.PrefetchScalarGridSpec(
        num_scalar_prefetch=2,
        grid=(ntiles,),
        in_specs=[
            pl.BlockSpec((MOE_TILE, D), xmap),
            pl.BlockSpec((1, D, D_EXP), wmap, pipeline_mode=once),
            pl.BlockSpec((1, D, D_EXP), wmap, pipeline_mode=once),
            pl.BlockSpec((1, D_EXP, D), wmap, pipeline_mode=once),
        ],
        out_specs=pl.BlockSpec((MOE_TILE, D), lambda k, te, tv: (k, 0)),
        scratch_shapes=[pltpu.VMEM((MOE_TILE, D), F32)],
    )
    return pl.pallas_call(
        _ffn_expert_kernel,
        grid_spec=grid_spec,
        out_shape=jax.ShapeDtypeStruct(xp.shape, BF16),
        compiler_params=_cparams(("arbitrary",)),
        name="ffn_experts",
    )(tile_e, tile_v, xp, wg, wu, wd)


def _router_kernel(lg_ref, info_ref, cnt_ref, run_ref):
    @pl.when(pl.program_id(0) == 0)
    def _():
        run_ref[...] = jnp.zeros_like(run_ref)

    lane = lax.broadcasted_iota(jnp.int32, (DISP_BLK, LANE), 1).astype(F32)
    neg = jnp.float32(-jnp.inf)
    lg = jnp.where(lane < N_EXP, lg_ref[...], neg)
    v1 = jnp.max(lg, axis=1, keepdims=True)
    i1 = jnp.min(jnp.where(lg == v1, lane, float(LANE)), axis=1, keepdims=True)
    oh1 = lane == i1
    lg2 = jnp.where(oh1, neg, lg)
    v2 = jnp.max(lg2, axis=1, keepdims=True)
    i2 = jnp.min(jnp.where(lg2 == v2, lane, float(LANE)), axis=1, keepdims=True)
    oh2 = lane == i2
    e = jnp.exp(v2 - v1)
    w1 = 1.0 / (1.0 + e)
    w2 = e / (1.0 + e)
    oh = oh1.astype(F32) + oh2.astype(F32)
    rr = lax.broadcasted_iota(jnp.int32, (DISP_BLK, DISP_BLK), 0)
    cc = lax.broadcasted_iota(jnp.int32, (DISP_BLK, DISP_BLK), 1)
    earlier = (cc < rr).astype(BF16)
    before = _dot(earlier, oh.astype(BF16)) + run_ref[0:1, :]
    pos1 = jnp.sum(jnp.where(oh1, before, 0.0), axis=1, keepdims=True)
    pos2 = jnp.sum(jnp.where(oh2, before, 0.0), axis=1, keepdims=True)
    blk_cnt = jnp.sum(oh, axis=0, keepdims=True)
    cnt_ref[0] = jnp.broadcast_to(blk_cnt, (SUBLANE, LANE))
    run_ref[...] = run_ref[...] + blk_cnt
    packed = jnp.where(lane == 0, i1,
             jnp.where(lane == 1, i2,
             jnp.where(lane == 2, w1,
             jnp.where(lane == 3, w2,
             jnp.where(lane == 4, pos1,
             jnp.where(lane == 5, pos2, 0.0))))))
    info_ref[0] = packed.T[0:SUBLANE, :]


def _router(logits):
    t = logits.shape[0]
    nblk = t // DISP_BLK
    return pl.pallas_call(
        _router_kernel,
        grid=(nblk,),
        in_specs=[pl.BlockSpec((DISP_BLK, LANE), lambda i: (i, 0))],
        out_specs=[pl.BlockSpec((1, SUBLANE, DISP_BLK), lambda i: (i, 0, 0)),
                   pl.BlockSpec((1, SUBLANE, LANE), lambda i: (i, 0, 0))],
        out_shape=[jax.ShapeDtypeStruct((nblk, SUBLANE, DISP_BLK), F32),
                   jax.ShapeDtypeStruct((nblk, SUBLANE, LANE), F32)],
        scratch_shapes=[pltpu.VMEM((SUBLANE, LANE), F32)],
        compiler_params=_cparams(("arbitrary",)),
        name="router",
    )(logits)


def _dispatch_plan(info, cnt):
    nblk = info.shape[0]
    t = nblk * DISP_BLK
    ntiles = (2 * t) // MOE_TILE + N_EXP
    nvis = ntiles + N_EXP * nblk
    e12 = info[:, 0:2, :].astype(jnp.int32)
    w12 = info[:, 2:4, :]
    pos12 = info[:, 4:6, :].astype(jnp.int32)
    bc = cnt[:, 0, :N_EXP].astype(jnp.int32)
    counts = jnp.sum(bc, axis=0)
    padded = (counts + MOE_TILE - 1) // MOE_TILE * MOE_TILE
    pends = jnp.cumsum(padded)
    pstarts = pends - padded
    dest = pstarts[e12] + pos12
    tile_ids = jnp.arange(ntiles, dtype=jnp.int32)
    tile_e = jnp.minimum(jnp.searchsorted(pends, tile_ids * MOE_TILE, side="right"),
                         N_EXP - 1).astype(jnp.int32)
    tile_v = (tile_ids * MOE_TILE < pends[-1]).astype(jnp.int32)

    seg_start = pstarts[None, :] + jnp.cumsum(bc, axis=0) - bc
    seg_end = seg_start + bc
    first_tile = seg_start // MOE_TILE
    nv = jnp.where(bc > 0, (seg_end - 1) // MOE_TILE - first_tile + 1, 0)
    blk_ids = jnp.broadcast_to(jnp.arange(nblk, dtype=jnp.int32)[:, None], (nblk, N_EXP))
    vi = jnp.arange(nvis, dtype=jnp.int32)

    def visits(order):
        nv_f, ft_f, blk_f = order(nv), order(first_tile), order(blk_ids)
        ends = jnp.cumsum(nv_f)
        total = ends[-1]
        valid = vi < total
        seg = jnp.minimum(jnp.searchsorted(ends, vi, side="right"), nv_f.shape[0] - 1)
        tile = ft_f[seg] + (vi - (ends[seg] - nv_f[seg]))
        blk = blk_f[seg]
        last = jnp.maximum(total - 1, 0)
        tile = jnp.where(valid, tile, tile[last]).astype(jnp.int32)
        blk = jnp.where(valid, blk, blk[last]).astype(jnp.int32)
        return tile, blk, valid.astype(jnp.int32)

    g_tile, g_blk, g_valid = visits(lambda a: a.T.reshape(-1))
    c_tile, c_blk, c_valid = visits(lambda a: a.reshape(-1))
    prev = jnp.concatenate([jnp.full((1,), -1, jnp.int32), g_tile[:-1]])
    g_first = (g_tile != prev).astype(jnp.int32)
    prev = jnp.concatenate([jnp.full((1,), -1, jnp.int32), c_blk[:-1]])
    nxt = jnp.concatenate([c_blk[1:], jnp.full((1,), -1, jnp.int32)])
    nxt_valid = jnp.concatenate([c_valid[1:], jnp.zeros((1,), jnp.int32)])
    c_first = ((c_blk != prev) & (c_valid == 1)).astype(jnp.int32)
    c_last = (((c_blk != nxt) | (nxt_valid == 0)) & (c_valid == 1)).astype(jnp.int32)
    return dict(dest=dest, w12=w12, tile_e=tile_e, tile_v=tile_v, ntiles=ntiles,
                gather=(g_tile, g_blk, g_first, g_valid),
                combine=(c_blk, c_tile, c_first, c_last, c_valid))


def _tile_hits(dest_ref, tile):
    rows = tile * MOE_TILE + lax.broadcasted_iota(jnp.int32, (MOE_TILE, DISP_BLK), 0)
    return rows == dest_ref[0, 0:1, :], rows == dest_ref[0, 1:2, :]


def _gather_kernel(vt_ref, vb_ref, vf_ref, vv_ref, h_ref, dest_ref, xp_ref):
    i = pl.program_id(0)
    hit1, hit2 = _tile_hits(dest_ref, vt_ref[i])
    sel = jnp.logical_and(jnp.logical_or(hit1, hit2), vv_ref[i] == 1).astype(BF16)
    rows = _dot(sel, h_ref[...]).astype(BF16)

    @pl.when(vf_ref[i] == 1)
    def _():
        xp_ref[...] = rows

    @pl.when(vf_ref[i] == 0)
    def _():
        xp_ref[...] = xp_ref[...] + rows


def _gather_rows(plan, h):
    vt, vb, vf, vv = plan["gather"]
    nrows = plan["ntiles"] * MOE_TILE
    grid_spec = pltpu.PrefetchScalarGridSpec(
        num_scalar_prefetch=4,
        grid=(vt.shape[0],),
        in_specs=[
            pl.BlockSpec((DISP_BLK, D), lambda i, vt, vb, vf, vv: (vb[i], 0)),
            pl.BlockSpec((1, 2, DISP_BLK), lambda i, vt, vb, vf, vv: (vb[i], 0, 0)),
        ],
        out_specs=pl.BlockSpec((MOE_TILE, D), lambda i, vt, vb, vf, vv: (vt[i], 0)),
    )
    return pl.pallas_call(
        _gather_kernel,
        grid_spec=grid_spec,
        out_shape=jax.ShapeDtypeStruct((nrows, D), BF16),
        compiler_params=_cparams(("arbitrary",)),
        name="moe_gather",
    )(vt, vb, vf, vv, h, plan["dest"])


def _combine_kernel(cb_ref, ct_ref, cf_ref, cl_ref, cv_ref,
                    yp_ref, dest_ref, w_ref, x_ref, gt_ref, gf_ref, o_ref, acc_ref):
    i = pl.program_id(0)

    @pl.when(cf_ref[i] == 1)
    def _():
        acc_ref[...] = jnp.zeros_like(acc_ref)

    @pl.when(cv_ref[i] == 1)
    def _():
        hit1, hit2 = _tile_hits(dest_ref, ct_ref[i])
        wsel = jnp.where(hit1, w_ref[0, 0:1, :], 0.0) + jnp.where(hit2, w_ref[0, 1:2, :], 0.0)
        wh, wl = _split_bf16(wsel)
        tn = (((0,), (0,)), ((), ()))
        yp = yp_ref[...]
        acc_ref[...] += (lax.dot_general(wh, yp, tn, preferred_element_type=F32)
                         + lax.dot_general(wl, yp, tn, preferred_element_type=F32))

    @pl.when(cl_ref[i] == 1)
    def _():
        x2 = x_ref[...] + gt_ref[0] * acc_ref[...]
        ms = jnp.mean(x2 * x2, axis=-1, keepdims=True)
        o_ref[...] = x2 * lax.rsqrt(ms + EPS) * gf_ref[...]


def _combine(st, plan, yp, x, gt, gf):
    cb, ct, cf, cl, cv = plan["combine"]
    blk = lambda i, cb, ct, cf, cl, cv: (cb[i], 0)
    blk3 = lambda i, cb, ct, cf, cl, cv: (cb[i], 0, 0)
    grid_spec = pltpu.PrefetchScalarGridSpec(
        num_scalar_prefetch=5,
        grid=(cb.shape[0],),
        in_specs=[
            pl.BlockSpec((MOE_TILE, D), lambda i, cb, ct, cf, cl, cv: (ct[i], 0)),
            pl.BlockSpec((1, 2, DISP_BLK), blk3),
            pl.BlockSpec((1, 2, DISP_BLK), blk3),
            pl.BlockSpec((DISP_BLK, D), blk),
            pl.BlockSpec((1, 1, D), lambda i, cb, ct, cf, cl, cv: (st.batch_of_block(cb[i], DISP_BLK), 0, 0)),
            pl.BlockSpec((1, D), lambda i, cb, ct, cf, cl, cv: (0, 0)),
        ],
        out_specs=pl.BlockSpec((DISP_BLK, D), blk),
        scratch_shapes=[pltpu.VMEM((DISP_BLK, D), F32)],
    )
    return pl.pallas_call(
        _combine_kernel,
        grid_spec=grid_spec,
        out_shape=jax.ShapeDtypeStruct((st.t, D), F32),
        compiler_params=_cparams(("arbitrary",)),
        name="moe_combine_norm",
    )(cb, ct, cf, cl, cv, yp, plan["dest"], plan["w12"], x, gt, gf)


def _pack_w_in(w):
    o = np.cumsum([0, GLA_DK, GLA_DK, GLA_DV, GLA_DV, RANK, RANK, FNET_W, D, D]).tolist()
    gates = jnp.pad(w[:, o[4]:o[6]], ((0, 0), (0, LANE - 2 * RANK)))
    return jnp.concatenate([w[:, o[0]:o[4]], gates, w[:, o[6]:o[9]]], axis=1).astype(BF16)


def _gate_weights(w_a2, offset):
    return jnp.pad(w_a2, ((offset, LANE - RANK - offset), (0, 0)))


def kernel(x_prompt, x_sample, c_prompt, c_sample, norm1_g, norm2_g, w_ada, b_ada, w_in, w_af2, b_af, w_ab2, b_ab, gla_norm_g, w_o_gla, w_o_fnet, w_out, w_ff_gate, w_ff_up, w_ff_down, w_router, w_e_gate, w_e_up, w_e_down, final_norm_g):
    b, s, _ = x_prompt.shape
    db, ds, _ = x_sample.shape
    depth = w_in.shape[0]
    assert depth == 2 and s % max(GLA_BLK, TM, DISP_BLK, FN_GROUPS_PER_STEP * N1) == 0
    assert ds % max(GLA_BLK, TM, DISP_BLK, FN_GROUPS_PER_STEP * N1) == 0
    st = _Stream(b, s, db, ds)

    x = jnp.concatenate([x_prompt.reshape(b * s, D), x_sample.reshape(db * ds, D)], axis=0)
    c = jnp.concatenate([c_prompt, c_sample], axis=0)
    nb_pad = -(-st.nbatch // SUBLANE) * SUBLANE
    c = jnp.pad(c, ((0, nb_pad - st.nbatch), (0, 0)))
    mod = _modulation(c, w_ada, b_ada)

    def mod_piece(l, j):
        return mod[l, :, j * D:(j + 1) * D].reshape(nb_pad, 1, D)

    for l in range(depth):
        sh1, sc1, gt1, sh2, sc2, gt2 = [mod_piece(l, j) for j in range(N_MOD)]
        q, k, v, r, za, zf, ga, gb = _inproj(st, x, norm1_g[l].reshape(1, D), sc1, sh1, _pack_w_in(w_in[l]))

        wa_f = _gate_weights(w_af2[l], 0)
        wa_b = _gate_weights(w_ab2[l], RANK)
        o_f = _gla_direction(st, False, q, k, v, za, wa_f, b_af[l].reshape(1, GLA_DK))
        o_gla = _gla_direction(st, True, q, k, v, za, wa_b, b_ab[l].reshape(1, GLA_DK),
                               extra=(o_f, r, gla_norm_g[l].reshape(1, DV)))

        yf = jnp.concatenate([_fnet_group(zf[:st.t0], b, s), _fnet_group(zf[st.t0:], db, ds)], axis=0)

        wog, wof, wout = w_o_gla[l].astype(BF16), w_o_fnet[l].astype(BF16), w_out[l].astype(BF16)
        g2 = norm2_g[l].reshape(1, D)
        if l % 2 == 0:
            x1, h2 = _merge(st, o_gla, yf, ga, gb, x, gt1, g2, sc2, sh2, wog, wof, wout)
            x = _ffn_dense(st, h2, x1, gt2, w_ff_gate[l // 2].astype(BF16),
                           w_ff_up[l // 2].astype(BF16), w_ff_down[l // 2].astype(BF16))
        else:
            wr = jnp.pad(w_router[l // 2], ((0, 0), (0, LANE - N_EXP)))
            x1, h2, logits = _merge(st, o_gla, yf, ga, gb, x, gt1, g2, sc2, sh2, wog, wof, wout, wr)
            info, cnt = _router(logits)
            plan = _dispatch_plan(info, cnt)
            xp = _gather_rows(plan, h2)
            yp = _ffn_experts(plan["tile_e"], plan["tile_v"], xp, w_e_gate[l // 2].astype(BF16),
                              w_e_up[l // 2].astype(BF16), w_e_down[l // 2].astype(BF16))
            x = _combine(st, plan, yp, x1, gt2, final_norm_g.reshape(1, D))

    return x[:st.t0].reshape(b, s, D), x[st.t0:].reshape(db, ds, D)
```

```python
import functools

import numpy as np
import jax
import jax.numpy as jnp
from jax import lax
from jax.experimental import pallas as pl
from jax.experimental.pallas import tpu as pltpu

D = 1024
HEADS = 4
DK = 128
DV = 256
GLA_DK = HEADS * DK
GLA_DV = HEADS * DV
RANK = 16
GATE_NORM = 16.0
CHUNK = 64
FN_GROUPS = 4
FN_GW = 128
FNET_W = FN_GROUPS * FN_GW
D_FF = 2816
N_EXP = 8
D_EXP = 3584
EPS = 1e-6
N_MOD = 6

LANE = 128
SUBLANE = 8
VMEM_LIMIT = 56 * 1024 * 1024
TM = 512
GLA_BLK = 512
MOE_TILE = 256
DISP_BLK = 512
FF_CHUNK_DENSE = 1408
FF_CHUNK_MOE = 512
FN_COLS = 4096
FN_GROUPS_PER_STEP = 4
N1 = 128

F32 = jnp.float32
BF16 = jnp.bfloat16
_HI = lax.Precision.HIGHEST


def _cparams(sem):
    return pltpu.CompilerParams(dimension_semantics=sem, vmem_limit_bytes=VMEM_LIMIT)


def _dot(a, b):
    return jnp.dot(a, b, preferred_element_type=F32)


def _dot_hi(a, b):
    return jnp.dot(a, b, precision=_HI, preferred_element_type=F32)


def _split_bf16(x):
    hi = x.astype(BF16)
    lo = (x - hi.astype(F32)).astype(BF16)
    return hi, lo


def _dot3(a_hi, a_lo, b_hi, b_lo):
    return _dot(a_hi, b_hi) + (_dot(a_hi, b_lo) + _dot(a_lo, b_hi))


class _Stream:
    def __init__(self, b, s, db, ds):
        self.b, self.s, self.db, self.ds = b, s, db, ds
        self.t0 = b * s
        self.t = b * s + db * ds
        self.nbatch = b + db

    def batch_of_block(self, i, rows):
        nb0, p0, p1 = self.t0 // rows, self.s // rows, self.ds // rows
        return jnp.where(i < nb0, i // p0, self.b + (i - nb0) // p1)

    def block_in_seq(self, i, rows):
        nb0, p0, p1 = self.t0 // rows, self.s // rows, self.ds // rows
        return jnp.where(i < nb0, i % p0, (i - nb0) % p1), jnp.where(i < nb0, p0, p1)


def _mod_kernel(c_ref, w_ref, b_ref, o_ref):
    c = c_ref[...]
    cs = c * jax.nn.sigmoid(c)
    o_ref[0] = _dot_hi(cs, w_ref[0]) + b_ref[0]


def _modulation(c_pad, w_ada, b_ada):
    depth = w_ada.shape[0]
    nb = c_pad.shape[0]
    return pl.pallas_call(
        _mod_kernel,
        grid=(depth, N_MOD),
        in_specs=[
            pl.BlockSpec((nb, D), lambda l, j: (0, 0)),
            pl.BlockSpec((1, D, D), lambda l, j: (l, 0, j)),
            pl.BlockSpec((1, 1, D), lambda l, j: (l, 0, j)),
        ],
        out_specs=pl.BlockSpec((1, nb, D), lambda l, j: (l, 0, j)),
        out_shape=jax.ShapeDtypeStruct((depth, nb, N_MOD * D), F32),
        compiler_params=_cparams(("arbitrary", "arbitrary")),
        name="adaln_mod",
    )(c_pad, w_ada, b_ada.reshape(depth, 1, N_MOD * D))


def _rms_mod(x, g, sc, sh):
    ms = jnp.mean(x * x, axis=-1, keepdims=True)
    return (x * lax.rsqrt(ms + EPS) * g) * (1.0 + sc) + sh


_C_Q, _C_K, _C_V, _C_R, _C_A, _C_F, _C_GA, _C_GB, _C_END = np.cumsum(
    [0, GLA_DK, GLA_DK, GLA_DV, GLA_DV, LANE, FNET_W, D, D]).tolist()


def _inproj_kernel(x_ref, g_ref, sc_ref, sh_ref, w_ref,
                   q_ref, k_ref, v_ref, r_ref, a_ref, f_ref, ga_ref, gb_ref):
    hb = _rms_mod(x_ref[...], g_ref[...], sc_ref[0], sh_ref[0]).astype(BF16)

    def proj(lo, hi):
        return _dot(hb, w_ref[:, lo:hi])

    q_ref[...] = (proj(_C_Q, _C_K) * (DK ** -0.5)).astype(BF16)
    k_ref[...] = proj(_C_K, _C_V).astype(BF16)
    v_ref[...] = proj(_C_V, _C_R).astype(BF16)
    zr = proj(_C_R, _C_A)
    r_ref[...] = (zr * jax.nn.sigmoid(zr)).astype(BF16)
    a_ref[...] = proj(_C_A, _C_F)
    f_ref[...] = proj(_C_F, _C_GA)
    ga_ref[...] = jax.nn.sigmoid(proj(_C_GA, _C_GB)).astype(BF16)
    gb_ref[...] = jax.nn.sigmoid(proj(_C_GB, _C_END)).astype(BF16)


def _inproj(st, x, g, sc, sh, w):
    t = st.t
    row = lambda i: (i, 0)
    bat = lambda i: (st.batch_of_block(i, TM), 0, 0)
    widths = [(GLA_DK, BF16), (GLA_DK, BF16), (GLA_DV, BF16), (GLA_DV, BF16),
              (LANE, F32), (FNET_W, F32), (D, BF16), (D, BF16)]
    return pl.pallas_call(
        _inproj_kernel,
        grid=(t // TM,),
        in_specs=[
            pl.BlockSpec((TM, D), row),
            pl.BlockSpec((1, D), lambda i: (0, 0)),
            pl.BlockSpec((1, 1, D), bat),
            pl.BlockSpec((1, 1, D), bat),
            pl.BlockSpec((D, _C_END), lambda i: (0, 0)),
        ],
        out_specs=[pl.BlockSpec((TM, w_), row) for w_, _ in widths],
        out_shape=[jax.ShapeDtypeStruct((t, w_), dt) for w_, dt in widths],
        compiler_params=_cparams(("parallel",)),
        name="inproj",
    )(x, g, sc, sh, w)


def _log_sigmoid(x):
    return jnp.minimum(x, 0.0) - jnp.log1p(jnp.exp(-jnp.abs(x)))


def _gla_kernel(*refs, bwd, st):
    if bwd:
        (q_ref, k_ref, v_ref, za_ref, wah_ref, wal_ref, ba_ref, of_ref, r_ref, ng_ref,
         o_ref, s_scr, qi_scr, ki_scr, qin_scr, kout_scr, att_scr, kv_scr) = refs
    else:
        (q_ref, k_ref, v_ref, za_ref, wah_ref, wal_ref, ba_ref,
         o_ref, s_scr, qi_scr, ki_scr, qin_scr, kout_scr, att_scr, kv_scr) = refs
    i = pl.program_id(0)
    nblk = pl.num_programs(0)
    blk = (nblk - 1 - i) if bwd else i
    local, per = st.block_in_seq(blk, GLA_BLK)
    at_boundary = (local == per - 1) if bwd else (local == 0)

    @pl.when(at_boundary)
    def _():
        s_scr[...] = jnp.zeros_like(s_scr)

    nch = GLA_BLK // CHUNK
    rr = lax.broadcasted_iota(jnp.int32, (CHUNK, CHUNK), 0)
    cc = lax.broadcasted_iota(jnp.int32, (CHUNK, CHUNK), 1)
    keep = (rr <= cc) if bwd else (rr >= cc)
    tri = keep.astype(BF16)
    last = 0 if bwd else CHUNK - 1
    mid = CHUNK // 2 if bwd else CHUNK // 2 - 1

    zh, zl = _split_bf16(za_ref[...])
    g = _log_sigmoid(_dot3(zh, zl, wah_ref[...], wal_ref[...]) + ba_ref[...]) * (1.0 / GATE_NORM)
    g0 = g.astype(BF16)
    rem = g - g0.astype(F32)
    g1 = rem.astype(BF16)
    g2 = (rem - g1.astype(F32)).astype(BF16)

    def sums(m, rows=slice(None)):
        return _dot(m, g0[rows]) + (_dot(m, g1[rows]) + _dot(m, g2[rows]))

    cid = lax.broadcasted_iota(jnp.int32, (LANE, GLA_BLK), 0)
    pos = lax.broadcasted_iota(jnp.int32, (LANE, GLA_BLK), 1)
    in_chunk = (pos // CHUNK == cid).astype(BF16)
    chunk_decay = jnp.exp(sums(in_chunk).T)

    for c in range(nch):
        rows = slice(c * CHUNK, (c + 1) * CHUNK)
        b = sums(tri, rows)
        b_last = b[last:last + 1, :]
        b_ref = b[mid:mid + 1, :]
        q = q_ref[rows, :].astype(F32)
        k = k_ref[rows, :].astype(F32)
        qi_scr[rows, :] = (q * jnp.exp(b - b_ref)).astype(BF16)
        ki_scr[rows, :] = (k * jnp.exp(b_ref - b)).astype(BF16)
        qin_scr[rows, :] = (q * jnp.exp(b)).astype(BF16)
        kout_scr[rows, :] = (k * jnp.exp(b_last - b)).astype(BF16)

    for c in range(nch):
        rows = slice(c * CHUNK, (c + 1) * CHUNK)
        for h in range(HEADS):
            ks = slice(h * DK, (h + 1) * DK)
            att = lax.dot_general(qi_scr[rows, ks], ki_scr[rows, ks], (((1,), (1,)), ((), ())),
                                  preferred_element_type=F32)
            att_scr[c * HEADS + h] = jnp.where(keep, att, 0.0).astype(BF16)
            kv_scr[c * HEADS + h] = lax.dot_general(
                kout_scr[rows, ks], v_ref[rows, h * DV:(h + 1) * DV], (((0,), (0,)), ((), ())),
                preferred_element_type=F32)

    state = [s_scr[h] for h in range(HEADS)]
    for c in (reversed(range(nch)) if bwd else range(nch)):
        rows = slice(c * CHUNK, (c + 1) * CHUNK)
        for h in range(HEADS):
            ks = slice(h * DK, (h + 1) * DK)
            vs = slice(h * DV, (h + 1) * DV)
            o = _dot(jnp.concatenate([qin_scr[rows, ks], att_scr[c * HEADS + h]], axis=1),
                     jnp.concatenate([state[h].astype(BF16), v_ref[rows, vs]], axis=0))
            state[h] = state[h] * chunk_decay[ks, c:c + 1] + kv_scr[c * HEADS + h]
            if bwd:
                o = o + of_ref[rows, vs]
                o = o * lax.rsqrt(jnp.mean(o * o, axis=-1, keepdims=True) + EPS) * ng_ref[...]
                o_ref[rows, vs] = (o * r_ref[rows, vs].astype(F32)).astype(o_ref.dtype)
            else:
                o_ref[rows, vs] = o
    for h in range(HEADS):
        s_scr[h] = state[h]


def _gla_direction(st, bwd, q, k, v, za, wa, ba, extra=()):
    t = st.t
    nblk = t // GLA_BLK
    row = (lambda i: (nblk - 1 - i, 0)) if bwd else (lambda i: (i, 0))
    const = lambda i: (0, 0)
    in_specs = [
        pl.BlockSpec((GLA_BLK, GLA_DK), row),
        pl.BlockSpec((GLA_BLK, GLA_DK), row),
        pl.BlockSpec((GLA_BLK, GLA_DV), row),
        pl.BlockSpec((GLA_BLK, LANE), row),
        pl.BlockSpec((LANE, GLA_DK), const),
        pl.BlockSpec((LANE, GLA_DK), const),
        pl.BlockSpec((1, GLA_DK), const),
    ]
    if bwd:
        in_specs += [pl.BlockSpec((GLA_BLK, GLA_DV), row), pl.BlockSpec((GLA_BLK, GLA_DV), row),
                     pl.BlockSpec((1, DV), const)]
    scaled = pltpu.VMEM((GLA_BLK, GLA_DK), BF16)
    return pl.pallas_call(
        functools.partial(_gla_kernel, bwd=bwd, st=st),
        grid=(nblk,),
        in_specs=in_specs,
        out_specs=pl.BlockSpec((GLA_BLK, GLA_DV), row),
        out_shape=jax.ShapeDtypeStruct((t, GLA_DV), BF16 if bwd else F32),
        scratch_shapes=[pltpu.VMEM((HEADS, DK, DV), F32), scaled, scaled, scaled, scaled,
                        pltpu.VMEM((GLA_BLK // CHUNK * HEADS, CHUNK, CHUNK), BF16),
                        pltpu.VMEM((GLA_BLK // CHUNK * HEADS, DK, DV), F32)],
        compiler_params=_cparams(("arbitrary",)),
        name="gla_bwd" if bwd else "gla_fwd",
    )(q, k, v, za, *_split_bf16(wa), ba, *extra)


def _np_split(m):
    m = np.asarray(m, np.float32)
    hi = m.astype(BF16)
    lo = (m - hi.astype(np.float32)).astype(BF16)
    return jnp.asarray(hi), jnp.asarray(lo)


def _dft_cos_sin(n):
    idx = np.arange(n, dtype=np.float64)
    ang = 2.0 * np.pi * np.outer(idx, idx) / n
    return np.cos(ang) / np.sqrt(n), np.sin(ang) / np.sqrt(n)


def _fnet_stage1_kernel(x_ref, mh_ref, ml_ref, y_ref):
    xh, xl = _split_bf16(x_ref[0])
    y_ref[0] = _dot3(mh_ref[...], ml_ref[...], xh, xl)


def _fnet_stage1(x2, mh, ml):
    bn, n2, cols = x2.shape
    return pl.pallas_call(
        _fnet_stage1_kernel,
        grid=(bn, cols // FN_COLS),
        in_specs=[
            pl.BlockSpec((1, n2, FN_COLS), lambda b, j: (b, 0, j)),
            pl.BlockSpec((2 * n2, n2), lambda b, j: (0, 0)),
            pl.BlockSpec((2 * n2, n2), lambda b, j: (0, 0)),
        ],
        out_specs=pl.BlockSpec((1, 2 * n2, FN_COLS), lambda b, j: (b, 0, j)),
        out_shape=jax.ShapeDtypeStruct((bn, 2 * n2, cols), F32),
        compiler_params=_cparams(("parallel", "parallel")),
        name="fnet_stage1",
    )(x2, mh, ml)


def _fnet_stage2_kernel(y_ref, tc_ref, ts_ref, m2h_ref, m2l_ref, m3h_ref, m3l_ref, o_ref):
    for g in range(FN_GROUPS_PER_STEP):
        rows = slice(g * N1, (g + 1) * N1)
        yr = y_ref[0, 0, rows, :]
        yi = y_ref[0, 1, rows, :]
        tc = tc_ref[g]
        ts = ts_ref[g]
        stack = jnp.concatenate([yr * tc + yi * ts, yi * tc - yr * ts], axis=0)
        sh, sl = _split_bf16(stack)
        z = _dot3(m2h_ref[...], m2l_ref[...], sh, sl)
        zcat = jnp.concatenate([z[:N1], z[N1:]], axis=1)
        zh, zl = _split_bf16(zcat)
        o_ref[0, rows, :] = _dot3(zh, zl, m3h_ref[...], m3l_ref[...]).astype(o_ref.dtype)


def _fnet_stage2(y4, tc, ts, m2, m3):
    bn, _, s, _ = y4.shape
    rows = FN_GROUPS_PER_STEP * N1
    const = lambda b, j: (0, 0)
    return pl.pallas_call(
        _fnet_stage2_kernel,
        grid=(bn, s // rows),
        in_specs=[
            pl.BlockSpec((1, 2, rows, FNET_W), lambda b, j: (b, 0, j, 0)),
            pl.BlockSpec((FN_GROUPS_PER_STEP, N1, 1), lambda b, j: (j, 0, 0)),
            pl.BlockSpec((FN_GROUPS_PER_STEP, N1, 1), lambda b, j: (j, 0, 0)),
            pl.BlockSpec((2 * N1, 2 * N1), const),
            pl.BlockSpec((2 * N1, 2 * N1), const),
            pl.BlockSpec((2 * FNET_W, FNET_W), const),
            pl.BlockSpec((2 * FNET_W, FNET_W), const),
        ],
        out_specs=pl.BlockSpec((1, rows, FNET_W), lambda b, j: (b, j, 0)),
        out_shape=jax.ShapeDtypeStruct((bn, s, FNET_W), BF16),
        compiler_params=_cparams(("parallel", "parallel")),
        name="fnet_stage2",
    )(y4, tc, ts, *m2, *m3)


def _fnet_group(zf, bn, s):
    n2 = s // N1
    c2, s2 = _dft_cos_sin(n2)
    m1 = _np_split(np.concatenate([c2, -s2], axis=0))
    c1, s1 = _dft_cos_sin(N1)
    m2 = _np_split(np.block([[c1, s1], [-s1, c1]]))
    cc, sc = _dft_cos_sin(FN_GW)
    eye = np.eye(FN_GROUPS)
    m3 = _np_split(np.concatenate([np.kron(eye, cc), np.kron(eye, sc)], axis=0))
    ang = 2.0 * np.pi * np.outer(np.arange(n2), np.arange(N1)) / s
    tc = jnp.asarray(np.cos(ang)[:, :, None], F32)
    ts = jnp.asarray(np.sin(ang)[:, :, None], F32)

    x2 = zf.reshape(bn, n2, N1 * FNET_W)
    y = _fnet_stage1(x2, *m1)
    y4 = y.reshape(bn, 2, n2 * N1, FNET_W)
    o = _fnet_stage2(y4, tc, ts, m2, m3)
    o = o.reshape(bn, n2, N1, FNET_W).transpose(0, 2, 1, 3)
    return o.reshape(bn * s, FNET_W)


def _merge_kernel(*refs, router):
    (og_ref, yf_ref, ga_ref, gb_ref, x_ref, gt_ref, g2_ref, sc_ref, sh_ref,
     wog_ref, wof_ref, wout_ref) = refs[:12]
    rest = refs[12:]
    ya = _dot(og_ref[...], wog_ref[...])
    yb = _dot(yf_ref[...], wof_ref[...])
    m = ga_ref[...].astype(F32) * ya + gb_ref[...].astype(F32) * yb
    x1 = x_ref[...] + gt_ref[0] * _dot(m.astype(BF16), wout_ref[...])
    h2 = _rms_mod(x1, g2_ref[...], sc_ref[0], sh_ref[0])
    if router:
        wrh_ref, wrl_ref, x1_ref, h2_ref, lg_ref = rest
        lg_ref[...] = _dot3(*_split_bf16(h2), wrh_ref[...], wrl_ref[...])
    else:
        x1_ref, h2_ref = rest
    x1_ref[...] = x1
    h2_ref[...] = h2.astype(BF16)


def _merge(st, og, yf, ga, gb, x, gt, g2, sc, sh, wog, wof, wout, wr=None):
    t = st.t
    row = lambda i: (i, 0)
    bat = lambda i: (st.batch_of_block(i, TM), 0, 0)
    const = lambda i: (0, 0)
    in_specs = [
        pl.BlockSpec((TM, GLA_DV), row), pl.BlockSpec((TM, FNET_W), row),
        pl.BlockSpec((TM, D), row), pl.BlockSpec((TM, D), row), pl.BlockSpec((TM, D), row),
        pl.BlockSpec((1, 1, D), bat), pl.BlockSpec((1, D), const),
        pl.BlockSpec((1, 1, D), bat), pl.BlockSpec((1, 1, D), bat),
        pl.BlockSpec((GLA_DV, D), const), pl.BlockSpec((FNET_W, D), const), pl.BlockSpec((D, D), const),
    ]
    out_specs = [pl.BlockSpec((TM, D), row), pl.BlockSpec((TM, D), row)]
    out_shape = [jax.ShapeDtypeStruct((t, D), F32), jax.ShapeDtypeStruct((t, D), BF16)]
    args = [og, yf, ga, gb, x, gt, g2, sc, sh, wog, wof, wout]
    if wr is not None:
        in_specs += [pl.BlockSpec((D, LANE), const), pl.BlockSpec((D, LANE), const)]
        out_specs.append(pl.BlockSpec((TM, LANE), row))
        out_shape.append(jax.ShapeDtypeStruct((t, LANE), F32))
        args += list(_split_bf16(wr))
    return pl.pallas_call(
        functools.partial(_merge_kernel, router=wr is not None),
        grid=(t // TM,),
        in_specs=in_specs, out_specs=out_specs, out_shape=out_shape,
        compiler_params=_cparams(("parallel",)),
        name="merge_router" if wr is not None else "merge",
    )(*args)


def _swiglu_into(acc_ref, h, wg_ref, wu_ref, wd_ref, width, chunk, lead=()):
    for c in range(width // chunk):
        cols = slice(c * chunk, (c + 1) * chunk)
        g = _dot(h, wg_ref[lead + (slice(None), cols)])
        u = _dot(h, wu_ref[lead + (slice(None), cols)])
        a = (g * jax.nn.sigmoid(g) * u).astype(BF16)
        part = _dot(a, wd_ref[lead + (cols, slice(None))])
        if c == 0:
            acc_ref[...] = part
        else:
            acc_ref[...] += part


def _ffn_dense_kernel(h_ref, x_ref, gt_ref, wg_ref, wu_ref, wd_ref, o_ref, acc_ref):
    _swiglu_into(acc_ref, h_ref[...], wg_ref, wu_ref, wd_ref, D_FF, FF_CHUNK_DENSE)
    o_ref[...] = x_ref[...] + gt_ref[0] * acc_ref[...]


def _ffn_dense(st, h, x, gt, wg, wu, wd):
    t = st.t
    row = lambda i: (i, 0)
    const = lambda i: (0, 0)
    once = pl.Buffered(1)
    return pl.pallas_call(
        _ffn_dense_kernel,
        grid=(t // TM,),
        in_specs=[
            pl.BlockSpec((TM, D), row), pl.BlockSpec((TM, D), row),
            pl.BlockSpec((1, 1, D), lambda i: (st.batch_of_block(i, TM), 0, 0)),
            pl.BlockSpec((D, D_FF), const, pipeline_mode=once),
            pl.BlockSpec((D, D_FF), const, pipeline_mode=once),
            pl.BlockSpec((D_FF, D), const, pipeline_mode=once),
        ],
        out_specs=pl.BlockSpec((TM, D), row),
        out_shape=jax.ShapeDtypeStruct((t, D), F32),
        scratch_shapes=[pltpu.VMEM((TM, D), F32)],
        compiler_params=_cparams(("parallel",)),
        name="ffn_dense",
    )(h, x, gt, wg, wu, wd)


def _ffn_expert_kernel(te_ref, tv_ref, x_ref, wg_ref, wu_ref, wd_ref, o_ref, acc_ref):
    k = pl.program_id(0)

    @pl.when(tv_ref[k] == 1)
    def _():
        _swiglu_into(acc_ref, x_ref[...], wg_ref, wu_ref, wd_ref, D_EXP, FF_CHUNK_MOE, lead=(0,))
        o_ref[...] = acc_ref[...].astype(o_ref.dtype)

    @pl.when(tv_ref[k] == 0)
    def _():
        o_ref[...] = jnp.zeros_like(o_ref)


def _ffn_experts(tile_e, tile_v, xp, wg, wu, wd):
    ntiles = xp.shape[0] // MOE_TILE
    once = pl.Buffered(1)
    wmap = lambda k, te, tv: (te[k], 0, 0)
    xmap = lambda k, te, tv: (k * tv[k], 0)
    grid_spec = pltpu.PrefetchScalarGridSpec(
        num_scalar_prefetch=2,
        grid=(ntiles,),
        in_specs=[
            pl.BlockSpec((MOE_TILE, D), xmap),
            pl.BlockSpec((1, D, D_EXP), wmap, pipeline_mode=once),
            pl.BlockSpec((1, D, D_EXP), wmap, pipeline_mode=once),
            pl.BlockSpec((1, D_EXP, D), wmap, pipeline_mode=once),
        ],
        out_specs=pl.BlockSpec((MOE_TILE, D), lambda k, te, tv: (k, 0)),
        scratch_shapes=[pltpu.VMEM((MOE_TILE, D), F32)],
    )
    return pl.pallas_call(
        _ffn_expert_kernel,
        grid_spec=grid_spec,
        out_shape=jax.ShapeDtypeStruct(xp.shape, BF16),
        compiler_params=_cparams(("arbitrary",)),
        name="ffn_experts",
    )(tile_e, tile_v, xp, wg, wu, wd)


def _router_kernel(lg_ref, info_ref, cnt_ref, run_ref):
    @pl.when(pl.program_id(0) == 0)
    def _():
        run_ref[...] = jnp.zeros_like(run_ref)

    lane = lax.broadcasted_iota(jnp.int32, (DISP_BLK, LANE), 1).astype(F32)
    neg = jnp.float32(-jnp.inf)
    lg = jnp.where(lane < N_EXP, lg_ref[...], neg)
    v1 = jnp.max(lg, axis=1, keepdims=True)
    i1 = jnp.min(jnp.where(lg == v1, lane, float(LANE)), axis=1, keepdims=True)
    oh1 = lane == i1
    lg2 = jnp.where(oh1, neg, lg)
    v2 = jnp.max(lg2, axis=1, keepdims=True)
    i2 = jnp.min(jnp.where(lg2 == v2, lane, float(LANE)), axis=1, keepdims=True)
    oh2 = lane == i2
    e = jnp.exp(v2 - v1)
    w1 = 1.0 / (1.0 + e)
    w2 = e / (1.0 + e)
    oh = oh1.astype(F32) + oh2.astype(F32)
    rr = lax.broadcasted_iota(jnp.int32, (DISP_BLK, DISP_BLK), 0)
    cc = lax.broadcasted_iota(jnp.int32, (DISP_BLK, DISP_BLK), 1)
    earlier = (cc < rr).astype(BF16)
    before = _dot(earlier, oh.astype(BF16)) + run_ref[0:1, :]
    pos1 = jnp.sum(jnp.where(oh1, before, 0.0), axis=1, keepdims=True)
    pos2 = jnp.sum(jnp.where(oh2, before, 0.0), axis=1, keepdims=True)
    blk_cnt = jnp.sum(oh, axis=0, keepdims=True)
    cnt_ref[0] = jnp.broadcast_to(blk_cnt, (SUBLANE, LANE))
    run_ref[...] = run_ref[...] + blk_cnt
    packed = jnp.where(lane == 0, i1,
             jnp.where(lane == 1, i2,
             jnp.where(lane == 2, w1,
             jnp.where(lane == 3, w2,
             jnp.where(lane == 4, pos1,
             jnp.where(lane == 5, pos2, 0.0))))))
    info_ref[0] = packed.T[0:SUBLANE, :]


def _router(logits):
    t = logits.shape[0]
    nblk = t // DISP_BLK
    return pl.pallas_call(
        _router_kernel,
        grid=(nblk,),
        in_specs=[pl.BlockSpec((DISP_BLK, LANE), lambda i: (i, 0))],
        out_specs=[pl.BlockSpec((1, SUBLANE, DISP_BLK), lambda i: (i, 0, 0)),
                   pl.BlockSpec((1, SUBLANE, LANE), lambda i: (i, 0, 0))],
        out_shape=[jax.ShapeDtypeStruct((nblk, SUBLANE, DISP_BLK), F32),
                   jax.ShapeDtypeStruct((nblk, SUBLANE, LANE), F32)],
        scratch_shapes=[pltpu.VMEM((SUBLANE, LANE), F32)],
        compiler_params=_cparams(("arbitrary",)),
        name="router",
    )(logits)


def _dispatch_plan(info, cnt):
    nblk = info.shape[0]
    t = nblk * DISP_BLK
    ntiles = (2 * t) // MOE_TILE + N_EXP
    nvis = ntiles + N_EXP * nblk
    e12 = info[:, 0:2, :].astype(jnp.int32)
    w12 = info[:, 2:4, :]
    pos12 = info[:, 4:6, :].astype(jnp.int32)
    bc = cnt[:, 0, :N_EXP].astype(jnp.int32)
    counts = jnp.sum(bc, axis=0)
    padded = (counts + MOE_TILE - 1) // MOE_TILE * MOE_TILE
    pends = jnp.cumsum(padded)
    pstarts = pends - padded
    dest = pstarts[e12] + pos12
    tile_ids = jnp.arange(ntiles, dtype=jnp.int32)
    tile_e = jnp.minimum(jnp.searchsorted(pends, tile_ids * MOE_TILE, side="right"),
                         N_EXP - 1).astype(jnp.int32)
    tile_v = (tile_ids * MOE_TILE < pends[-1]).astype(jnp.int32)

    seg_start = pstarts[None, :] + jnp.cumsum(bc, axis=0) - bc
    seg_end = seg_start + bc
    first_tile = seg_start // MOE_TILE
    nv = jnp.where(bc > 0, (seg_end - 1) // MOE_TILE - first_tile + 1, 0)
    blk_ids = jnp.broadcast_to(jnp.arange(nblk, dtype=jnp.int32)[:, None], (nblk, N_EXP))
    vi = jnp.arange(nvis, dtype=jnp.int32)

    def visits(order):
        nv_f, ft_f, blk_f = order(nv), order(first_tile), order(blk_ids)
        ends = jnp.cumsum(nv_f)
        total = ends[-1]
        valid = vi < total
        seg = jnp.minimum(jnp.searchsorted(ends, vi, side="right"), nv_f.shape[0] - 1)
        tile = ft_f[seg] + (vi - (ends[seg] - nv_f[seg]))
        blk = blk_f[seg]
        last = jnp.maximum(total - 1, 0)
        tile = jnp.where(valid, tile, tile[last]).astype(jnp.int32)
        blk = jnp.where(valid, blk, blk[last]).astype(jnp.int32)
        return tile, blk, valid.astype(jnp.int32)

    g_tile, g_blk, g_valid = visits(lambda a: a.T.reshape(-1))
    c_tile, c_blk, c_valid = visits(lambda a: a.reshape(-1))
    prev = jnp.concatenate([jnp.full((1,), -1, jnp.int32), g_tile[:-1]])
    g_first = (g_tile != prev).astype(jnp.int32)
    prev = jnp.concatenate([jnp.full((1,), -1, jnp.int32), c_blk[:-1]])
    nxt = jnp.concatenate([c_blk[1:], jnp.full((1,), -1, jnp.int32)])
    nxt_valid = jnp.concatenate([c_valid[1:], jnp.zeros((1,), jnp.int32)])
    c_first = ((c_blk != prev) & (c_valid == 1)).astype(jnp.int32)
    c_last = (((c_blk != nxt) | (nxt_valid == 0)) & (c_valid == 1)).astype(jnp.int32)
    return dict(dest=dest, w12=w12, tile_e=tile_e, tile_v=tile_v, ntiles=ntiles,
                gather=(g_tile, g_blk, g_first, g_valid),
                combine=(c_blk, c_tile, c_first, c_last, c_valid))


def _tile_hits(dest_ref, tile):
    rows = tile * MOE_TILE + lax.broadcasted_iota(jnp.int32, (MOE_TILE, DISP_BLK), 0)
    return rows == dest_ref[0, 0:1, :], rows == dest_ref[0, 1:2, :]


def _gather_kernel(vt_ref, vb_ref, vf_ref, vv_ref, h_ref, dest_ref, xp_ref):
    i = pl.program_id(0)
    hit1, hit2 = _tile_hits(dest_ref, vt_ref[i])
    sel = jnp.logical_and(jnp.logical_or(hit1, hit2), vv_ref[i] == 1).astype(BF16)
    rows = _dot(sel, h_ref[...]).astype(BF16)

    @pl.when(vf_ref[i] == 1)
    def _():
        xp_ref[...] = rows

    @pl.when(vf_ref[i] == 0)
    def _():
        xp_ref[...] = xp_ref[...] + rows


def _gather_rows(plan, h):
    vt, vb, vf, vv = plan["gather"]
    nrows = plan["ntiles"] * MOE_TILE
    grid_spec = pltpu.PrefetchScalarGridSpec(
        num_scalar_prefetch=4,
        grid=(vt.shape[0],),
        in_specs=[
            pl.BlockSpec((DISP_BLK, D), lambda i, vt, vb, vf, vv: (vb[i], 0)),
            pl.BlockSpec((1, 2, DISP_BLK), lambda i, vt, vb, vf, vv: (vb[i], 0, 0)),
        ],
        out_specs=pl.BlockSpec((MOE_TILE, D), lambda i, vt, vb, vf, vv: (vt[i], 0)),
    )
    return pl.pallas_call(
        _gather_kernel,
        grid_spec=grid_spec,
        out_shape=jax.ShapeDtypeStruct((nrows, D), BF16),
        compiler_params=_cparams(("arbitrary",)),
        name="moe_gather",
    )(vt, vb, vf, vv, h, plan["dest"])


def _combine_kernel(cb_ref, ct_ref, cf_ref, cl_ref, cv_ref,
                    yp_ref, dest_ref, w_ref, x_ref, gt_ref, gf_ref, o_ref, acc_ref):
    i = pl.program_id(0)

    @pl.when(cf_ref[i] == 1)
    def _():
        acc_ref[...] = jnp.zeros_like(acc_ref)

    @pl.when(cv_ref[i] == 1)
    def _():
        hit1, hit2 = _tile_hits(dest_ref, ct_ref[i])
        wsel = jnp.where(hit1, w_ref[0, 0:1, :], 0.0) + jnp.where(hit2, w_ref[0, 1:2, :], 0.0)
        wh, wl = _split_bf16(wsel)
        tn = (((0,), (0,)), ((), ()))
        yp = yp_ref[...]
        acc_ref[...] += (lax.dot_general(wh, yp, tn, preferred_element_type=F32)
                         + lax.dot_general(wl, yp, tn, preferred_element_type=F32))

    @pl.when(cl_ref[i] == 1)
    def _():
        x2 = x_ref[...] + gt_ref[0] * acc_ref[...]
        ms = jnp.mean(x2 * x2, axis=-1, keepdims=True)
        o_ref[...] = x2 * lax.rsqrt(ms + EPS) * gf_ref[...]


def _combine(st, plan, yp, x, gt, gf):
    cb, ct, cf, cl, cv = plan["combine"]
    blk = lambda i, cb, ct, cf, cl, cv: (cb[i], 0)
    blk3 = lambda i, cb, ct, cf, cl, cv: (cb[i], 0, 0)
    grid_spec = pltpu.PrefetchScalarGridSpec(
        num_scalar_prefetch=5,
        grid=(cb.shape[0],),
        in_specs=[
            pl.BlockSpec((MOE_TILE, D), lambda i, cb, ct, cf, cl, cv: (ct[i], 0)),
            pl.BlockSpec((1, 2, DISP_BLK), blk3),
            pl.BlockSpec((1, 2, DISP_BLK), blk3),
            pl.BlockSpec((DISP_BLK, D), blk),
            pl.BlockSpec((1, 1, D), lambda i, cb, ct, cf, cl, cv: (st.batch_of_block(cb[i], DISP_BLK), 0, 0)),
            pl.BlockSpec((1, D), lambda i, cb, ct, cf, cl, cv: (0, 0)),
        ],
        out_specs=pl.BlockSpec((DISP_BLK, D), blk),
        scratch_shapes=[pltpu.VMEM((DISP_BLK, D), F32)],
    )
    return pl.pallas_call(
        _combine_kernel,
        grid_spec=grid_spec,
        out_shape=jax.ShapeDtypeStruct((st.t, D), F32),
        compiler_params=_cparams(("arbitrary",)),
        name="moe_combine_norm",
    )(cb, ct, cf, cl, cv, yp, plan["dest"], plan["w12"], x, gt, gf)


def _pack_w_in(w):
    o = np.cumsum([0, GLA_DK, GLA_DK, GLA_DV, GLA_DV, RANK, RANK, FNET_W, D, D]).tolist()
    gates = jnp.pad(w[:, o[4]:o[6]], ((0, 0), (0, LANE - 2 * RANK)))
    return jnp.concatenate([w[:, o[0]:o[4]], gates, w[:, o[6]:o[9]]], axis=1).astype(BF16)


def _gate_weights(w_a2, offset):
    return jnp.pad(w_a2, ((offset, LANE - RANK - offset), (0, 0)))


def kernel(x_prompt, x_sample, c_prompt, c_sample, norm1_g, norm2_g, w_ada, b_ada, w_in, w_af2, b_af, w_ab2, b_ab, gla_norm_g, w_o_gla, w_o_fnet, w_out, w_ff_gate, w_ff_up, w_ff_down, w_router, w_e_gate, w_e_up, w_e_down, final_norm_g):
    b, s, _ = x_prompt.shape
    db, ds, _ = x_sample.shape
    depth = w_in.shape[0]
    assert depth == 2 and s % max(GLA_BLK, TM, DISP_BLK, FN_GROUPS_PER_STEP * N1) == 0
    assert ds % max(GLA_BLK, TM, DISP_BLK, FN_GROUPS_PER_STEP * N1) == 0
    st = _Stream(b, s, db, ds)

    x = jnp.concatenate([x_prompt.reshape(b * s, D), x_sample.reshape(db * ds, D)], axis=0)
    c = jnp.concatenate([c_prompt, c_sample], axis=0)
    nb_pad = -(-st.nbatch // SUBLANE) * SUBLANE
    c = jnp.pad(c, ((0, nb_pad - st.nbatch), (0, 0)))
    mod = _modulation(c, w_ada, b_ada)

    def mod_piece(l, j):
        return mod[l, :, j * D:(j + 1) * D].reshape(nb_pad, 1, D)

    for l in range(depth):
        sh1, sc1, gt1, sh2, sc2, gt2 = [mod_piece(l, j) for j in range(N_MOD)]
        q, k, v, r, za, zf, ga, gb = _inproj(st, x, norm1_g[l].reshape(1, D), sc1, sh1, _pack_w_in(w_in[l]))

        wa_f = _gate_weights(w_af2[l], 0)
        wa_b = _gate_weights(w_ab2[l], RANK)
        o_f = _gla_direction(st, False, q, k, v, za, wa_f, b_af[l].reshape(1, GLA_DK))
        o_gla = _gla_direction(st, True, q, k, v, za, wa_b, b_ab[l].reshape(1, GLA_DK),
                               extra=(o_f, r, gla_norm_g[l].reshape(1, DV)))

        yf = jnp.concatenate([_fnet_group(zf[:st.t0], b, s), _fnet_group(zf[st.t0:], db, ds)], axis=0)

        wog, wof, wout = w_o_gla[l].astype(BF16), w_o_fnet[l].astype(BF16), w_out[l].astype(BF16)
        g2 = norm2_g[l].reshape(1, D)
        if l % 2 == 0:
            x1, h2 = _merge(st, o_gla, yf, ga, gb, x, gt1, g2, sc2, sh2, wog, wof, wout)
            x = _ffn_dense(st, h2, x1, gt2, w_ff_gate[l // 2].astype(BF16),
                           w_ff_up[l // 2].astype(BF16), w_ff_down[l // 2].astype(BF16))
        else:
            wr = jnp.pad(w_router[l // 2], ((0, 0), (0, LANE - N_EXP)))
            x1, h2, logits = _merge(st, o_gla, yf, ga, gb, x, gt1, g2, sc2, sh2, wog, wof, wout, wr)
            info, cnt = _router(logits)
            plan = _dispatch_plan(info, cnt)
            xp = _gather_rows(plan, h2)
            yp = _ffn_experts(plan["tile_e"], plan["tile_v"], xp, w_e_gate[l // 2].astype(BF16),
                              w_e_up[l // 2].astype(BF16), w_e_down[l // 2].astype(BF16))
            x = _combine(st, plan, yp, x1, gt2, final_norm_g.reshape(1, D))

    return x[:st.t0].reshape(b, s, D), x[st.t0:].reshape(db, ds, D)
```

```python
import functools

import numpy as np
import jax
import jax.numpy as jnp
from jax import lax
from jax.experimental import pallas as pl
from jax.experimental.pallas import tpu as pltpu

D = 1024
HEADS = 4
DK = 128
DV = 256
GLA_DK = HEADS * DK
GLA_DV = HEADS * DV
RANK = 16
GATE_NORM = 16.0
CHUNK = 64
FN_GROUPS = 4
FN_GW = 128
FNET_W = FN_GROUPS * FN_GW
D_FF = 2816
N_EXP = 8
D_EXP = 3584
EPS = 1e-6
N_MOD = 6

LANE = 128
SUBLANE = 8
VMEM_LIMIT = 56 * 1024 * 1024
TM = 512
GLA_BLK = 512
MOE_TILE = 256
DISP_BLK = 512
SEG_ALIGN = 16
LROWS = 2 * DISP_BLK + N_EXP * SEG_ALIGN
FF_CHUNK_DENSE = 1408
FF_CHUNK_MOE = 512
N1 = 128

F32 = jnp.float32
BF16 = jnp.bfloat16
_HI = lax.Precision.HIGHEST


def _cparams(sem):
    return pltpu.CompilerParams(dimension_semantics=sem, vmem_limit_bytes=VMEM_LIMIT)


def _dot(a, b):
    return jnp.dot(a, b, preferred_element_type=F32)


def _dot_hi(a, b):
    return jnp.dot(a, b, precision=_HI, preferred_element_type=F32)


def _split_bf16(x):
    hi = x.astype(BF16)
    lo = (x - hi.astype(F32)).astype(BF16)
    return hi, lo


def _dot3(a_hi, a_lo, b_hi, b_lo):
    return _dot(a_hi, b_hi) + (_dot(a_hi, b_lo) + _dot(a_lo, b_hi))


class _Stream:
    def __init__(self, b, s, db, ds):
        self.b, self.s, self.db, self.ds = b, s, db, ds
        self.t0 = b * s
        self.t = b * s + db * ds
        self.nbatch = b + db

    def batch_of_block(self, i, rows):
        nb0, p0, p1 = self.t0 // rows, self.s // rows, self.ds // rows
        return jnp.where(i < nb0, i // p0, self.b + (i - nb0) // p1)

    def block_in_seq(self, i, rows):
        nb0, p0, p1 = self.t0 // rows, self.s // rows, self.ds // rows
        return jnp.where(i < nb0, i % p0, (i - nb0) % p1), jnp.where(i < nb0, p0, p1)


def _mod_kernel(c_ref, w_ref, b_ref, o_ref):
    c = c_ref[...]
    cs = c * jax.nn.sigmoid(c)
    o_ref[0] = _dot_hi(cs, w_ref[0]) + b_ref[0]


def _modulation(c_pad, w_ada, b_ada):
    depth = w_ada.shape[0]
    nb = c_pad.shape[0]
    return pl.pallas_call(
        _mod_kernel,
        grid=(depth, N_MOD),
        in_specs=[
            pl.BlockSpec((nb, D), lambda l, j: (0, 0)),
            pl.BlockSpec((1, D, D), lambda l, j: (l, 0, j)),
            pl.BlockSpec((1, 1, D), lambda l, j: (l, 0, j)),
        ],
        out_specs=pl.BlockSpec((1, nb, D), lambda l, j: (l, 0, j)),
        out_shape=jax.ShapeDtypeStruct((depth, nb, N_MOD * D), F32),
        compiler_params=_cparams(("arbitrary", "arbitrary")),
        name="adaln_mod",
    )(c_pad, w_ada, b_ada.reshape(depth, 1, N_MOD * D))


def _rms_mod(x, g, sc, sh):
    ms = jnp.mean(x * x, axis=-1, keepdims=True)
    return (x * lax.rsqrt(ms + EPS) * g) * (1.0 + sc) + sh


_C_Q, _C_K, _C_V, _C_R, _C_A, _C_F, _C_GA, _C_GB, _C_END = np.cumsum(
    [0, GLA_DK, GLA_DK, GLA_DV, GLA_DV, LANE, FNET_W, D, D]).tolist()


def _stream_specs(st, xs, rows):
    width = xs[0].shape[1]
    if len(xs) == 1:
        return [pl.BlockSpec((rows, width), lambda i, *_: (i, 0))]
    nb0 = st.t0 // rows
    return [pl.BlockSpec((rows, width), lambda i, *_: (jnp.minimum(i, nb0 - 1), 0)),
            pl.BlockSpec((rows, width), lambda i, *_: (jnp.maximum(i - nb0, 0), 0))]


def _stream_block(st, x_refs, rows):
    if len(x_refs) == 1:
        return x_refs[0][...]
    return jnp.where(pl.program_id(0) < st.t0 // rows, x_refs[0][...], x_refs[1][...])


def _inproj_kernel(*refs, st, nx):
    x_refs = refs[:nx]
    (g_ref, sc_ref, sh_ref, w_ref,
     q_ref, k_ref, v_ref, r_ref, a_ref, f_ref, ga_ref, gb_ref) = refs[nx:]
    hb = _rms_mod(_stream_block(st, x_refs, TM), g_ref[...], sc_ref[0], sh_ref[0]).astype(BF16)

    def proj(lo, hi):
        return _dot(hb, w_ref[:, lo:hi])

    q_ref[...] = (proj(_C_Q, _C_K) * (DK ** -0.5)).astype(BF16)
    k_ref[...] = proj(_C_K, _C_V).astype(BF16)
    v_ref[...] = proj(_C_V, _C_R).astype(BF16)
    zr = proj(_C_R, _C_A)
    r_ref[...] = (zr * jax.nn.sigmoid(zr)).astype(BF16)
    a_ref[...] = proj(_C_A, _C_F)
    f_ref[...] = proj(_C_F, _C_GA)
    ga_ref[...] = jax.nn.sigmoid(proj(_C_GA, _C_GB)).astype(BF16)
    gb_ref[...] = jax.nn.sigmoid(proj(_C_GB, _C_END)).astype(BF16)


def _inproj(st, xs, g, sc, sh, w):
    t = st.t
    row = lambda i: (i, 0)
    bat = lambda i: (st.batch_of_block(i, TM), 0, 0)
    widths = [(GLA_DK, BF16), (GLA_DK, BF16), (GLA_DV, BF16), (GLA_DV, BF16),
              (LANE, F32), (FNET_W, F32), (D, BF16), (D, BF16)]
    return pl.pallas_call(
        functools.partial(_inproj_kernel, st=st, nx=len(xs)),
        grid=(t // TM,),
        in_specs=_stream_specs(st, xs, TM) + [
            pl.BlockSpec((1, D), lambda i: (0, 0)),
            pl.BlockSpec((1, 1, D), bat),
            pl.BlockSpec((1, 1, D), bat),
            pl.BlockSpec((D, _C_END), lambda i: (0, 0)),
        ],
        out_specs=[pl.BlockSpec((TM, w_), row) for w_, _ in widths],
        out_shape=[jax.ShapeDtypeStruct((t, w_), dt) for w_, dt in widths],
        compiler_params=_cparams(("parallel",)),
        name="inproj",
    )(*xs, g, sc, sh, w)


def _log_sigmoid(x):
    return jnp.minimum(x, 0.0) - jnp.log1p(jnp.exp(-jnp.abs(x)))


def _gla_kernel(*refs, bwd, st):
    if bwd:
        (q_ref, k_ref, v_ref, za_ref, wah_ref, wal_ref, ba_ref, of_ref, r_ref, ng_ref,
         o_ref, s_scr, qi_scr, ki_scr, qin_scr, kout_scr, att_scr, kv_scr) = refs
    else:
        (q_ref, k_ref, v_ref, za_ref, wah_ref, wal_ref, ba_ref,
         o_ref, s_scr, qi_scr, ki_scr, qin_scr, kout_scr, att_scr, kv_scr) = refs
    i = pl.program_id(0)
    nblk = pl.num_programs(0)
    blk = (nblk - 1 - i) if bwd else i
    local, per = st.block_in_seq(blk, GLA_BLK)
    at_boundary = (local == per - 1) if bwd else (local == 0)

    @pl.when(at_boundary)
    def _():
        s_scr[...] = jnp.zeros_like(s_scr)

    nch = GLA_BLK // CHUNK
    rr = lax.broadcasted_iota(jnp.int32, (CHUNK, CHUNK), 0)
    cc = lax.broadcasted_iota(jnp.int32, (CHUNK, CHUNK), 1)
    keep = (rr <= cc) if bwd else (rr >= cc)
    tri = keep.astype(BF16)
    last = 0 if bwd else CHUNK - 1
    mid = CHUNK // 2 if bwd else CHUNK // 2 - 1

    zh, zl = _split_bf16(za_ref[...])
    g = _log_sigmoid(_dot3(zh, zl, wah_ref[...], wal_ref[...]) + ba_ref[...]) * (1.0 / GATE_NORM)
    g0 = g.astype(BF16)
    rem = g - g0.astype(F32)
    g1 = rem.astype(BF16)
    g2 = (rem - g1.astype(F32)).astype(BF16)

    def sums(m, rows=slice(None)):
        return _dot(m, g0[rows]) + (_dot(m, g1[rows]) + _dot(m, g2[rows]))

    cid = lax.broadcasted_iota(jnp.int32, (LANE, GLA_BLK), 0)
    pos = lax.broadcasted_iota(jnp.int32, (LANE, GLA_BLK), 1)
    in_chunk = (pos // CHUNK == cid).astype(BF16)
    chunk_decay = jnp.exp(sums(in_chunk).T)

    for c in range(nch):
        rows = slice(c * CHUNK, (c + 1) * CHUNK)
        b = sums(tri, rows)
        b_last = b[last:last + 1, :]
        b_ref = b[mid:mid + 1, :]
        q = q_ref[rows, :].astype(F32)
        k = k_ref[rows, :].astype(F32)
        qi_scr[rows, :] = (q * jnp.exp(b - b_ref)).astype(BF16)
        ki_scr[rows, :] = (k * jnp.exp(b_ref - b)).astype(BF16)
        qin_scr[rows, :] = (q * jnp.exp(b)).astype(BF16)
        kout_scr[rows, :] = (k * jnp.exp(b_last - b)).astype(BF16)

    for c in range(nch):
        rows = slice(c * CHUNK, (c + 1) * CHUNK)
        for h in range(HEADS):
            ks = slice(h * DK, (h + 1) * DK)
            att = lax.dot_general(qi_scr[rows, ks], ki_scr[rows, ks], (((1,), (1,)), ((), ())),
                                  preferred_element_type=F32)
            att_scr[c * HEADS + h] = jnp.where(keep, att, 0.0).astype(BF16)
            kv_scr[c * HEADS + h] = lax.dot_general(
                kout_scr[rows, ks], v_ref[rows, h * DV:(h + 1) * DV], (((0,), (0,)), ((), ())),
                preferred_element_type=F32)

    state = [s_scr[h] for h in range(HEADS)]
    for c in (reversed(range(nch)) if bwd else range(nch)):
        rows = slice(c * CHUNK, (c + 1) * CHUNK)
        for h in range(HEADS):
            ks = slice(h * DK, (h + 1) * DK)
            vs = slice(h * DV, (h + 1) * DV)
            o = _dot(jnp.concatenate([qin_scr[rows, ks], att_scr[c * HEADS + h]], axis=1),
                     jnp.concatenate([state[h].astype(BF16), v_ref[rows, vs]], axis=0))
            state[h] = state[h] * chunk_decay[ks, c:c + 1] + kv_scr[c * HEADS + h]
            if bwd:
                o = o + of_ref[rows, vs]
                o = o * lax.rsqrt(jnp.mean(o * o, axis=-1, keepdims=True) + EPS) * ng_ref[...]
                o_ref[rows, vs] = (o * r_ref[rows, vs].astype(F32)).astype(o_ref.dtype)
            else:
                o_ref[rows, vs] = o
    for h in range(HEADS):
        s_scr[h] = state[h]


def _gla_direction(st, bwd, q, k, v, za, wa, ba, extra=()):
    t = st.t
    nblk = t // GLA_BLK
    row = (lambda i: (nblk - 1 - i, 0)) if bwd else (lambda i: (i, 0))
    const = lambda i: (0, 0)
    in_specs = [
        pl.BlockSpec((GLA_BLK, GLA_DK), row),
        pl.BlockSpec((GLA_BLK, GLA_DK), row),
        pl.BlockSpec((GLA_BLK, GLA_DV), row),
        pl.BlockSpec((GLA_BLK, LANE), row),
        pl.BlockSpec((LANE, GLA_DK), const),
        pl.BlockSpec((LANE, GLA_DK), const),
        pl.BlockSpec((1, GLA_DK), const),
    ]
    if bwd:
        in_specs += [pl.BlockSpec((GLA_BLK, GLA_DV), row), pl.BlockSpec((GLA_BLK, GLA_DV), row),
                     pl.BlockSpec((1, DV), const)]
    scaled = pltpu.VMEM((GLA_BLK, GLA_DK), BF16)
    return pl.pallas_call(
        functools.partial(_gla_kernel, bwd=bwd, st=st),
        grid=(nblk,),
        in_specs=in_specs,
        out_specs=pl.BlockSpec((GLA_BLK, GLA_DV), row),
        out_shape=jax.ShapeDtypeStruct((t, GLA_DV), BF16 if bwd else F32),
        scratch_shapes=[pltpu.VMEM((HEADS, DK, DV), F32), scaled, scaled, scaled, scaled,
                        pltpu.VMEM((GLA_BLK // CHUNK * HEADS, CHUNK, CHUNK), BF16),
                        pltpu.VMEM((GLA_BLK // CHUNK * HEADS, DK, DV), F32)],
        compiler_params=_cparams(("arbitrary",)),
        name="gla_bwd" if bwd else "gla_fwd",
    )(q, k, v, za, *_split_bf16(wa), ba, *extra)


def _np_split(m):
    m = np.asarray(m, np.float32)
    hi = m.astype(BF16)
    lo = (m - hi.astype(np.float32)).astype(BF16)
    return jnp.asarray(hi), jnp.asarray(lo)


def _dft_cos_sin(n):
    idx = np.arange(n, dtype=np.float64)
    ang = 2.0 * np.pi * np.outer(idx, idx) / n
    return np.cos(ang) / np.sqrt(n), np.sin(ang) / np.sqrt(n)


def _fnet_stage1_kernel(x_ref, mh_ref, ml_ref, y_ref):
    n2 = x_ref.shape[0]
    for j in range(SUBLANE):
        xh, xl = _split_bf16(x_ref[:, j, :])
        y = _dot3(mh_ref[...], ml_ref[...], xh, xl)
        y_ref[0, :, j, :] = y[:n2]
        y_ref[1, :, j, :] = y[n2:]


def _fnet_stage1(zf3, bn, n2, seq0, mh, ml):
    const = lambda b, j: (0, 0)
    return pl.pallas_call(
        _fnet_stage1_kernel,
        grid=(bn, N1 // SUBLANE),
        in_specs=[
            pl.BlockSpec((n2, SUBLANE, FNET_W), lambda b, j: (seq0 + b, j, 0)),
            pl.BlockSpec((2 * n2, n2), const),
            pl.BlockSpec((2 * n2, n2), const),
        ],
        out_specs=pl.BlockSpec((2, n2, SUBLANE, FNET_W), lambda b, j: (0, b, j, 0)),
        out_shape=jax.ShapeDtypeStruct((2, bn * n2, N1, FNET_W), F32),
        compiler_params=_cparams(("parallel", "parallel")),
        name="fnet_stage1",
    )(zf3, mh, ml)


def _fnet_stage2_kernel(y_ref, tc_ref, ts_ref, m2h_ref, m2l_ref, m3h_ref, m3l_ref, o_ref):
    for g in range(SUBLANE):
        rows = slice(g * N1, (g + 1) * N1)
        yr = y_ref[0, rows, :]
        yi = y_ref[1, rows, :]
        tc = tc_ref[g]
        ts = ts_ref[g]
        stack = jnp.concatenate([yr * tc + yi * ts, yi * tc - yr * ts], axis=0)
        sh, sl = _split_bf16(stack)
        z = _dot3(m2h_ref[...], m2l_ref[...], sh, sl)
        zcat = jnp.concatenate([z[:N1], z[N1:]], axis=1)
        zh, zl = _split_bf16(zcat)
        o_ref[:, g, :] = _dot3(zh, zl, m3h_ref[...], m3l_ref[...])


def _fnet_stage2(y2, bn, n2, tc, ts, m2, m3):
    rows = SUBLANE * N1
    per_seq = n2 // SUBLANE
    const = lambda b, j: (0, 0)
    return pl.pallas_call(
        _fnet_stage2_kernel,
        grid=(bn, per_seq),
        in_specs=[
            pl.BlockSpec((2, rows, FNET_W), lambda b, j: (0, b * per_seq + j, 0)),
            pl.BlockSpec((SUBLANE, N1, 1), lambda b, j: (j, 0, 0)),
            pl.BlockSpec((SUBLANE, N1, 1), lambda b, j: (j, 0, 0)),
            pl.BlockSpec((2 * N1, 2 * N1), const),
            pl.BlockSpec((2 * N1, 2 * N1), const),
            pl.BlockSpec((2 * FNET_W, FNET_W), const),
            pl.BlockSpec((2 * FNET_W, FNET_W), const),
        ],
        out_specs=pl.BlockSpec((N1, SUBLANE, FNET_W), lambda b, j: (b, j, 0)),
        out_shape=jax.ShapeDtypeStruct((bn * N1, n2, FNET_W), F32),
        compiler_params=_cparams(("parallel", "parallel")),
        name="fnet_stage2",
    )(y2, tc, ts, *m2, *m3).reshape(bn * n2 * N1, FNET_W)


def _fnet(st, zf):
    c1, s1 = _dft_cos_sin(N1)
    m2 = _np_split(np.block([[c1, s1], [-s1, c1]]))
    cc, sc = _dft_cos_sin(FN_GW)
    eye = np.eye(FN_GROUPS)
    m3 = _np_split(np.concatenate([np.kron(eye, cc), np.kron(eye, sc)], axis=0))
    zf3 = zf.reshape(st.t // N1, N1, FNET_W)
    outs = []
    for bn, s, row0 in ((st.b, st.s, 0), (st.db, st.ds, st.t0)):
        n2 = s // N1
        c2, s2 = _dft_cos_sin(n2)
        m1 = _np_split(np.concatenate([c2, -s2], axis=0))
        ang = 2.0 * np.pi * np.outer(np.arange(n2), np.arange(N1)) / s
        tc = jnp.asarray(np.cos(ang)[:, :, None], F32)
        ts = jnp.asarray(np.sin(ang)[:, :, None], F32)
        y = _fnet_stage1(zf3, bn, n2, row0 // s, *m1)
        outs.append(_fnet_stage2(y.reshape(2, bn * s, FNET_W), bn, n2, tc, ts, m2, m3))
    return tuple(outs)


def _merge_kernel(*refs, st, nx, router):
    x_refs = refs[:nx]
    yf_refs = refs[nx:nx + 2]
    (og_ref, ga_ref, gb_ref, gt_ref, g2_ref, sc_ref, sh_ref,
     wog_ref, wof_ref, wout_ref) = refs[nx + 2:nx + 12]
    rest = refs[nx + 12:]
    ya = _dot(og_ref[...], wog_ref[...])
    yb = _dot(_stream_block(st, yf_refs, TM).astype(BF16), wof_ref[...])
    m = ga_ref[...].astype(F32) * ya + gb_ref[...].astype(F32) * yb
    x1 = _stream_block(st, x_refs, TM) + gt_ref[0] * _dot(m.astype(BF16), wout_ref[...])
    h2 = _rms_mod(x1, g2_ref[...], sc_ref[0], sh_ref[0])
    if router:
        wrh_ref, wrl_ref, x1_ref, h2_ref, lg_ref = rest
        lg_ref[...] = _dot3(*_split_bf16(h2), wrh_ref[...], wrl_ref[...])
    else:
        x1_ref, h2_ref = rest
    x1_ref[...] = x1
    h2_ref[...] = h2.astype(BF16)


def _merge(st, og, yfs, ga, gb, xs, gt, g2, sc, sh, wog, wof, wout, wr=None):
    t = st.t
    row = lambda i: (i, 0)
    bat = lambda i: (st.batch_of_block(i, TM), 0, 0)
    const = lambda i: (0, 0)
    in_specs = _stream_specs(st, xs, TM) + _stream_specs(st, yfs, TM) + [
        pl.BlockSpec((TM, GLA_DV), row),
        pl.BlockSpec((TM, D), row), pl.BlockSpec((TM, D), row),
        pl.BlockSpec((1, 1, D), bat), pl.BlockSpec((1, D), const),
        pl.BlockSpec((1, 1, D), bat), pl.BlockSpec((1, 1, D), bat),
        pl.BlockSpec((GLA_DV, D), const), pl.BlockSpec((FNET_W, D), const), pl.BlockSpec((D, D), const),
    ]
    out_specs = [pl.BlockSpec((TM, D), row), pl.BlockSpec((TM, D), row)]
    out_shape = [jax.ShapeDtypeStruct((t, D), F32), jax.ShapeDtypeStruct((t, D), BF16)]
    args = [*xs, *yfs, og, ga, gb, gt, g2, sc, sh, wog, wof, wout]
    if wr is not None:
        in_specs += [pl.BlockSpec((D, LANE), const), pl.BlockSpec((D, LANE), const)]
        out_specs.append(pl.BlockSpec((TM, LANE), row))
        out_shape.append(jax.ShapeDtypeStruct((t, LANE), F32))
        args += list(_split_bf16(wr))
    return pl.pallas_call(
        functools.partial(_merge_kernel, st=st, nx=len(xs), router=wr is not None),
        grid=(t // TM,),
        in_specs=in_specs, out_specs=out_specs, out_shape=out_shape,
        compiler_params=_cparams(("parallel",)),
        name="merge_router" if wr is not None else "merge",
    )(*args)


def _swiglu_into(acc_ref, h, wg_ref, wu_ref, wd_ref, width, chunk, lead=()):
    for c in range(width // chunk):
        cols = slice(c * chunk, (c + 1) * chunk)
        g = _dot(h, wg_ref[lead + (slice(None), cols)])
        u = _dot(h, wu_ref[lead + (slice(None), cols)])
        a = (g * jax.nn.sigmoid(g) * u).astype(BF16)
        part = _dot(a, wd_ref[lead + (cols, slice(None))])
        if c == 0:
            acc_ref[...] = part
        else:
            acc_ref[...] += part


def _ffn_dense_kernel(h_ref, x_ref, gt_ref, wg_ref, wu_ref, wd_ref, o_ref, acc_ref):
    _swiglu_into(acc_ref, h_ref[...], wg_ref, wu_ref, wd_ref, D_FF, FF_CHUNK_DENSE)
    o_ref[...] = x_ref[...] + gt_ref[0] * acc_ref[...]


def _ffn_dense(st, h, x, gt, wg, wu, wd):
    t = st.t
    row = lambda i: (i, 0)
    const = lambda i: (0, 0)
    once = pl.Buffered(1)
    return pl.pallas_call(
        _ffn_dense_kernel,
        grid=(t // TM,),
        in_specs=[
            pl.BlockSpec((TM, D), row), pl.BlockSpec((TM, D), row),
            pl.BlockSpec((1, 1, D), lambda i: (st.batch_of_block(i, TM), 0, 0)),
            pl.BlockSpec((D, D_FF), const, pipeline_mode=once),
            pl.BlockSpec((D, D_FF), const, pipeline_mode=once),
            pl.BlockSpec((D_FF, D), const, pipeline_mode=once),
        ],
        out_specs=pl.BlockSpec((TM, D), row),
        out_shape=jax.ShapeDtypeStruct((t, D), F32),
        scratch_shapes=[pltpu.VMEM((TM, D), F32)],
        compiler_params=_cparams(("parallel",)),
        name="ffn_dense",
    )(h, x, gt, wg, wu, wd)


def _ffn_expert_kernel(te_ref, tv_ref, x_ref, wg_ref, wu_ref, wd_ref, o_ref, acc_ref):
    k = pl.program_id(0)

    @pl.when(tv_ref[k] == 1)
    def _():
        _swiglu_into(acc_ref, x_ref[...], wg_ref, wu_ref, wd_ref, D_EXP, FF_CHUNK_MOE, lead=(0,))
        o_ref[...] = acc_ref[...].astype(o_ref.dtype)

    @pl.when(tv_ref[k] == 0)
    def _():
        o_ref[...] = jnp.zeros_like(o_ref)


def _ffn_experts(tile_e, tile_v, xp, wg, wu, wd):
    ntiles = xp.shape[0] // MOE_TILE
    once = pl.Buffered(1)
    wmap = lambda k, te, tv: (te[k], 0, 0)
    xmap = lambda k, te, tv: (k * tv[k], 0)
    grid_spec = pltpu.PrefetchScalarGridSpec(
        num_scalar_prefetch=2,
        grid=(ntiles,),
        in_specs=[
            pl.BlockSpec((MOE_TILE, D), xmap),
            pl.BlockSpec((1, D, D_EXP), wmap, pipeline_mode=once),
            pl.BlockSpec((1, D, D_EXP), wmap, pipeline_mode=once),
            pl.BlockSpec((1, D_EXP, D), wmap, pipeline_mode=once),
        ],
        out_specs=pl.BlockSpec((MOE_TILE, D), lambda k, te, tv: (k, 0)),
        scratch_shapes=[pltpu.VMEM((MOE_TILE, D), F32)],
    )
    return pl.pallas_call(
        _ffn_expert_kernel,
        grid_spec=grid_spec,
        out_shape=jax.ShapeDtypeStruct(xp.shape, BF16),
        compiler_params=_cparams(("arbitrary",)),
        name="ffn_experts",
    )(tile_e, tile_v, xp, wg, wu, wd)


def _router_kernel(lg_ref, info_ref, cnt_ref):
    lane = lax.broadcasted_iota(jnp.int32, (DISP_BLK, LANE), 1).astype(F32)
    neg = jnp.float32(-jnp.inf)
    lg = jnp.where(lane < N_EXP, lg_ref[...], neg)
    v1 = jnp.max(lg, axis=1, keepdims=True)
    i1 = jnp.min(jnp.where(lg == v1, lane, float(LANE)), axis=1, keepdims=True)
    oh1 = lane == i1
    lg2 = jnp.where(oh1, neg, lg)
    v2 = jnp.max(lg2, axis=1, keepdims=True)
    i2 = jnp.min(jnp.where(lg2 == v2, lane, float(LANE)), axis=1, keepdims=True)
    oh2 = lane == i2
    e = jnp.exp(v2 - v1)
    w1 = 1.0 / (1.0 + e)
    w2 = e / (1.0 + e)
    oh = oh1.astype(F32) + oh2.astype(F32)
    rr = lax.broadcasted_iota(jnp.int32, (DISP_BLK, DISP_BLK), 0)
    cc = lax.broadcasted_iota(jnp.int32, (DISP_BLK, DISP_BLK), 1)
    earlier = (cc < rr).astype(BF16)
    before = _dot(earlier, oh.astype(BF16))
    pos1 = jnp.sum(jnp.where(oh1, before, 0.0), axis=1, keepdims=True)
    pos2 = jnp.sum(jnp.where(oh2, before, 0.0), axis=1, keepdims=True)
    blk_cnt = jnp.sum(oh, axis=0, keepdims=True)
    cnt_ref[0] = jnp.broadcast_to(blk_cnt, (SUBLANE, LANE))
    packed = jnp.where(lane == 0, i1,
             jnp.where(lane == 1, i2,
             jnp.where(lane == 2, w1,
             jnp.where(lane == 3, w2,
             jnp.where(lane == 4, pos1,
             jnp.where(lane == 5, pos2, 0.0))))))
    info_ref[0] = packed.T[0:SUBLANE, :]


def _router(logits):
    t = logits.shape[0]
    nblk = t // DISP_BLK
    return pl.pallas_call(
        _router_kernel,
        grid=(nblk,),
        in_specs=[pl.BlockSpec((DISP_BLK, LANE), lambda i: (i, 0))],
        out_specs=[pl.BlockSpec((1, SUBLANE, DISP_BLK), lambda i: (i, 0, 0)),
                   pl.BlockSpec((1, SUBLANE, LANE), lambda i: (i, 0, 0))],
        out_shape=[jax.ShapeDtypeStruct((nblk, SUBLANE, DISP_BLK), F32),
                   jax.ShapeDtypeStruct((nblk, SUBLANE, LANE), F32)],
        compiler_params=_cparams(("parallel",)),
        name="router",
    )(logits)


def _dispatch_plan(info, cnt):
    nblk = info.shape[0]
    t = nblk * DISP_BLK
    max_rows = 2 * t + nblk * N_EXP * (SEG_ALIGN - 1) + N_EXP * (MOE_TILE - 1)
    ntiles = -(-max_rows // MOE_TILE)
    e12 = info[:, 0:2, :].astype(jnp.int32)
    w12 = info[:, 2:4, :]
    rank12 = info[:, 4:6, :].astype(jnp.int32)
    bc = cnt[:, 0, :N_EXP].astype(jnp.int32)
    pc = (bc + SEG_ALIGN - 1) // SEG_ALIGN * SEG_ALIGN
    loff = jnp.cumsum(pc, axis=1) - pc
    etot = jnp.sum(pc, axis=0)
    epad = (etot + MOE_TILE - 1) // MOE_TILE * MOE_TILE
    eend = jnp.cumsum(epad)
    goff = (eend - epad)[None, :] + jnp.cumsum(pc, axis=0) - pc
    experts = jnp.arange(N_EXP, dtype=jnp.int32)
    lpos = jnp.sum(jnp.where(e12[..., None] == experts, loff[:, None, None, :], 0), axis=-1) + rank12
    tile_row = jnp.arange(ntiles, dtype=jnp.int32) * MOE_TILE
    tile_e = jnp.minimum(jnp.sum((eend[None, :] <= tile_row[:, None]).astype(jnp.int32), axis=1),
                         N_EXP - 1)
    tile_v = (tile_row < eend[-1]).astype(jnp.int32)
    return dict(lpos=lpos, w12=w12, tile_e=tile_e, tile_v=tile_v, ntiles=ntiles,
                segs=(goff.reshape(-1), loff.reshape(-1), (pc // SEG_ALIGN).reshape(-1)))


def _for_each_segment_group(goff_ref, loff_ref, ngrp_ref, blk, fn):
    for e in range(N_EXP):
        idx = blk * N_EXP + e
        lo, go = loff_ref[idx], goff_ref[idx]

        def body(i, carry):
            fn(pl.multiple_of(lo + i * SEG_ALIGN, SEG_ALIGN), pl.multiple_of(go + i * SEG_ALIGN, SEG_ALIGN))
            return carry

        lax.fori_loop(0, ngrp_ref[idx], body, 0)


def _local_rows(lpos_ref):
    rows = lax.broadcasted_iota(jnp.int32, (LROWS, DISP_BLK), 0)
    return rows == lpos_ref[0, 0:1, :], rows == lpos_ref[0, 1:2, :]


def _gather_kernel(goff_ref, loff_ref, ngrp_ref, h_ref, lpos_ref, xp_init_ref, xp_ref, xl_ref, sem):
    del xp_init_ref
    b = pl.program_id(0)
    nblk = pl.num_programs(0)
    slot = b % 2

    def copy(slot_, lrow, srow):
        return pltpu.make_async_copy(xl_ref.at[slot_, pl.ds(lrow, SEG_ALIGN), :],
                                     xp_ref.at[pl.ds(srow, SEG_ALIGN), :], sem.at[slot_])

    def each(blk, fn):
        _for_each_segment_group(goff_ref, loff_ref, ngrp_ref, blk, fn)

    @pl.when(b >= 2)
    def _():
        each(b - 2, lambda lrow, srow: copy(slot, lrow, srow).wait())

    hit1, hit2 = _local_rows(lpos_ref)
    sel = jnp.logical_or(hit1, hit2).astype(BF16)
    xl_ref[slot] = _dot(sel, h_ref[...]).astype(BF16)
    each(b, lambda lrow, srow: copy(slot, lrow, srow).start())

    @pl.when(b == nblk - 1)
    def _():
        each(b, lambda lrow, srow: copy(slot, lrow, srow).wait())

        @pl.when(b >= 1)
        def _():
            each(b - 1, lambda lrow, srow: copy(1 - slot, lrow, srow).wait())


def _gather_rows(plan, h):
    nrows = plan["ntiles"] * MOE_TILE
    nblk = h.shape[0] // DISP_BLK
    grid_spec = pltpu.PrefetchScalarGridSpec(
        num_scalar_prefetch=3,
        grid=(nblk,),
        in_specs=[
            pl.BlockSpec((DISP_BLK, D), lambda b, *_: (b, 0)),
            pl.BlockSpec((1, 2, DISP_BLK), lambda b, *_: (b, 0, 0)),
            pl.BlockSpec(memory_space=pl.ANY),
        ],
        out_specs=pl.BlockSpec(memory_space=pl.ANY),
        scratch_shapes=[pltpu.VMEM((2, LROWS, D), BF16), pltpu.SemaphoreType.DMA((2,))],
    )
    return pl.pallas_call(
        _gather_kernel,
        grid_spec=grid_spec,
        out_shape=jax.ShapeDtypeStruct((nrows, D), BF16),
        input_output_aliases={5: 0},
        compiler_params=_cparams(("arbitrary",)),
        name="moe_gather",
    )(*plan["segs"], h, plan["lpos"], jnp.zeros((nrows, D), BF16))


def _combine_kernel(goff_ref, loff_ref, ngrp_ref, yp_ref, lpos_ref, w_ref, x_ref, gt_ref, gf_ref,
                    op_ref, os_ref, yl_ref, sem, *, nb0):
    b = pl.program_id(0)
    nblk = pl.num_programs(0)
    slot = b % 2

    def copy(slot_, lrow, srow):
        return pltpu.make_async_copy(yp_ref.at[pl.ds(srow, SEG_ALIGN), :],
                                     yl_ref.at[slot_, pl.ds(lrow, SEG_ALIGN), :], sem.at[slot_])

    def each(blk, fn):
        _for_each_segment_group(goff_ref, loff_ref, ngrp_ref, blk, fn)

    @pl.when(b == 0)
    def _():
        yl_ref[...] = jnp.zeros_like(yl_ref)
        each(0, lambda lrow, srow: copy(0, lrow, srow).start())

    @pl.when(b + 1 < nblk)
    def _():
        each(b + 1, lambda lrow, srow: copy(1 - slot, lrow, srow).start())

    each(b, lambda lrow, srow: copy(slot, lrow, srow).wait())

    hit1, hit2 = _local_rows(lpos_ref)
    wsel = jnp.where(hit1, w_ref[0, 0:1, :], 0.0) + jnp.where(hit2, w_ref[0, 1:2, :], 0.0)
    wh, wl = _split_bf16(wsel)
    tn = (((0,), (0,)), ((), ()))
    yl = yl_ref[slot]
    y = (lax.dot_general(wh, yl, tn, preferred_element_type=F32)
         + lax.dot_general(wl, yl, tn, preferred_element_type=F32))
    x2 = x_ref[...] + gt_ref[0] * y
    ms = jnp.mean(x2 * x2, axis=-1, keepdims=True)
    out = x2 * lax.rsqrt(ms + EPS) * gf_ref[...]

    @pl.when(b < nb0)
    def _():
        op_ref[...] = out

    @pl.when(b >= nb0)
    def _():
        os_ref[...] = out


def _combine(st, plan, yp, x, gt, gf):
    nblk = st.t // DISP_BLK
    nb0 = st.t0 // DISP_BLK
    blk3 = lambda b, *_: (b, 0, 0)
    grid_spec = pltpu.PrefetchScalarGridSpec(
        num_scalar_prefetch=3,
        grid=(nblk,),
        in_specs=[
            pl.BlockSpec(memory_space=pl.ANY),
            pl.BlockSpec((1, 2, DISP_BLK), blk3),
            pl.BlockSpec((1, 2, DISP_BLK), blk3),
            pl.BlockSpec((DISP_BLK, D), lambda b, *_: (b, 0)),
            pl.BlockSpec((1, 1, D), lambda b, *_: (st.batch_of_block(b, DISP_BLK), 0, 0)),
            pl.BlockSpec((1, D), lambda b, *_: (0, 0)),
        ],
        out_specs=[pl.BlockSpec((DISP_BLK, D), lambda b, *_: (jnp.minimum(b, nb0 - 1), 0)),
                   pl.BlockSpec((DISP_BLK, D), lambda b, *_: (jnp.maximum(b - nb0, 0), 0))],
        scratch_shapes=[pltpu.VMEM((2, LROWS, D), BF16), pltpu.SemaphoreType.DMA((2,))],
    )
    return pl.pallas_call(
        functools.partial(_combine_kernel, nb0=nb0),
        grid_spec=grid_spec,
        out_shape=[jax.ShapeDtypeStruct((st.t0, D), F32), jax.ShapeDtypeStruct((st.t - st.t0, D), F32)],
        compiler_params=_cparams(("arbitrary",)),
        name="moe_combine_norm",
    )(*plan["segs"], yp, plan["lpos"], plan["w12"], x, gt, gf)


def _pack_w_in(w):
    o = np.cumsum([0, GLA_DK, GLA_DK, GLA_DV, GLA_DV, RANK, RANK, FNET_W, D, D]).tolist()
    gates = jnp.pad(w[:, o[4]:o[6]], ((0, 0), (0, LANE - 2 * RANK)))
    return jnp.concatenate([w[:, o[0]:o[4]], gates, w[:, o[6]:o[9]]], axis=1).astype(BF16)


def _gate_weights(w_a2, offset):
    return jnp.pad(w_a2, ((offset, LANE - RANK - offset), (0, 0)))


def kernel(x_prompt, x_sample, c_prompt, c_sample, norm1_g, norm2_g, w_ada, b_ada, w_in, w_af2, b_af, w_ab2, b_ab, gla_norm_g, w_o_gla, w_o_fnet, w_out, w_ff_gate, w_ff_up, w_ff_down, w_router, w_e_gate, w_e_up, w_e_down, final_norm_g):
    b, s, _ = x_prompt.shape
    db, ds, _ = x_sample.shape
    depth = w_in.shape[0]
    assert depth == 2 and s % max(GLA_BLK, TM, DISP_BLK, SUBLANE * N1) == 0
    assert ds % max(GLA_BLK, TM, DISP_BLK, SUBLANE * N1) == 0 and (b * s) % ds == 0
    st = _Stream(b, s, db, ds)

    xs = (x_prompt.reshape(b * s, D), x_sample.reshape(db * ds, D))
    c = jnp.concatenate([c_prompt, c_sample], axis=0)
    nb_pad = -(-st.nbatch // SUBLANE) * SUBLANE
    c = jnp.pad(c, ((0, nb_pad - st.nbatch), (0, 0)))
    mod = _modulation(c, w_ada, b_ada)

    def mod_piece(l, j):
        return mod[l, :, j * D:(j + 1) * D].reshape(nb_pad, 1, D)

    for l in range(depth):
        sh1, sc1, gt1, sh2, sc2, gt2 = [mod_piece(l, j) for j in range(N_MOD)]
        q, k, v, r, za, zf, ga, gb = _inproj(st, xs, norm1_g[l].reshape(1, D), sc1, sh1, _pack_w_in(w_in[l]))

        wa_f = _gate_weights(w_af2[l], 0)
        wa_b = _gate_weights(w_ab2[l], RANK)
        o_f = _gla_direction(st, False, q, k, v, za, wa_f, b_af[l].reshape(1, GLA_DK))
        o_gla = _gla_direction(st, True, q, k, v, za, wa_b, b_ab[l].reshape(1, GLA_DK),
                               extra=(o_f, r, gla_norm_g[l].reshape(1, DV)))

        yf = _fnet(st, zf)

        wog, wof, wout = w_o_gla[l].astype(BF16), w_o_fnet[l].astype(BF16), w_out[l].astype(BF16)
        g2 = norm2_g[l].reshape(1, D)
        if l % 2 == 0:
            x1, h2 = _merge(st, o_gla, yf, ga, gb, xs, gt1, g2, sc2, sh2, wog, wof, wout)
            xs = (_ffn_dense(st, h2, x1, gt2, w_ff_gate[l // 2].astype(BF16),
                             w_ff_up[l // 2].astype(BF16), w_ff_down[l // 2].astype(BF16)),)
        else:
            wr = jnp.pad(w_router[l // 2], ((0, 0), (0, LANE - N_EXP)))
            x1, h2, logits = _merge(st, o_gla, yf, ga, gb, xs, gt1, g2, sc2, sh2, wog, wof, wout, wr)
            info, cnt = _router(logits)
            plan = _dispatch_plan(info, cnt)
            xp = _gather_rows(plan, h2)
            yp = _ffn_experts(plan["tile_e"], plan["tile_v"], xp, w_e_gate[l // 2].astype(BF16),
                              w_e_up[l // 2].astype(BF16), w_e_down[l // 2].astype(BF16))
            y_prompt, y_sample = _combine(st, plan, yp, x1, gt2, final_norm_g.reshape(1, D))

    return y_prompt.reshape(b, s, D), y_sample.reshape(db, ds, D)
```

```python
import functools

import numpy as np
import jax
import jax.numpy as jnp
from jax import lax
from jax.experimental import pallas as pl
from jax.experimental.pallas import tpu as pltpu

D = 1024
HEADS = 4
DK = 128
DV = 256
GLA_DK = HEADS * DK
GLA_DV = HEADS * DV
RANK = 16
GATE_NORM = 16.0
CHUNK = 64
FN_GROUPS = 4
FN_GW = 128
FNET_W = FN_GROUPS * FN_GW
D_FF = 2816
N_EXP = 8
D_EXP = 3584
EPS = 1e-6
N_MOD = 6

LANE = 128
SUBLANE = 8
VMEM_LIMIT = 56 * 1024 * 1024
TM = 512
GLA_BLK = 512
MOE_TILE = 256
DISP_BLK = 512
SEG_ALIGN = 16
LROWS = 2 * DISP_BLK + N_EXP * SEG_ALIGN
FF_CHUNK_DENSE = 1408
FF_CHUNK_MOE = 512
N1 = 128

F32 = jnp.float32
BF16 = jnp.bfloat16
_HI = lax.Precision.HIGHEST


def _cparams(sem):
    return pltpu.CompilerParams(dimension_semantics=sem, vmem_limit_bytes=VMEM_LIMIT)


def _dot(a, b):
    return jnp.dot(a, b, preferred_element_type=F32)


def _dot_hi(a, b):
    return jnp.dot(a, b, precision=_HI, preferred_element_type=F32)


def _split_bf16(x):
    hi = x.astype(BF16)
    lo = (x - hi.astype(F32)).astype(BF16)
    return hi, lo


def _dot3(a_hi, a_lo, b_hi, b_lo):
    return _dot(a_hi, b_hi) + (_dot(a_hi, b_lo) + _dot(a_lo, b_hi))


class _Stream:
    def __init__(self, b, s, db, ds):
        self.b, self.s, self.db, self.ds = b, s, db, ds
        self.t0 = b * s
        self.t = b * s + db * ds
        self.nbatch = b + db

    def batch_of_block(self, i, rows):
        nb0, p0, p1 = self.t0 // rows, self.s // rows, self.ds // rows
        return jnp.where(i < nb0, i // p0, self.b + (i - nb0) // p1)

    def block_in_seq(self, i, rows):
        nb0, p0, p1 = self.t0 // rows, self.s // rows, self.ds // rows
        return jnp.where(i < nb0, i % p0, (i - nb0) % p1), jnp.where(i < nb0, p0, p1)


def _mod_kernel(c_ref, w_ref, b_ref, o_ref):
    c = c_ref[...]
    cs = c * jax.nn.sigmoid(c)
    o_ref[0] = _dot_hi(cs, w_ref[0]) + b_ref[0]


def _modulation(c_pad, w_ada, b_ada):
    depth = w_ada.shape[0]
    nb = c_pad.shape[0]
    return pl.pallas_call(
        _mod_kernel,
        grid=(depth, N_MOD),
        in_specs=[
            pl.BlockSpec((nb, D), lambda l, j: (0, 0)),
            pl.BlockSpec((1, D, D), lambda l, j: (l, 0, j)),
            pl.BlockSpec((1, 1, D), lambda l, j: (l, 0, j)),
        ],
        out_specs=pl.BlockSpec((1, nb, D), lambda l, j: (l, 0, j)),
        out_shape=jax.ShapeDtypeStruct((depth, nb, N_MOD * D), F32),
        compiler_params=_cparams(("arbitrary", "arbitrary")),
        name="adaln_mod",
    )(c_pad, w_ada, b_ada.reshape(depth, 1, N_MOD * D))


def _rms_mod(x, g, sc, sh):
    ms = jnp.mean(x * x, axis=-1, keepdims=True)
    return (x * lax.rsqrt(ms + EPS) * g) * (1.0 + sc) + sh


_C_Q, _C_K, _C_V, _C_R, _C_A, _C_F, _C_GA, _C_GB, _C_END = np.cumsum(
    [0, GLA_DK, GLA_DK, GLA_DV, GLA_DV, LANE, FNET_W, D, D]).tolist()


def _stream_specs(st, xs, rows):
    width = xs[0].shape[1]
    if len(xs) == 1:
        return [pl.BlockSpec((rows, width), lambda i, *_: (i, 0))]
    nb0 = st.t0 // rows
    return [pl.BlockSpec((rows, width), lambda i, *_: (jnp.minimum(i, nb0 - 1), 0)),
            pl.BlockSpec((rows, width), lambda i, *_: (jnp.maximum(i - nb0, 0), 0))]


def _stream_block(st, x_refs, rows):
    if len(x_refs) == 1:
        return x_refs[0][...]
    return jnp.where(pl.program_id(0) < st.t0 // rows, x_refs[0][...], x_refs[1][...])


def _inproj_kernel(*refs, st, nx):
    x_refs = refs[:nx]
    (g_ref, sc_ref, sh_ref, w_ref,
     q_ref, k_ref, v_ref, r_ref, a_ref, f_ref, ga_ref, gb_ref) = refs[nx:]
    hb = _rms_mod(_stream_block(st, x_refs, TM), g_ref[...], sc_ref[0], sh_ref[0]).astype(BF16)

    def proj(lo, hi):
        return _dot(hb, w_ref[:, lo:hi])

    q_ref[...] = (proj(_C_Q, _C_K) * (DK ** -0.5)).astype(BF16)
    k_ref[...] = proj(_C_K, _C_V).astype(BF16)
    v_ref[...] = proj(_C_V, _C_R).astype(BF16)
    zr = proj(_C_R, _C_A)
    r_ref[...] = (zr * jax.nn.sigmoid(zr)).astype(BF16)
    a_ref[...] = proj(_C_A, _C_F)
    f_ref[...] = proj(_C_F, _C_GA)
    ga_ref[...] = jax.nn.sigmoid(proj(_C_GA, _C_GB)).astype(BF16)
    gb_ref[...] = jax.nn.sigmoid(proj(_C_GB, _C_END)).astype(BF16)


def _inproj(st, xs, g, sc, sh, w):
    t = st.t
    row = lambda i: (i, 0)
    bat = lambda i: (st.batch_of_block(i, TM), 0, 0)
    widths = [(GLA_DK, BF16), (GLA_DK, BF16), (GLA_DV, BF16), (GLA_DV, BF16),
              (LANE, F32), (FNET_W, F32), (D, BF16), (D, BF16)]
    return pl.pallas_call(
        functools.partial(_inproj_kernel, st=st, nx=len(xs)),
        grid=(t // TM,),
        in_specs=_stream_specs(st, xs, TM) + [
            pl.BlockSpec((1, D), lambda i: (0, 0)),
            pl.BlockSpec((1, 1, D), bat),
            pl.BlockSpec((1, 1, D), bat),
            pl.BlockSpec((D, _C_END), lambda i: (0, 0)),
        ],
        out_specs=[pl.BlockSpec((TM, w_), row) for w_, _ in widths],
        out_shape=[jax.ShapeDtypeStruct((t, w_), dt) for w_, dt in widths],
        compiler_params=_cparams(("parallel",)),
        name="inproj",
    )(*xs, g, sc, sh, w)


def _log_sigmoid(x):
    return jnp.minimum(x, 0.0) - jnp.log(1.0 + jnp.exp(-jnp.abs(x)))


def _gla_kernel(*refs, bwd, st):
    if bwd:
        (q_ref, k_ref, v_ref, za_ref, wah_ref, wal_ref, ba_ref, of_ref, r_ref, ng_ref,
         o_ref, s_scr, qi_scr, ki_scr, qin_scr, kout_scr, att_scr, kv_scr) = refs
    else:
        (q_ref, k_ref, v_ref, za_ref, wah_ref, wal_ref, ba_ref,
         o_ref, s_scr, qi_scr, ki_scr, qin_scr, kout_scr, att_scr, kv_scr) = refs
    i = pl.program_id(0)
    nblk = pl.num_programs(0)
    blk = (nblk - 1 - i) if bwd else i
    local, per = st.block_in_seq(blk, GLA_BLK)
    at_boundary = (local == per - 1) if bwd else (local == 0)

    @pl.when(at_boundary)
    def _():
        s_scr[...] = jnp.zeros_like(s_scr)

    nch = GLA_BLK // CHUNK
    rr = lax.broadcasted_iota(jnp.int32, (CHUNK, CHUNK), 0)
    cc = lax.broadcasted_iota(jnp.int32, (CHUNK, CHUNK), 1)
    keep = (rr <= cc) if bwd else (rr >= cc)
    tri = keep.astype(BF16)
    last = 0 if bwd else CHUNK - 1
    mid = CHUNK // 2 if bwd else CHUNK // 2 - 1

    zh, zl = _split_bf16(za_ref[...])
    g = _log_sigmoid(_dot3(zh, zl, wah_ref[...], wal_ref[...]) + ba_ref[...]) * (1.0 / GATE_NORM)
    g0, g1 = _split_bf16(g)

    def sums(m, rows=slice(None)):
        return _dot(m, g0[rows]) + _dot(m, g1[rows])

    cid = lax.broadcasted_iota(jnp.int32, (LANE, GLA_BLK), 0)
    pos = lax.broadcasted_iota(jnp.int32, (LANE, GLA_BLK), 1)
    in_chunk = (pos // CHUNK == cid).astype(BF16)
    chunk_decay = jnp.exp(sums(in_chunk).T)

    for c in range(nch):
        rows = slice(c * CHUNK, (c + 1) * CHUNK)
        b = sums(tri, rows)
        b_last = b[last:last + 1, :]
        b_ref = b[mid:mid + 1, :]
        q = q_ref[rows, :].astype(F32)
        k = k_ref[rows, :].astype(F32)
        qi_scr[rows, :] = (q * jnp.exp(b - b_ref)).astype(BF16)
        ki_scr[rows, :] = (k * jnp.exp(b_ref - b)).astype(BF16)
        qin_scr[rows, :] = (q * jnp.exp(b)).astype(BF16)
        kout_scr[rows, :] = (k * jnp.exp(b_last - b)).astype(BF16)

    for c in range(nch):
        rows = slice(c * CHUNK, (c + 1) * CHUNK)
        for h in range(HEADS):
            ks = slice(h * DK, (h + 1) * DK)
            att = lax.dot_general(qi_scr[rows, ks], ki_scr[rows, ks], (((1,), (1,)), ((), ())),
                                  preferred_element_type=F32)
            att_scr[c * HEADS + h] = jnp.where(keep, att, 0.0).astype(BF16)
            kv_scr[c * HEADS + h] = lax.dot_general(
                kout_scr[rows, ks], v_ref[rows, h * DV:(h + 1) * DV], (((0,), (0,)), ((), ())),
                preferred_element_type=F32)

    state = [s_scr[h] for h in range(HEADS)]
    for c in (reversed(range(nch)) if bwd else range(nch)):
        rows = slice(c * CHUNK, (c + 1) * CHUNK)
        for h in range(HEADS):
            ks = slice(h * DK, (h + 1) * DK)
            vs = slice(h * DV, (h + 1) * DV)
            o = _dot(jnp.concatenate([qin_scr[rows, ks], att_scr[c * HEADS + h]], axis=1),
                     jnp.concatenate([state[h].astype(BF16), v_ref[rows, vs]], axis=0))
            state[h] = state[h] * chunk_decay[ks, c:c + 1] + kv_scr[c * HEADS + h]
            if bwd:
                o = o + of_ref[rows, vs]
                o = o * lax.rsqrt(jnp.mean(o * o, axis=-1, keepdims=True) + EPS) * ng_ref[...]
                o_ref[rows, vs] = (o * r_ref[rows, vs].astype(F32)).astype(o_ref.dtype)
            else:
                o_ref[rows, vs] = o
    for h in range(HEADS):
        s_scr[h] = state[h]


def _gla_direction(st, bwd, q, k, v, za, wa, ba, extra=()):
    t = st.t
    nblk = t // GLA_BLK
    row = (lambda i: (nblk - 1 - i, 0)) if bwd else (lambda i: (i, 0))
    const = lambda i: (0, 0)
    in_specs = [
        pl.BlockSpec((GLA_BLK, GLA_DK), row),
        pl.BlockSpec((GLA_BLK, GLA_DK), row),
        pl.BlockSpec((GLA_BLK, GLA_DV), row),
        pl.BlockSpec((GLA_BLK, LANE), row),
        pl.BlockSpec((LANE, GLA_DK), const),
        pl.BlockSpec((LANE, GLA_DK), const),
        pl.BlockSpec((1, GLA_DK), const),
    ]
    if bwd:
        in_specs += [pl.BlockSpec((GLA_BLK, GLA_DV), row), pl.BlockSpec((GLA_BLK, GLA_DV), row),
                     pl.BlockSpec((1, DV), const)]
    scaled = pltpu.VMEM((GLA_BLK, GLA_DK), BF16)
    return pl.pallas_call(
        functools.partial(_gla_kernel, bwd=bwd, st=st),
        grid=(nblk,),
        in_specs=in_specs,
        out_specs=pl.BlockSpec((GLA_BLK, GLA_DV), row),
        out_shape=jax.ShapeDtypeStruct((t, GLA_DV), BF16 if bwd else F32),
        scratch_shapes=[pltpu.VMEM((HEADS, DK, DV), F32), scaled, scaled, scaled, scaled,
                        pltpu.VMEM((GLA_BLK // CHUNK * HEADS, CHUNK, CHUNK), BF16),
                        pltpu.VMEM((GLA_BLK // CHUNK * HEADS, DK, DV), F32)],
        compiler_params=_cparams(("arbitrary",)),
        name="gla_bwd" if bwd else "gla_fwd",
    )(q, k, v, za, *_split_bf16(wa), ba, *extra)


def _dft_cos_sin(n):
    idx = np.arange(n, dtype=np.float64)
    ang = 2.0 * np.pi * np.outer(idx, idx) / n
    return np.cos(ang) / np.sqrt(n), np.sin(ang) / np.sqrt(n)


def _fnet_stage1_kernel(x_ref, m_ref, y_ref):
    n2 = x_ref.shape[0]
    x = x_ref[...].reshape(n2 * SUBLANE, FNET_W).astype(BF16)
    y_ref[...] = _dot(m_ref[...], x).reshape(2, n2, SUBLANE, FNET_W)


def _fnet_stage1(zf3, bn, n2, seq0, m1):
    return pl.pallas_call(
        _fnet_stage1_kernel,
        grid=(bn, N1 // SUBLANE),
        in_specs=[
            pl.BlockSpec((n2, SUBLANE, FNET_W), lambda b, j: (seq0 + b, j, 0)),
            pl.BlockSpec(m1.shape, lambda b, j: (0, 0)),
        ],
        out_specs=pl.BlockSpec((2, n2, SUBLANE, FNET_W), lambda b, j: (0, b, j, 0)),
        out_shape=jax.ShapeDtypeStruct((2, bn * n2, N1, FNET_W), F32),
        compiler_params=_cparams(("parallel", "parallel")),
        name="fnet_stage1",
    )(zf3, m1)


def _fnet_stage2_kernel(y_ref, tc_ref, ts_ref, m2_ref, m3_ref, o_ref, z_scr):
    for g in range(SUBLANE):
        rows = slice(g * N1, (g + 1) * N1)
        yr = y_ref[0, rows, :]
        yi = y_ref[1, rows, :]
        tc = tc_ref[g]
        ts = ts_ref[g]
        stack = jnp.concatenate([yr * tc + yi * ts, yi * tc - yr * ts], axis=0)
        z = _dot(m2_ref[...], stack.astype(BF16))
        z_scr[rows, 0:FNET_W] = z[:N1].astype(BF16)
        z_scr[rows, FNET_W:2 * FNET_W] = z[N1:].astype(BF16)
    out = _dot(z_scr[...], m3_ref[...])
    for g in range(SUBLANE):
        o_ref[:, g, :] = out[g * N1:(g + 1) * N1]


def _fnet_stage2(y2, bn, n2, tc, ts, m2, m3):
    rows = SUBLANE * N1
    per_seq = n2 // SUBLANE
    const = lambda b, j: (0, 0)
    return pl.pallas_call(
        _fnet_stage2_kernel,
        grid=(bn, per_seq),
        in_specs=[
            pl.BlockSpec((2, rows, FNET_W), lambda b, j: (0, b * per_seq + j, 0)),
            pl.BlockSpec((SUBLANE, N1, 1), lambda b, j: (j, 0, 0)),
            pl.BlockSpec((SUBLANE, N1, 1), lambda b, j: (j, 0, 0)),
            pl.BlockSpec((2 * N1, 2 * N1), const),
            pl.BlockSpec((2 * FNET_W, FNET_W), const),
        ],
        out_specs=pl.BlockSpec((N1, SUBLANE, FNET_W), lambda b, j: (b, j, 0)),
        out_shape=jax.ShapeDtypeStruct((bn * N1, n2, FNET_W), F32),
        scratch_shapes=[pltpu.VMEM((rows, 2 * FNET_W), BF16)],
        compiler_params=_cparams(("parallel", "parallel")),
        name="fnet_stage2",
    )(y2, tc, ts, m2, m3).reshape(bn * n2 * N1, FNET_W)


def _fnet(st, zf):
    as_bf16 = lambda m: jnp.asarray(np.asarray(m, np.float32).astype(BF16))
    c1, s1 = _dft_cos_sin(N1)
    m2 = as_bf16(np.block([[c1, s1], [-s1, c1]]))
    cc, sc = _dft_cos_sin(FN_GW)
    eye = np.eye(FN_GROUPS)
    m3 = as_bf16(np.concatenate([np.kron(eye, cc), np.kron(eye, sc)], axis=0))
    zf3 = zf.reshape(st.t // N1, N1, FNET_W)
    outs = []
    for bn, s, row0 in ((st.b, st.s, 0), (st.db, st.ds, st.t0)):
        n2 = s // N1
        c2, s2 = _dft_cos_sin(n2)
        m1 = as_bf16(np.kron(np.concatenate([c2, -s2], axis=0), np.eye(SUBLANE)))
        ang = 2.0 * np.pi * np.outer(np.arange(n2), np.arange(N1)) / s
        tc = jnp.asarray(np.cos(ang)[:, :, None], F32)
        ts = jnp.asarray(np.sin(ang)[:, :, None], F32)
        y = _fnet_stage1(zf3, bn, n2, row0 // s, m1)
        outs.append(_fnet_stage2(y.reshape(2, bn * s, FNET_W), bn, n2, tc, ts, m2, m3))
    return tuple(outs)


def _merge_kernel(*refs, st, nx, router):
    x_refs = refs[:nx]
    yf_refs = refs[nx:nx + 2]
    (og_ref, ga_ref, gb_ref, gt_ref, g2_ref, sc_ref, sh_ref,
     wog_ref, wof_ref, wout_ref) = refs[nx + 2:nx + 12]
    rest = refs[nx + 12:]
    ya = _dot(og_ref[...], wog_ref[...])
    yb = _dot(_stream_block(st, yf_refs, TM).astype(BF16), wof_ref[...])
    m = ga_ref[...].astype(F32) * ya + gb_ref[...].astype(F32) * yb
    x1 = _stream_block(st, x_refs, TM) + gt_ref[0] * _dot(m.astype(BF16), wout_ref[...])
    h2 = _rms_mod(x1, g2_ref[...], sc_ref[0], sh_ref[0])
    if router:
        wrh_ref, wrl_ref, x1_ref, h2_ref, lg_ref = rest
        lg_ref[...] = _dot3(*_split_bf16(h2), wrh_ref[...], wrl_ref[...])
    else:
        x1_ref, h2_ref = rest
    x1_ref[...] = x1
    h2_ref[...] = h2.astype(BF16)


def _merge(st, og, yfs, ga, gb, xs, gt, g2, sc, sh, wog, wof, wout, wr=None):
    t = st.t
    row = lambda i: (i, 0)
    bat = lambda i: (st.batch_of_block(i, TM), 0, 0)
    const = lambda i: (0, 0)
    in_specs = _stream_specs(st, xs, TM) + _stream_specs(st, yfs, TM) + [
        pl.BlockSpec((TM, GLA_DV), row),
        pl.BlockSpec((TM, D), row), pl.BlockSpec((TM, D), row),
        pl.BlockSpec((1, 1, D), bat), pl.BlockSpec((1, D), const),
        pl.BlockSpec((1, 1, D), bat), pl.BlockSpec((1, 1, D), bat),
        pl.BlockSpec((GLA_DV, D), const), pl.BlockSpec((FNET_W, D), const), pl.BlockSpec((D, D), const),
    ]
    out_specs = [pl.BlockSpec((TM, D), row), pl.BlockSpec((TM, D), row)]
    out_shape = [jax.ShapeDtypeStruct((t, D), F32), jax.ShapeDtypeStruct((t, D), BF16)]
    args = [*xs, *yfs, og, ga, gb, gt, g2, sc, sh, wog, wof, wout]
    if wr is not None:
        in_specs += [pl.BlockSpec((D, LANE), const), pl.BlockSpec((D, LANE), const)]
        out_specs.append(pl.BlockSpec((TM, LANE), row))
        out_shape.append(jax.ShapeDtypeStruct((t, LANE), F32))
        args += list(_split_bf16(wr))
    return pl.pallas_call(
        functools.partial(_merge_kernel, st=st, nx=len(xs), router=wr is not None),
        grid=(t // TM,),
        in_specs=in_specs, out_specs=out_specs, out_shape=out_shape,
        compiler_params=_cparams(("parallel",)),
        name="merge_router" if wr is not None else "merge",
    )(*args)


def _swiglu_into(acc_ref, h, wg_ref, wu_ref, wd_ref, width, chunk, lead=()):
    for c in range(width // chunk):
        cols = slice(c * chunk, (c + 1) * chunk)
        g = _dot(h, wg_ref[lead + (slice(None), cols)])
        u = _dot(h, wu_ref[lead + (slice(None), cols)])
        a = (g * jax.nn.sigmoid(g) * u).astype(BF16)
        part = _dot(a, wd_ref[lead + (cols, slice(None))])
        if c == 0:
            acc_ref[...] = part
        else:
            acc_ref[...] += part


def _ffn_dense_kernel(h_ref, x_ref, gt_ref, wg_ref, wu_ref, wd_ref, o_ref, acc_ref):
    _swiglu_into(acc_ref, h_ref[...], wg_ref, wu_ref, wd_ref, D_FF, FF_CHUNK_DENSE)
    o_ref[...] = x_ref[...] + gt_ref[0] * acc_ref[...]


def _ffn_dense(st, h, x, gt, wg, wu, wd):
    t = st.t
    row = lambda i: (i, 0)
    const = lambda i: (0, 0)
    once = pl.Buffered(1)
    return pl.pallas_call(
        _ffn_dense_kernel,
        grid=(t // TM,),
        in_specs=[
            pl.BlockSpec((TM, D), row), pl.BlockSpec((TM, D), row),
            pl.BlockSpec((1, 1, D), lambda i: (st.batch_of_block(i, TM), 0, 0)),
            pl.BlockSpec((D, D_FF), const, pipeline_mode=once),
            pl.BlockSpec((D, D_FF), const, pipeline_mode=once),
            pl.BlockSpec((D_FF, D), const, pipeline_mode=once),
        ],
        out_specs=pl.BlockSpec((TM, D), row),
        out_shape=jax.ShapeDtypeStruct((t, D), F32),
        scratch_shapes=[pltpu.VMEM((TM, D), F32)],
        compiler_params=_cparams(("parallel",)),
        name="ffn_dense",
    )(h, x, gt, wg, wu, wd)


def _ffn_expert_kernel(te_ref, tv_ref, x_ref, wg_ref, wu_ref, wd_ref, o_ref, acc_ref):
    k = pl.program_id(0)

    @pl.when(tv_ref[k] == 1)
    def _():
        _swiglu_into(acc_ref, x_ref[...], wg_ref, wu_ref, wd_ref, D_EXP, FF_CHUNK_MOE, lead=(0,))
        o_ref[...] = acc_ref[...].astype(o_ref.dtype)

    @pl.when(tv_ref[k] == 0)
    def _():
        o_ref[...] = jnp.zeros_like(o_ref)


def _ffn_experts(tile_e, tile_v, xp, wg, wu, wd):
    ntiles = xp.shape[0] // MOE_TILE
    once = pl.Buffered(1)
    wmap = lambda k, te, tv: (te[k], 0, 0)
    xmap = lambda k, te, tv: (k * tv[k], 0)
    grid_spec = pltpu.PrefetchScalarGridSpec(
        num_scalar_prefetch=2,
        grid=(ntiles,),
        in_specs=[
            pl.BlockSpec((MOE_TILE, D), xmap),
            pl.BlockSpec((1, D, D_EXP), wmap, pipeline_mode=once),
            pl.BlockSpec((1, D, D_EXP), wmap, pipeline_mode=once),
            pl.BlockSpec((1, D_EXP, D), wmap, pipeline_mode=once),
        ],
        out_specs=pl.BlockSpec((MOE_TILE, D), lambda k, te, tv: (k, 0)),
        scratch_shapes=[pltpu.VMEM((MOE_TILE, D), F32)],
    )
    return pl.pallas_call(
        _ffn_expert_kernel,
        grid_spec=grid_spec,
        out_shape=jax.ShapeDtypeStruct(xp.shape, BF16),
        compiler_params=_cparams(("arbitrary",)),
        name="ffn_experts",
    )(tile_e, tile_v, xp, wg, wu, wd)


def _router_kernel(lg_ref, info_ref, cnt_ref):
    lane = lax.broadcasted_iota(jnp.int32, (DISP_BLK, LANE), 1).astype(F32)
    neg = jnp.float32(-jnp.inf)
    lg = jnp.where(lane < N_EXP, lg_ref[...], neg)
    v1 = jnp.max(lg, axis=1, keepdims=True)
    i1 = jnp.min(jnp.where(lg == v1, lane, float(LANE)), axis=1, keepdims=True)
    oh1 = lane == i1
    lg2 = jnp.where(oh1, neg, lg)
    v2 = jnp.max(lg2, axis=1, keepdims=True)
    i2 = jnp.min(jnp.where(lg2 == v2, lane, float(LANE)), axis=1, keepdims=True)
    oh2 = lane == i2
    e = jnp.exp(v2 - v1)
    w1 = 1.0 / (1.0 + e)
    w2 = e / (1.0 + e)
    oh = oh1.astype(F32) + oh2.astype(F32)
    rr = lax.broadcasted_iota(jnp.int32, (DISP_BLK, DISP_BLK), 0)
    cc = lax.broadcasted_iota(jnp.int32, (DISP_BLK, DISP_BLK), 1)
    earlier = (cc < rr).astype(BF16)
    before = _dot(earlier, oh.astype(BF16))
    pos1 = jnp.sum(jnp.where(oh1, before, 0.0), axis=1, keepdims=True)
    pos2 = jnp.sum(jnp.where(oh2, before, 0.0), axis=1, keepdims=True)
    blk_cnt = jnp.sum(oh, axis=0, keepdims=True)
    cnt_ref[0] = jnp.broadcast_to(blk_cnt, (SUBLANE, LANE))
    packed = jnp.where(lane == 0, i1,
             jnp.where(lane == 1, i2,
             jnp.where(lane == 2, w1,
             jnp.where(lane == 3, w2,
             jnp.where(lane == 4, pos1,
             jnp.where(lane == 5, pos2, 0.0))))))
    info_ref[0] = packed.T[0:SUBLANE, :]


def _router(logits):
    t = logits.shape[0]
    nblk = t // DISP_BLK
    return pl.pallas_call(
        _router_kernel,
        grid=(nblk,),
        in_specs=[pl.BlockSpec((DISP_BLK, LANE), lambda i: (i, 0))],
        out_specs=[pl.BlockSpec((1, SUBLANE, DISP_BLK), lambda i: (i, 0, 0)),
                   pl.BlockSpec((1, SUBLANE, LANE), lambda i: (i, 0, 0))],
        out_shape=[jax.ShapeDtypeStruct((nblk, SUBLANE, DISP_BLK), F32),
                   jax.ShapeDtypeStruct((nblk, SUBLANE, LANE), F32)],
        compiler_params=_cparams(("parallel",)),
        name="router",
    )(logits)


def _dispatch_plan(info, cnt):
    nblk = info.shape[0]
    t = nblk * DISP_BLK
    max_rows = 2 * t + nblk * N_EXP * (SEG_ALIGN - 1) + N_EXP * (MOE_TILE - 1)
    ntiles = -(-max_rows // MOE_TILE)
    e12 = info[:, 0:2, :].astype(jnp.int32)
    w12 = info[:, 2:4, :]
    rank12 = info[:, 4:6, :].astype(jnp.int32)
    bc = cnt[:, 0, :N_EXP].astype(jnp.int32)
    pc = (bc + SEG_ALIGN - 1) // SEG_ALIGN * SEG_ALIGN
    loff = jnp.cumsum(pc, axis=1) - pc
    etot = jnp.sum(pc, axis=0)
    epad = (etot + MOE_TILE - 1) // MOE_TILE * MOE_TILE
    eend = jnp.cumsum(epad)
    goff = (eend - epad)[None, :] + jnp.cumsum(pc, axis=0) - pc
    experts = jnp.arange(N_EXP, dtype=jnp.int32)
    lpos = jnp.sum(jnp.where(e12[..., None] == experts, loff[:, None, None, :], 0), axis=-1) + rank12
    tile_row = jnp.arange(ntiles, dtype=jnp.int32) * MOE_TILE
    tile_e = jnp.minimum(jnp.sum((eend[None, :] <= tile_row[:, None]).astype(jnp.int32), axis=1),
                         N_EXP - 1)
    tile_v = (tile_row < eend[-1]).astype(jnp.int32)
    tail_end = eend.at[N_EXP - 1].set(ntiles * MOE_TILE)
    tail_off = eend - epad + etot
    return dict(lpos=lpos, w12=w12, tile_e=tile_e, tile_v=tile_v, ntiles=ntiles,
                segs=(goff.reshape(-1), loff.reshape(-1), (pc // SEG_ALIGN).reshape(-1)),
                tails=(tail_off, (tail_end - tail_off) // SEG_ALIGN))


def _for_each_segment_group(goff_ref, loff_ref, ngrp_ref, blk, fn):
    for e in range(N_EXP):
        idx = blk * N_EXP + e
        lo, go = loff_ref[idx], goff_ref[idx]

        def body(i, carry):
            fn(pl.multiple_of(lo + i * SEG_ALIGN, SEG_ALIGN), pl.multiple_of(go + i * SEG_ALIGN, SEG_ALIGN))
            return carry

        lax.fori_loop(0, ngrp_ref[idx], body, 0)


def _local_rows(lpos_ref):
    rows = lax.broadcasted_iota(jnp.int32, (LROWS, DISP_BLK), 0)
    return rows == lpos_ref[0, 0:1, :], rows == lpos_ref[0, 1:2, :]


def _gather_kernel(goff_ref, loff_ref, ngrp_ref, toff_ref, tgrp_ref, h_ref, lpos_ref,
                   xp_ref, xl_ref, zero_ref, sem):
    b = pl.program_id(0)
    nblk = pl.num_programs(0)
    slot = b % 2

    def copy(slot_, lrow, srow):
        return pltpu.make_async_copy(xl_ref.at[slot_, pl.ds(lrow, SEG_ALIGN), :],
                                     xp_ref.at[pl.ds(srow, SEG_ALIGN), :], sem.at[slot_])

    def each(blk, fn):
        _for_each_segment_group(goff_ref, loff_ref, ngrp_ref, blk, fn)

    def each_tail(fn):
        for e in range(N_EXP):
            def body(i, carry):
                row = pl.multiple_of(toff_ref[e] + i * SEG_ALIGN, SEG_ALIGN)
                fn(pltpu.make_async_copy(zero_ref, xp_ref.at[pl.ds(row, SEG_ALIGN), :], sem.at[2]))
                return carry

            lax.fori_loop(0, tgrp_ref[e], body, 0)

    @pl.when(b == 0)
    def _():
        zero_ref[...] = jnp.zeros_like(zero_ref)
        each_tail(lambda cp: cp.start())

    @pl.when(b >= 2)
    def _():
        each(b - 2, lambda lrow, srow: copy(slot, lrow, srow).wait())

    hit1, hit2 = _local_rows(lpos_ref)
    sel = jnp.logical_or(hit1, hit2).astype(BF16)
    xl_ref[slot] = _dot(sel, h_ref[...]).astype(BF16)
    each(b, lambda lrow, srow: copy(slot, lrow, srow).start())

    @pl.when(b == nblk - 1)
    def _():
        each(b, lambda lrow, srow: copy(slot, lrow, srow).wait())
        each_tail(lambda cp: cp.wait())

        @pl.when(b >= 1)
        def _():
            each(b - 1, lambda lrow, srow: copy(1 - slot, lrow, srow).wait())


def _gather_rows(plan, h):
    nrows = plan["ntiles"] * MOE_TILE
    nblk = h.shape[0] // DISP_BLK
    grid_spec = pltpu.PrefetchScalarGridSpec(
        num_scalar_prefetch=5,
        grid=(nblk,),
        in_specs=[
            pl.BlockSpec((DISP_BLK, D), lambda b, *_: (b, 0)),
            pl.BlockSpec((1, 2, DISP_BLK), lambda b, *_: (b, 0, 0)),
        ],
        out_specs=pl.BlockSpec(memory_space=pl.ANY),
        scratch_shapes=[pltpu.VMEM((2, LROWS, D), BF16), pltpu.VMEM((SEG_ALIGN, D), BF16),
                        pltpu.SemaphoreType.DMA((3,))],
    )
    return pl.pallas_call(
        _gather_kernel,
        grid_spec=grid_spec,
        out_shape=jax.ShapeDtypeStruct((nrows, D), BF16),
        compiler_params=_cparams(("arbitrary",)),
        name="moe_gather",
    )(*plan["segs"], *plan["tails"], h, plan["lpos"])


def _combine_kernel(goff_ref, loff_ref, ngrp_ref, yp_ref, lpos_ref, w_ref, x_ref, gt_ref, gf_ref,
                    op_ref, os_ref, yl_ref, sem, *, nb0):
    b = pl.program_id(0)
    nblk = pl.num_programs(0)
    slot = b % 2

    def copy(slot_, lrow, srow):
        return pltpu.make_async_copy(yp_ref.at[pl.ds(srow, SEG_ALIGN), :],
                                     yl_ref.at[slot_, pl.ds(lrow, SEG_ALIGN), :], sem.at[slot_])

    def each(blk, fn):
        _for_each_segment_group(goff_ref, loff_ref, ngrp_ref, blk, fn)

    @pl.when(b == 0)
    def _():
        yl_ref[...] = jnp.zeros_like(yl_ref)
        each(0, lambda lrow, srow: copy(0, lrow, srow).start())

    @pl.when(b + 1 < nblk)
    def _():
        each(b + 1, lambda lrow, srow: copy(1 - slot, lrow, srow).start())

    each(b, lambda lrow, srow: copy(slot, lrow, srow).wait())

    hit1, hit2 = _local_rows(lpos_ref)
    wsel = jnp.where(hit1, w_ref[0, 0:1, :], 0.0) + jnp.where(hit2, w_ref[0, 1:2, :], 0.0)
    wh, wl = _split_bf16(wsel)
    tn = (((0,), (0,)), ((), ()))
    yl = yl_ref[slot]
    y = (lax.dot_general(wh, yl, tn, preferred_element_type=F32)
         + lax.dot_general(wl, yl, tn, preferred_element_type=F32))
    x2 = x_ref[...] + gt_ref[0] * y
    ms = jnp.mean(x2 * x2, axis=-1, keepdims=True)
    out = x2 * lax.rsqrt(ms + EPS) * gf_ref[...]

    @pl.when(b < nb0)
    def _():
        op_ref[...] = out

    @pl.when(b >= nb0)
    def _():
        os_ref[...] = out


def _combine(st, plan, yp, x, gt, gf):
    nblk = st.t // DISP_BLK
    nb0 = st.t0 // DISP_BLK
    blk3 = lambda b, *_: (b, 0, 0)
    grid_spec = pltpu.PrefetchScalarGridSpec(
        num_scalar_prefetch=3,
        grid=(nblk,),
        in_specs=[
            pl.BlockSpec(memory_space=pl.ANY),
            pl.BlockSpec((1, 2, DISP_BLK), blk3),
            pl.BlockSpec((1, 2, DISP_BLK), blk3),
            pl.BlockSpec((DISP_BLK, D), lambda b, *_: (b, 0)),
            pl.BlockSpec((1, 1, D), lambda b, *_: (st.batch_of_block(b, DISP_BLK), 0, 0)),
            pl.BlockSpec((1, D), lambda b, *_: (0, 0)),
        ],
        out_specs=[pl.BlockSpec((DISP_BLK, D), lambda b, *_: (jnp.minimum(b, nb0 - 1), 0)),
                   pl.BlockSpec((DISP_BLK, D), lambda b, *_: (jnp.maximum(b - nb0, 0), 0))],
        scratch_shapes=[pltpu.VMEM((2, LROWS, D), BF16), pltpu.SemaphoreType.DMA((2,))],
    )
    return pl.pallas_call(
        functools.partial(_combine_kernel, nb0=nb0),
        grid_spec=grid_spec,
        out_shape=[jax.ShapeDtypeStruct((st.t0, D), F32), jax.ShapeDtypeStruct((st.t - st.t0, D), F32)],
        compiler_params=_cparams(("arbitrary",)),
        name="moe_combine_norm",
    )(*plan["segs"], yp, plan["lpos"], plan["w12"], x, gt, gf)


def _pack_w_in(w):
    o = np.cumsum([0, GLA_DK, GLA_DK, GLA_DV, GLA_DV, RANK, RANK, FNET_W, D, D]).tolist()
    gates = jnp.pad(w[:, o[4]:o[6]], ((0, 0), (0, LANE - 2 * RANK)))
    return jnp.concatenate([w[:, o[0]:o[4]], gates, w[:, o[6]:o[9]]], axis=1).astype(BF16)


def _gate_weights(w_a2, offset):
    return jnp.pad(w_a2, ((offset, LANE - RANK - offset), (0, 0)))


def kernel(x_prompt, x_sample, c_prompt, c_sample, norm1_g, norm2_g, w_ada, b_ada, w_in, w_af2, b_af, w_ab2, b_ab, gla_norm_g, w_o_gla, w_o_fnet, w_out, w_ff_gate, w_ff_up, w_ff_down, w_router, w_e_gate, w_e_up, w_e_down, final_norm_g):
    b, s, _ = x_prompt.shape
    db, ds, _ = x_sample.shape
    depth = w_in.shape[0]
    assert depth == 2 and s % max(GLA_BLK, TM, DISP_BLK, SUBLANE * N1) == 0
    assert ds % max(GLA_BLK, TM, DISP_BLK, SUBLANE * N1) == 0 and (b * s) % ds == 0
    st = _Stream(b, s, db, ds)

    xs = (x_prompt.reshape(b * s, D), x_sample.reshape(db * ds, D))
    c = jnp.concatenate([c_prompt, c_sample], axis=0)
    nb_pad = -(-st.nbatch // SUBLANE) * SUBLANE
    c = jnp.pad(c, ((0, nb_pad - st.nbatch), (0, 0)))
    mod = _modulation(c, w_ada, b_ada)

    def mod_piece(l, j):
        return mod[l, :, j * D:(j + 1) * D].reshape(nb_pad, 1, D)

    for l in range(depth):
        sh1, sc1, gt1, sh2, sc2, gt2 = [mod_piece(l, j) for j in range(N_MOD)]
        q, k, v, r, za, zf, ga, gb = _inproj(st, xs, norm1_g[l].reshape(1, D), sc1, sh1, _pack_w_in(w_in[l]))

        wa_f = _gate_weights(w_af2[l], 0)
        wa_b = _gate_weights(w_ab2[l], RANK)
        o_f = _gla_direction(st, False, q, k, v, za, wa_f, b_af[l].reshape(1, GLA_DK))
        o_gla = _gla_direction(st, True, q, k, v, za, wa_b, b_ab[l].reshape(1, GLA_DK),
                               extra=(o_f, r, gla_norm_g[l].reshape(1, DV)))

        yf = _fnet(st, zf)

        wog, wof, wout = w_o_gla[l].astype(BF16), w_o_fnet[l].astype(BF16), w_out[l].astype(BF16)
        g2 = norm2_g[l].reshape(1, D)
        if l % 2 == 0:
            x1, h2 = _merge(st, o_gla, yf, ga, gb, xs, gt1, g2, sc2, sh2, wog, wof, wout)
            xs = (_ffn_dense(st, h2, x1, gt2, w_ff_gate[l // 2].astype(BF16),
                             w_ff_up[l // 2].astype(BF16), w_ff_down[l // 2].astype(BF16)),)
        else:
            wr = jnp.pad(w_router[l // 2], ((0, 0), (0, LANE - N_EXP)))
            x1, h2, logits = _merge(st, o_gla, yf, ga, gb, xs, gt1, g2, sc2, sh2, wog, wof, wout, wr)
            info, cnt = _router(logits)
            plan = _dispatch_plan(info, cnt)
            xp = _gather_rows(plan, h2)
            yp = _ffn_experts(plan["tile_e"], plan["tile_v"], xp, w_e_gate[l // 2].astype(BF16),
                              w_e_up[l // 2].astype(BF16), w_e_down[l // 2].astype(BF16))
            y_prompt, y_sample = _combine(st, plan, yp, x1, gt2, final_norm_g.reshape(1, D))

    return y_prompt.reshape(b, s, D), y_sample.reshape(db, ds, D)
```

```python
import functools

import numpy as np
import jax
import jax.numpy as jnp
from jax import lax
from jax.experimental import pallas as pl
from jax.experimental.pallas import tpu as pltpu

D = 1024
HEADS = 4
DK = 128
DV = 256
GLA_DK = HEADS * DK
GLA_DV = HEADS * DV
RANK = 16
GATE_NORM = 16.0
CHUNK = 64
FN_GROUPS = 4
FN_GW = 128
FNET_W = FN_GROUPS * FN_GW
D_FF = 2816
N_EXP = 8
D_EXP = 3584
EPS = 1e-6
N_MOD = 6

LANE = 128
SUBLANE = 8
VMEM_LIMIT = 56 * 1024 * 1024
TM = 512
GLA_BLK = 512
MOE_TILE = 512
DISP_BLK = 512
SEG_ALIGN = 16
LROWS = 2 * DISP_BLK + N_EXP * SEG_ALIGN
FF_CHUNK_DENSE = 1408
FF_CHUNK_MOE = 512
N1 = 128

F32 = jnp.float32
BF16 = jnp.bfloat16
_HI = lax.Precision.HIGHEST


def _cparams(sem):
    return pltpu.CompilerParams(dimension_semantics=sem, vmem_limit_bytes=VMEM_LIMIT)


def _dot(a, b):
    return jnp.dot(a, b, preferred_element_type=F32)


def _dot_hi(a, b):
    return jnp.dot(a, b, precision=_HI, preferred_element_type=F32)


def _split_bf16(x):
    hi = x.astype(BF16)
    lo = (x - hi.astype(F32)).astype(BF16)
    return hi, lo


def _dot3(a_hi, a_lo, b_hi, b_lo):
    return _dot(a_hi, b_hi) + (_dot(a_hi, b_lo) + _dot(a_lo, b_hi))


class _Stream:
    def __init__(self, b, s, db, ds):
        self.b, self.s, self.db, self.ds = b, s, db, ds
        self.t0 = b * s
        self.t = b * s + db * ds
        self.nbatch = b + db

    def batch_of_block(self, i, rows):
        nb0, p0, p1 = self.t0 // rows, self.s // rows, self.ds // rows
        return jnp.where(i < nb0, i // p0, self.b + (i - nb0) // p1)

    def block_in_seq(self, i, rows):
        nb0, p0, p1 = self.t0 // rows, self.s // rows, self.ds // rows
        return jnp.where(i < nb0, i % p0, (i - nb0) % p1), jnp.where(i < nb0, p0, p1)


def _mod_kernel(c_ref, w_ref, b_ref, o_ref):
    c = c_ref[...]
    cs = c * jax.nn.sigmoid(c)
    o_ref[0] = _dot_hi(cs, w_ref[0]) + b_ref[0]


def _modulation(c_pad, w_ada, b_ada):
    depth = w_ada.shape[0]
    nb = c_pad.shape[0]
    return pl.pallas_call(
        _mod_kernel,
        grid=(depth, N_MOD),
        in_specs=[
            pl.BlockSpec((nb, D), lambda l, j: (0, 0)),
            pl.BlockSpec((1, D, D), lambda l, j: (l, 0, j)),
            pl.BlockSpec((1, 1, D), lambda l, j: (l, 0, j)),
        ],
        out_specs=pl.BlockSpec((1, nb, D), lambda l, j: (l, 0, j)),
        out_shape=jax.ShapeDtypeStruct((depth, nb, N_MOD * D), F32),
        compiler_params=_cparams(("arbitrary", "arbitrary")),
        name="adaln_mod",
    )(c_pad, w_ada, b_ada.reshape(depth, 1, N_MOD * D))


def _rms_mod(x, g, sc, sh):
    ms = jnp.mean(x * x, axis=-1, keepdims=True)
    return (x * lax.rsqrt(ms + EPS) * g) * (1.0 + sc) + sh


_C_Q, _C_K, _C_V, _C_R, _C_A, _C_F, _C_GA, _C_GB, _C_END = np.cumsum(
    [0, GLA_DK, GLA_DK, GLA_DV, GLA_DV, LANE, FNET_W, D, D]).tolist()


def _stream_specs(st, xs, rows):
    width = xs[0].shape[1]
    if len(xs) == 1:
        return [pl.BlockSpec((rows, width), lambda i, *_: (i, 0))]
    nb0 = st.t0 // rows
    return [pl.BlockSpec((rows, width), lambda i, *_: (jnp.minimum(i, nb0 - 1), 0)),
            pl.BlockSpec((rows, width), lambda i, *_: (jnp.maximum(i - nb0, 0), 0))]


def _stream_block(st, x_refs, rows):
    if len(x_refs) == 1:
        return x_refs[0][...]
    return jnp.where(pl.program_id(0) < st.t0 // rows, x_refs[0][...], x_refs[1][...])


def _inproj_kernel(*refs, st, nx):
    x_refs = refs[:nx]
    (g_ref, sc_ref, sh_ref, w_ref,
     q_ref, k_ref, v_ref, r_ref, a_ref, f_ref, ga_ref, gb_ref) = refs[nx:]
    hb = _rms_mod(_stream_block(st, x_refs, TM), g_ref[...], sc_ref[0], sh_ref[0]).astype(BF16)

    def proj(lo, hi):
        return _dot(hb, w_ref[:, lo:hi])

    q_ref[...] = (proj(_C_Q, _C_K) * (DK ** -0.5)).astype(BF16)
    k_ref[...] = proj(_C_K, _C_V).astype(BF16)
    v_ref[...] = proj(_C_V, _C_R).astype(BF16)
    zr = proj(_C_R, _C_A)
    r_ref[...] = (zr * jax.nn.sigmoid(zr)).astype(BF16)
    a_ref[...] = proj(_C_A, _C_F)
    f_ref[...] = proj(_C_F, _C_GA)
    ga_ref[...] = jax.nn.sigmoid(proj(_C_GA, _C_GB)).astype(BF16)
    gb_ref[...] = jax.nn.sigmoid(proj(_C_GB, _C_END)).astype(BF16)


def _inproj(st, xs, g, sc, sh, w):
    t = st.t
    row = lambda i: (i, 0)
    bat = lambda i: (st.batch_of_block(i, TM), 0, 0)
    widths = [(GLA_DK, BF16), (GLA_DK, BF16), (GLA_DV, BF16), (GLA_DV, BF16),
              (LANE, F32), (FNET_W, F32), (D, BF16), (D, BF16)]
    return pl.pallas_call(
        functools.partial(_inproj_kernel, st=st, nx=len(xs)),
        grid=(t // TM,),
        in_specs=_stream_specs(st, xs, TM) + [
            pl.BlockSpec((1, D), lambda i: (0, 0)),
            pl.BlockSpec((1, 1, D), bat),
            pl.BlockSpec((1, 1, D), bat),
            pl.BlockSpec((D, _C_END), lambda i: (0, 0)),
        ],
        out_specs=[pl.BlockSpec((TM, w_), row) for w_, _ in widths],
        out_shape=[jax.ShapeDtypeStruct((t, w_), dt) for w_, dt in widths],
        compiler_params=_cparams(("parallel",)),
        name="inproj",
    )(*xs, g, sc, sh, w)


def _log_sigmoid(x):
    return jnp.minimum(x, 0.0) - jnp.log(1.0 + jnp.exp(-jnp.abs(x)))


def _gla_kernel(*refs, bwd, st):
    if bwd:
        (q_ref, k_ref, v_ref, za_ref, wa_ref, ba_ref, of_ref, r_ref, ng_ref,
         o_ref, s_scr, qi_scr, ki_scr, qin_scr, kout_scr, att_scr, kv_scr, dec_scr) = refs
    else:
        (q_ref, k_ref, v_ref, za_ref, wa_ref, ba_ref,
         o_ref, s_scr, qi_scr, ki_scr, qin_scr, kout_scr, att_scr, kv_scr, dec_scr) = refs
    i = pl.program_id(0)
    nblk = pl.num_programs(0)
    blk = (nblk - 1 - i) if bwd else i
    local, per = st.block_in_seq(blk, GLA_BLK)
    at_boundary = (local == per - 1) if bwd else (local == 0)

    @pl.when(at_boundary)
    def _():
        s_scr[...] = jnp.zeros_like(s_scr)

    nch = GLA_BLK // CHUNK
    rr = lax.broadcasted_iota(jnp.int32, (CHUNK, CHUNK), 0)
    cc = lax.broadcasted_iota(jnp.int32, (CHUNK, CHUNK), 1)
    keep = (rr <= cc) if bwd else (rr >= cc)
    tri = keep.astype(BF16)
    last = 0 if bwd else CHUNK - 1
    mid = CHUNK // 2 if bwd else CHUNK // 2 - 1

    lane = lax.broadcasted_iota(jnp.int32, (CHUNK, LANE), 1)
    low_part_lanes = jnp.logical_and(lane >= 2 * RANK, lane < 4 * RANK)

    log_decay = {}

    def gate_chunk(c):
        rows = slice(c * CHUNK, (c + 1) * CHUNK)
        zh, zl = _split_bf16(za_ref[rows, :])
        pre = _dot(jnp.where(low_part_lanes, zl, zh), wa_ref[...]) + ba_ref[...]
        g = _log_sigmoid(pre) * (1.0 / GATE_NORM)
        log_decay[c] = _split_bf16(g)

    def scale_chunk(c):
        rows = slice(c * CHUNK, (c + 1) * CHUNK)
        g0, g1 = log_decay.pop(c)
        b = _dot(tri, g0) + _dot(tri, g1)
        b_last = b[last:last + 1, :]
        b_ref = b[mid:mid + 1, :]
        dec_scr[c] = jnp.broadcast_to(jnp.exp(b_last), (LANE, GLA_DK)).T
        q = q_ref[rows, :].astype(F32)
        k = k_ref[rows, :].astype(F32)
        qi_scr[rows, :] = (q * jnp.exp(b - b_ref)).astype(BF16)
        ki_scr[rows, :] = (k * jnp.exp(b_ref - b)).astype(BF16)
        qin_scr[rows, :] = (q * jnp.exp(b)).astype(BF16)
        kout_scr[rows, :] = (k * jnp.exp(b_last - b)).astype(BF16)

    def stateless_products(c):
        rows = slice(c * CHUNK, (c + 1) * CHUNK)
        for h in range(HEADS):
            ks = slice(h * DK, (h + 1) * DK)
            att = lax.dot_general(qi_scr[rows, ks], ki_scr[rows, ks], (((1,), (1,)), ((), ())),
                                  preferred_element_type=F32)
            att_scr[c * HEADS + h] = jnp.where(keep, att, 0.0).astype(BF16)
            kv_scr[c * HEADS + h] = lax.dot_general(
                kout_scr[rows, ks], v_ref[rows, h * DV:(h + 1) * DV], (((0,), (0,)), ((), ())),
                preferred_element_type=F32)

    state = [s_scr[h] for h in range(HEADS)]

    def scan_chunk(c):
        rows = slice(c * CHUNK, (c + 1) * CHUNK)
        for h in range(HEADS):
            ks = slice(h * DK, (h + 1) * DK)
            vs = slice(h * DV, (h + 1) * DV)
            o = _dot(jnp.concatenate([qin_scr[rows, ks], att_scr[c * HEADS + h]], axis=1),
                     jnp.concatenate([state[h].astype(BF16), v_ref[rows, vs]], axis=0))
            decay = dec_scr[c, ks, :]
            state[h] = state[h] * jnp.concatenate([decay] * (DV // LANE), axis=1) + kv_scr[c * HEADS + h]
            if bwd:
                o = o + of_ref[rows, vs]
                o = o * lax.rsqrt(jnp.mean(o * o, axis=-1, keepdims=True) + EPS) * ng_ref[...]
                o_ref[rows, vs] = (o * r_ref[rows, vs].astype(F32)).astype(o_ref.dtype)
            else:
                o_ref[rows, vs] = o

    order = list(reversed(range(nch))) if bwd else list(range(nch))
    stages = (gate_chunk, scale_chunk, stateless_products, scan_chunk)
    for step in range(nch + len(stages) - 1):
        for lag, stage in enumerate(stages):
            if 0 <= step - lag < nch:
                stage(order[step - lag])
    for h in range(HEADS):
        s_scr[h] = state[h]


def _gla_direction(st, bwd, q, k, v, za, wa, ba, extra=()):
    t = st.t
    nblk = t // GLA_BLK
    row = (lambda i: (nblk - 1 - i, 0)) if bwd else (lambda i: (i, 0))
    const = lambda i: (0, 0)
    in_specs = [
        pl.BlockSpec((GLA_BLK, GLA_DK), row),
        pl.BlockSpec((GLA_BLK, GLA_DK), row),
        pl.BlockSpec((GLA_BLK, GLA_DV), row),
        pl.BlockSpec((GLA_BLK, LANE), row),
        pl.BlockSpec((LANE, GLA_DK), const),
        pl.BlockSpec((1, GLA_DK), const),
    ]
    if bwd:
        in_specs += [pl.BlockSpec((GLA_BLK, GLA_DV), row), pl.BlockSpec((GLA_BLK, GLA_DV), row),
                     pl.BlockSpec((1, DV), const)]
    scaled = pltpu.VMEM((GLA_BLK, GLA_DK), BF16)
    nch = GLA_BLK // CHUNK
    return pl.pallas_call(
        functools.partial(_gla_kernel, bwd=bwd, st=st),
        grid=(nblk,),
        in_specs=in_specs,
        out_specs=pl.BlockSpec((GLA_BLK, GLA_DV), row),
        out_shape=jax.ShapeDtypeStruct((t, GLA_DV), BF16 if bwd else F32),
        scratch_shapes=[pltpu.VMEM((HEADS, DK, DV), F32), scaled, scaled, scaled, scaled,
                        pltpu.VMEM((nch * HEADS, CHUNK, CHUNK), BF16),
                        pltpu.VMEM((nch * HEADS, DK, DV), F32),
                        pltpu.VMEM((nch, GLA_DK, LANE), F32)],
        compiler_params=_cparams(("arbitrary",)),
        name="gla_bwd" if bwd else "gla_fwd",
    )(q, k, v, za, wa, ba, *extra)


def _dft_cos_sin(n):
    idx = np.arange(n, dtype=np.float64)
    ang = 2.0 * np.pi * np.outer(idx, idx) / n
    return np.cos(ang) / np.sqrt(n), np.sin(ang) / np.sqrt(n)


def _fnet_stage1_kernel(x_ref, m_ref, y_ref):
    n2 = x_ref.shape[0]
    x = x_ref[...].reshape(n2 * SUBLANE, FNET_W).astype(BF16)
    y_ref[...] = _dot(m_ref[...], x).reshape(2, n2, SUBLANE, FNET_W)


def _fnet_stage1(zf3, bn, n2, seq0, m1):
    return pl.pallas_call(
        _fnet_stage1_kernel,
        grid=(bn, N1 // SUBLANE),
        in_specs=[
            pl.BlockSpec((n2, SUBLANE, FNET_W), lambda b, j: (seq0 + b, j, 0)),
            pl.BlockSpec(m1.shape, lambda b, j: (0, 0)),
        ],
        out_specs=pl.BlockSpec((2, n2, SUBLANE, FNET_W), lambda b, j: (0, b, j, 0)),
        out_shape=jax.ShapeDtypeStruct((2, bn * n2, N1, FNET_W), F32),
        compiler_params=_cparams(("parallel", "parallel")),
        name="fnet_stage1",
    )(zf3, m1)


def _fnet_stage2_kernel(y_ref, tc_ref, ts_ref, m2_ref, m3_ref, o_ref, z_scr):
    for g in range(SUBLANE):
        rows = slice(g * N1, (g + 1) * N1)
        yr = y_ref[0, rows, :]
        yi = y_ref[1, rows, :]
        tc = tc_ref[g]
        ts = ts_ref[g]
        stack = jnp.concatenate([yr * tc + yi * ts, yi * tc - yr * ts], axis=0)
        z = _dot(m2_ref[...], stack.astype(BF16))
        z_scr[rows, 0:FNET_W] = z[:N1].astype(BF16)
        z_scr[rows, FNET_W:2 * FNET_W] = z[N1:].astype(BF16)
    out = _dot(z_scr[...], m3_ref[...])
    for g in range(SUBLANE):
        o_ref[:, g, :] = out[g * N1:(g + 1) * N1]


def _fnet_stage2(y2, bn, n2, tc, ts, m2, m3):
    rows = SUBLANE * N1
    per_seq = n2 // SUBLANE
    const = lambda b, j: (0, 0)
    return pl.pallas_call(
        _fnet_stage2_kernel,
        grid=(bn, per_seq),
        in_specs=[
            pl.BlockSpec((2, rows, FNET_W), lambda b, j: (0, b * per_seq + j, 0)),
            pl.BlockSpec((SUBLANE, N1, 1), lambda b, j: (j, 0, 0)),
            pl.BlockSpec((SUBLANE, N1, 1), lambda b, j: (j, 0, 0)),
            pl.BlockSpec((2 * N1, 2 * N1), const),
            pl.BlockSpec((2 * FNET_W, FNET_W), const),
        ],
        out_specs=pl.BlockSpec((N1, SUBLANE, FNET_W), lambda b, j: (b, j, 0)),
        out_shape=jax.ShapeDtypeStruct((bn * N1, n2, FNET_W), F32),
        scratch_shapes=[pltpu.VMEM((rows, 2 * FNET_W), BF16)],
        compiler_params=_cparams(("parallel", "parallel")),
        name="fnet_stage2",
    )(y2, tc, ts, m2, m3).reshape(bn * n2 * N1, FNET_W)


def _fnet(st, zf):
    as_bf16 = lambda m: jnp.asarray(np.asarray(m, np.float32).astype(BF16))
    c1, s1 = _dft_cos_sin(N1)
    m2 = as_bf16(np.block([[c1, s1], [-s1, c1]]))
    cc, sc = _dft_cos_sin(FN_GW)
    eye = np.eye(FN_GROUPS)
    m3 = as_bf16(np.concatenate([np.kron(eye, cc), np.kron(eye, sc)], axis=0))
    zf3 = zf.reshape(st.t // N1, N1, FNET_W)
    outs = []
    for bn, s, row0 in ((st.b, st.s, 0), (st.db, st.ds, st.t0)):
        n2 = s // N1
        c2, s2 = _dft_cos_sin(n2)
        m1 = as_bf16(np.kron(np.concatenate([c2, -s2], axis=0), np.eye(SUBLANE)))
        ang = 2.0 * np.pi * np.outer(np.arange(n2), np.arange(N1)) / s
        tc = jnp.asarray(np.cos(ang)[:, :, None], F32)
        ts = jnp.asarray(np.sin(ang)[:, :, None], F32)
        y = _fnet_stage1(zf3, bn, n2, row0 // s, m1)
        outs.append(_fnet_stage2(y.reshape(2, bn * s, FNET_W), bn, n2, tc, ts, m2, m3))
    return tuple(outs)


def _merge_kernel(*refs, st, nx, router):
    x_refs = refs[:nx]
    yf_refs = refs[nx:nx + 2]
    (og_ref, ga_ref, gb_ref, gt_ref, g2_ref, sc_ref, sh_ref,
     wog_ref, wof_ref, wout_ref) = refs[nx + 2:nx + 12]
    rest = refs[nx + 12:]
    ya = _dot(og_ref[...], wog_ref[...])
    yb = _dot(_stream_block(st, yf_refs, TM).astype(BF16), wof_ref[...])
    m = ga_ref[...].astype(F32) * ya + gb_ref[...].astype(F32) * yb
    x1 = _stream_block(st, x_refs, TM) + gt_ref[0] * _dot(m.astype(BF16), wout_ref[...])
    h2 = _rms_mod(x1, g2_ref[...], sc_ref[0], sh_ref[0])
    if router:
        wrh_ref, wrl_ref, x1_ref, h2_ref, lg_ref = rest
        lg_ref[...] = _dot3(*_split_bf16(h2), wrh_ref[...], wrl_ref[...])
    else:
        x1_ref, h2_ref = rest
    x1_ref[...] = x1
    h2_ref[...] = h2.astype(BF16)


def _merge(st, og, yfs, ga, gb, xs, gt, g2, sc, sh, wog, wof, wout, wr=None):
    t = st.t
    row = lambda i: (i, 0)
    bat = lambda i: (st.batch_of_block(i, TM), 0, 0)
    const = lambda i: (0, 0)
    in_specs = _stream_specs(st, xs, TM) + _stream_specs(st, yfs, TM) + [
        pl.BlockSpec((TM, GLA_DV), row),
        pl.BlockSpec((TM, D), row), pl.BlockSpec((TM, D), row),
        pl.BlockSpec((1, 1, D), bat), pl.BlockSpec((1, D), const),
        pl.BlockSpec((1, 1, D), bat), pl.BlockSpec((1, 1, D), bat),
        pl.BlockSpec((GLA_DV, D), const), pl.BlockSpec((FNET_W, D), const), pl.BlockSpec((D, D), const),
    ]
    out_specs = [pl.BlockSpec((TM, D), row), pl.BlockSpec((TM, D), row)]
    out_shape = [jax.ShapeDtypeStruct((t, D), F32), jax.ShapeDtypeStruct((t, D), BF16)]
    args = [*xs, *yfs, og, ga, gb, gt, g2, sc, sh, wog, wof, wout]
    if wr is not None:
        in_specs += [pl.BlockSpec((D, LANE), const), pl.BlockSpec((D, LANE), const)]
        out_specs.append(pl.BlockSpec((TM, LANE), row))
        out_shape.append(jax.ShapeDtypeStruct((t, LANE), F32))
        args += list(_split_bf16(wr))
    return pl.pallas_call(
        functools.partial(_merge_kernel, st=st, nx=len(xs), router=wr is not None),
        grid=(t // TM,),
        in_specs=in_specs, out_specs=out_specs, out_shape=out_shape,
        compiler_params=_cparams(("parallel",)),
        name="merge_router" if wr is not None else "merge",
    )(*args)


def _swiglu_into(acc_ref, h, wg_ref, wu_ref, wd_ref, width, chunk, lead=()):
    for c in range(width // chunk):
        cols = slice(c * chunk, (c + 1) * chunk)
        g = _dot(h, wg_ref[lead + (slice(None), cols)])
        u = _dot(h, wu_ref[lead + (slice(None), cols)])
        a = (g * jax.nn.sigmoid(g) * u).astype(BF16)
        part = _dot(a, wd_ref[lead + (cols, slice(None))])
        if c == 0:
            acc_ref[...] = part
        else:
            acc_ref[...] += part


def _ffn_dense_kernel(h_ref, x_ref, gt_ref, wg_ref, wu_ref, wd_ref, o_ref, acc_ref):
    _swiglu_into(acc_ref, h_ref[...], wg_ref, wu_ref, wd_ref, D_FF, FF_CHUNK_DENSE)
    o_ref[...] = x_ref[...] + gt_ref[0] * acc_ref[...]


def _ffn_dense(st, h, x, gt, wg, wu, wd):
    t = st.t
    row = lambda i: (i, 0)
    const = lambda i: (0, 0)
    once = pl.Buffered(1)
    return pl.pallas_call(
        _ffn_dense_kernel,
        grid=(t // TM,),
        in_specs=[
            pl.BlockSpec((TM, D), row), pl.BlockSpec((TM, D), row),
            pl.BlockSpec((1, 1, D), lambda i: (st.batch_of_block(i, TM), 0, 0)),
            pl.BlockSpec((D, D_FF), const, pipeline_mode=once),
            pl.BlockSpec((D, D_FF), const, pipeline_mode=once),
            pl.BlockSpec((D_FF, D), const, pipeline_mode=once),
        ],
        out_specs=pl.BlockSpec((TM, D), row),
        out_shape=jax.ShapeDtypeStruct((t, D), F32),
        scratch_shapes=[pltpu.VMEM((TM, D), F32)],
        compiler_params=_cparams(("parallel",)),
        name="ffn_dense",
    )(h, x, gt, wg, wu, wd)


def _ffn_expert_kernel(te_ref, tv_ref, x_ref, wg_ref, wu_ref, wd_ref, o_ref, acc_ref):
    k = pl.program_id(0)

    @pl.when(tv_ref[k] == 1)
    def _():
        _swiglu_into(acc_ref, x_ref[...], wg_ref, wu_ref, wd_ref, D_EXP, FF_CHUNK_MOE, lead=(0,))
        o_ref[...] = acc_ref[...].astype(o_ref.dtype)

    @pl.when(tv_ref[k] == 0)
    def _():
        o_ref[...] = jnp.zeros_like(o_ref)


def _ffn_experts(tile_e, tile_v, xp, wg, wu, wd):
    ntiles = xp.shape[0] // MOE_TILE
    once = pl.Buffered(1)
    wmap = lambda k, te, tv: (te[k], 0, 0)
    xmap = lambda k, te, tv: (k * tv[k], 0)
    grid_spec = pltpu.PrefetchScalarGridSpec(
        num_scalar_prefetch=2,
        grid=(ntiles,),
        in_specs=[
            pl.BlockSpec((MOE_TILE, D), xmap),
            pl.BlockSpec((1, D, D_EXP), wmap, pipeline_mode=once),
            pl.BlockSpec((1, D, D_EXP), wmap, pipeline_mode=once),
            pl.BlockSpec((1, D_EXP, D), wmap, pipeline_mode=once),
        ],
        out_specs=pl.BlockSpec((MOE_TILE, D), lambda k, te, tv: (k, 0)),
        scratch_shapes=[pltpu.VMEM((MOE_TILE, D), F32)],
    )
    return pl.pallas_call(
        _ffn_expert_kernel,
        grid_spec=grid_spec,
        out_shape=jax.ShapeDtypeStruct(xp.shape, BF16),
        compiler_params=_cparams(("arbitrary",)),
        name="ffn_experts",
    )(tile_e, tile_v, xp, wg, wu, wd)


def _router_kernel(lg_ref, info_ref, cnt_ref):
    lane = lax.broadcasted_iota(jnp.int32, (DISP_BLK, LANE), 1).astype(F32)
    neg = jnp.float32(-jnp.inf)
    lg = jnp.where(lane < N_EXP, lg_ref[...], neg)
    v1 = jnp.max(lg, axis=1, keepdims=True)
    i1 = jnp.min(jnp.where(lg == v1, lane, float(LANE)), axis=1, keepdims=True)
    oh1 = lane == i1
    lg2 = jnp.where(oh1, neg, lg)
    v2 = jnp.max(lg2, axis=1, keepdims=True)
    i2 = jnp.min(jnp.where(lg2 == v2, lane, float(LANE)), axis=1, keepdims=True)
    oh2 = lane == i2
    e = jnp.exp(v2 - v1)
    w1 = 1.0 / (1.0 + e)
    w2 = e / (1.0 + e)
    oh = oh1.astype(F32) + oh2.astype(F32)
    rr = lax.broadcasted_iota(jnp.int32, (DISP_BLK, DISP_BLK), 0)
    cc = lax.broadcasted_iota(jnp.int32, (DISP_BLK, DISP_BLK), 1)
    earlier = (cc < rr).astype(BF16)
    before = _dot(earlier, oh.astype(BF16))
    pos1 = jnp.sum(jnp.where(oh1, before, 0.0), axis=1, keepdims=True)
    pos2 = jnp.sum(jnp.where(oh2, before, 0.0), axis=1, keepdims=True)
    blk_cnt = jnp.sum(oh, axis=0, keepdims=True)
    cnt_ref[0] = jnp.broadcast_to(blk_cnt, (SUBLANE, LANE))
    packed = jnp.where(lane == 0, i1,
             jnp.where(lane == 1, i2,
             jnp.where(lane == 2, w1,
             jnp.where(lane == 3, w2,
             jnp.where(lane == 4, pos1,
             jnp.where(lane == 5, pos2, 0.0))))))
    info_ref[0] = packed.T[0:SUBLANE, :]


def _router(logits):
    t = logits.shape[0]
    nblk = t // DISP_BLK
    return pl.pallas_call(
        _router_kernel,
        grid=(nblk,),
        in_specs=[pl.BlockSpec((DISP_BLK, LANE), lambda i: (i, 0))],
        out_specs=[pl.BlockSpec((1, SUBLANE, DISP_BLK), lambda i: (i, 0, 0)),
                   pl.BlockSpec((1, SUBLANE, LANE), lambda i: (i, 0, 0))],
        out_shape=[jax.ShapeDtypeStruct((nblk, SUBLANE, DISP_BLK), F32),
                   jax.ShapeDtypeStruct((nblk, SUBLANE, LANE), F32)],
        compiler_params=_cparams(("parallel",)),
        name="router",
    )(logits)


def _dispatch_plan(info, cnt):
    nblk = info.shape[0]
    t = nblk * DISP_BLK
    max_rows = 2 * t + nblk * N_EXP * (SEG_ALIGN - 1) + N_EXP * (MOE_TILE - 1)
    ntiles = -(-max_rows // MOE_TILE)
    e12 = info[:, 0:2, :].astype(jnp.int32)
    w12 = info[:, 2:4, :]
    rank12 = info[:, 4:6, :].astype(jnp.int32)
    bc = cnt[:, 0, :N_EXP].astype(jnp.int32)
    pc = (bc + SEG_ALIGN - 1) // SEG_ALIGN * SEG_ALIGN
    loff = jnp.cumsum(pc, axis=1) - pc
    etot = jnp.sum(pc, axis=0)
    epad = (etot + MOE_TILE - 1) // MOE_TILE * MOE_TILE
    eend = jnp.cumsum(epad)
    goff = (eend - epad)[None, :] + jnp.cumsum(pc, axis=0) - pc
    experts = jnp.arange(N_EXP, dtype=jnp.int32)
    lpos = jnp.sum(jnp.where(e12[..., None] == experts, loff[:, None, None, :], 0), axis=-1) + rank12
    tile_row = jnp.arange(ntiles, dtype=jnp.int32) * MOE_TILE
    tile_e = jnp.minimum(jnp.sum((eend[None, :] <= tile_row[:, None]).astype(jnp.int32), axis=1),
                         N_EXP - 1)
    tile_v = (tile_row < eend[-1]).astype(jnp.int32)
    tail_end = eend.at[N_EXP - 1].set(ntiles * MOE_TILE)
    tail_off = eend - epad + etot
    return dict(lpos=lpos, w12=w12, tile_e=tile_e, tile_v=tile_v, ntiles=ntiles,
                segs=(goff.reshape(-1), loff.reshape(-1), (pc // SEG_ALIGN).reshape(-1)),
                tails=(tail_off, (tail_end - tail_off) // SEG_ALIGN))


def _for_each_segment_group(goff_ref, loff_ref, ngrp_ref, blk, fn):
    for e in range(N_EXP):
        idx = blk * N_EXP + e
        lo, go = loff_ref[idx], goff_ref[idx]

        def body(i, carry):
            fn(pl.multiple_of(lo + i * SEG_ALIGN, SEG_ALIGN), pl.multiple_of(go + i * SEG_ALIGN, SEG_ALIGN))
            return carry

        lax.fori_loop(0, ngrp_ref[idx], body, 0)


def _local_rows(lpos_ref):
    rows = lax.broadcasted_iota(jnp.int32, (LROWS, DISP_BLK), 0)
    return rows == lpos_ref[0, 0:1, :], rows == lpos_ref[0, 1:2, :]


def _gather_kernel(goff_ref, loff_ref, ngrp_ref, toff_ref, tgrp_ref, h_ref, lpos_ref,
                   xp_ref, xl_ref, zero_ref, sem):
    b = pl.program_id(0)
    nblk = pl.num_programs(0)
    slot = b % 2

    def copy(slot_, lrow, srow):
        return pltpu.make_async_copy(xl_ref.at[slot_, pl.ds(lrow, SEG_ALIGN), :],
                                     xp_ref.at[pl.ds(srow, SEG_ALIGN), :], sem.at[slot_])

    def each(blk, fn):
        _for_each_segment_group(goff_ref, loff_ref, ngrp_ref, blk, fn)

    def each_tail(fn):
        for e in range(N_EXP):
            def body(i, carry):
                row = pl.multiple_of(toff_ref[e] + i * SEG_ALIGN, SEG_ALIGN)
                fn(pltpu.make_async_copy(zero_ref, xp_ref.at[pl.ds(row, SEG_ALIGN), :], sem.at[2]))
                return carry

            lax.fori_loop(0, tgrp_ref[e], body, 0)

    @pl.when(b == 0)
    def _():
        zero_ref[...] = jnp.zeros_like(zero_ref)
        each_tail(lambda cp: cp.start())

    @pl.when(b >= 2)
    def _():
        each(b - 2, lambda lrow, srow: copy(slot, lrow, srow).wait())

    hit1, hit2 = _local_rows(lpos_ref)
    sel = jnp.logical_or(hit1, hit2).astype(BF16)
    xl_ref[slot] = _dot(sel, h_ref[...]).astype(BF16)
    each(b, lambda lrow, srow: copy(slot, lrow, srow).start())

    @pl.when(b == nblk - 1)
    def _():
        each(b, lambda lrow, srow: copy(slot, lrow, srow).wait())
        each_tail(lambda cp: cp.wait())

        @pl.when(b >= 1)
        def _():
            each(b - 1, lambda lrow, srow: copy(1 - slot, lrow, srow).wait())


def _gather_rows(plan, h):
    nrows = plan["ntiles"] * MOE_TILE
    nblk = h.shape[0] // DISP_BLK
    grid_spec = pltpu.PrefetchScalarGridSpec(
        num_scalar_prefetch=5,
        grid=(nblk,),
        in_specs=[
            pl.BlockSpec((DISP_BLK, D), lambda b, *_: (b, 0)),
            pl.BlockSpec((1, 2, DISP_BLK), lambda b, *_: (b, 0, 0)),
        ],
        out_specs=pl.BlockSpec(memory_space=pl.ANY),
        scratch_shapes=[pltpu.VMEM((2, LROWS, D), BF16), pltpu.VMEM((SEG_ALIGN, D), BF16),
                        pltpu.SemaphoreType.DMA((3,))],
    )
    return pl.pallas_call(
        _gather_kernel,
        grid_spec=grid_spec,
        out_shape=jax.ShapeDtypeStruct((nrows, D), BF16),
        compiler_params=_cparams(("arbitrary",)),
        name="moe_gather",
    )(*plan["segs"], *plan["tails"], h, plan["lpos"])


def _combine_kernel(goff_ref, loff_ref, ngrp_ref, yp_ref, lpos_ref, w_ref, x_ref, gt_ref, gf_ref,
                    op_ref, os_ref, yl_ref, sem, *, nb0):
    b = pl.program_id(0)
    nblk = pl.num_programs(0)
    slot = b % 2

    def copy(slot_, lrow, srow):
        return pltpu.make_async_copy(yp_ref.at[pl.ds(srow, SEG_ALIGN), :],
                                     yl_ref.at[slot_, pl.ds(lrow, SEG_ALIGN), :], sem.at[slot_])

    def each(blk, fn):
        _for_each_segment_group(goff_ref, loff_ref, ngrp_ref, blk, fn)

    @pl.when(b == 0)
    def _():
        yl_ref[...] = jnp.zeros_like(yl_ref)
        each(0, lambda lrow, srow: copy(0, lrow, srow).start())

    @pl.when(b + 1 < nblk)
    def _():
        each(b + 1, lambda lrow, srow: copy(1 - slot, lrow, srow).start())

    each(b, lambda lrow, srow: copy(slot, lrow, srow).wait())

    hit1, hit2 = _local_rows(lpos_ref)
    wsel = jnp.where(hit1, w_ref[0, 0:1, :], 0.0) + jnp.where(hit2, w_ref[0, 1:2, :], 0.0)
    wh, wl = _split_bf16(wsel)
    tn = (((0,), (0,)), ((), ()))
    yl = yl_ref[slot]
    y = (lax.dot_general(wh, yl, tn, preferred_element_type=F32)
         + lax.dot_general(wl, yl, tn, preferred_element_type=F32))
    x2 = x_ref[...] + gt_ref[0] * y
    ms = jnp.mean(x2 * x2, axis=-1, keepdims=True)
    out = x2 * lax.rsqrt(ms + EPS) * gf_ref[...]

    @pl.when(b < nb0)
    def _():
        op_ref[...] = out

    @pl.when(b >= nb0)
    def _():
        os_ref[...] = out


def _combine(st, plan, yp, x, gt, gf):
    nblk = st.t // DISP_BLK
    nb0 = st.t0 // DISP_BLK
    blk3 = lambda b, *_: (b, 0, 0)
    grid_spec = pltpu.PrefetchScalarGridSpec(
        num_scalar_prefetch=3,
        grid=(nblk,),
        in_specs=[
            pl.BlockSpec(memory_space=pl.ANY),
            pl.BlockSpec((1, 2, DISP_BLK), blk3),
            pl.BlockSpec((1, 2, DISP_BLK), blk3),
            pl.BlockSpec((DISP_BLK, D), lambda b, *_: (b, 0)),
            pl.BlockSpec((1, 1, D), lambda b, *_: (st.batch_of_block(b, DISP_BLK), 0, 0)),
            pl.BlockSpec((1, D), lambda b, *_: (0, 0)),
        ],
        out_specs=[pl.BlockSpec((DISP_BLK, D), lambda b, *_: (jnp.minimum(b, nb0 - 1), 0)),
                   pl.BlockSpec((DISP_BLK, D), lambda b, *_: (jnp.maximum(b - nb0, 0), 0))],
        scratch_shapes=[pltpu.VMEM((2, LROWS, D), BF16), pltpu.SemaphoreType.DMA((2,))],
    )
    return pl.pallas_call(
        functools.partial(_combine_kernel, nb0=nb0),
        grid_spec=grid_spec,
        out_shape=[jax.ShapeDtypeStruct((st.t0, D), F32), jax.ShapeDtypeStruct((st.t - st.t0, D), F32)],
        compiler_params=_cparams(("arbitrary",)),
        name="moe_combine_norm",
    )(*plan["segs"], yp, plan["lpos"], plan["w12"], x, gt, gf)


def _pack_w_in(w):
    o = np.cumsum([0, GLA_DK, GLA_DK, GLA_DV, GLA_DV, RANK, RANK, FNET_W, D, D]).tolist()
    gates = jnp.pad(jnp.tile(w[:, o[4]:o[6]], (1, 3)), ((0, 0), (0, LANE - 6 * RANK)))
    return jnp.concatenate([w[:, o[0]:o[4]], gates, w[:, o[6]:o[9]]], axis=1).astype(BF16)


def _gate_weights(w_a2, offset):
    hi, lo = _split_bf16(w_a2)
    out = jnp.zeros((LANE, GLA_DK), BF16)
    for copy, part in enumerate((hi, hi, lo)):
        out = lax.dynamic_update_slice(out, part, (copy * 2 * RANK + offset, 0))
    return out


def kernel(x_prompt, x_sample, c_prompt, c_sample, norm1_g, norm2_g, w_ada, b_ada, w_in, w_af2, b_af, w_ab2, b_ab, gla_norm_g, w_o_gla, w_o_fnet, w_out, w_ff_gate, w_ff_up, w_ff_down, w_router, w_e_gate, w_e_up, w_e_down, final_norm_g):
    b, s, _ = x_prompt.shape
    db, ds, _ = x_sample.shape
    depth = w_in.shape[0]
    assert depth == 2 and s % max(GLA_BLK, TM, DISP_BLK, SUBLANE * N1) == 0
    assert ds % max(GLA_BLK, TM, DISP_BLK, SUBLANE * N1) == 0 and (b * s) % ds == 0
    st = _Stream(b, s, db, ds)

    xs = (x_prompt.reshape(b * s, D), x_sample.reshape(db * ds, D))
    c = jnp.concatenate([c_prompt, c_sample], axis=0)
    nb_pad = -(-st.nbatch // SUBLANE) * SUBLANE
    c = jnp.pad(c, ((0, nb_pad - st.nbatch), (0, 0)))
    mod = _modulation(c, w_ada, b_ada)

    def mod_piece(l, j):
        return mod[l, :, j * D:(j + 1) * D].reshape(nb_pad, 1, D)

    for l in range(depth):
        sh1, sc1, gt1, sh2, sc2, gt2 = [mod_piece(l, j) for j in range(N_MOD)]
        q, k, v, r, za, zf, ga, gb = _inproj(st, xs, norm1_g[l].reshape(1, D), sc1, sh1, _pack_w_in(w_in[l]))

        wa_f = _gate_weights(w_af2[l], 0)
        wa_b = _gate_weights(w_ab2[l], RANK)
        o_f = _gla_direction(st, False, q, k, v, za, wa_f, b_af[l].reshape(1, GLA_DK))
        o_gla = _gla_direction(st, True, q, k, v, za, wa_b, b_ab[l].reshape(1, GLA_DK),
                               extra=(o_f, r, gla_norm_g[l].reshape(1, DV)))

        yf = _fnet(st, zf)

        wog, wof, wout = w_o_gla[l].astype(BF16), w_o_fnet[l].astype(BF16), w_out[l].astype(BF16)
        g2 = norm2_g[l].reshape(1, D)
        if l % 2 == 0:
            x1, h2 = _merge(st, o_gla, yf, ga, gb, xs, gt1, g2, sc2, sh2, wog, wof, wout)
            xs = (_ffn_dense(st, h2, x1, gt2, w_ff_gate[l // 2].astype(BF16),
                             w_ff_up[l // 2].astype(BF16), w_ff_down[l // 2].astype(BF16)),)
        else:
            wr = jnp.pad(w_router[l // 2], ((0, 0), (0, LANE - N_EXP)))
            x1, h2, logits = _merge(st, o_gla, yf, ga, gb, xs, gt1, g2, sc2, sh2, wog, wof, wout, wr)
            info, cnt = _router(logits)
            plan = _dispatch_plan(info, cnt)
            xp = _gather_rows(plan, h2)
            yp = _ffn_experts(plan["tile_e"], plan["tile_v"], xp, w_e_gate[l // 2].astype(BF16),
                              w_e_up[l // 2].astype(BF16), w_e_down[l // 2].astype(BF16))
            y_prompt, y_sample = _combine(st, plan, yp, x1, gt2, final_norm_g.reshape(1, D))

    return y_prompt.reshape(b, s, D), y_sample.reshape(db, ds, D)
```

```python
import functools

import numpy as np
import jax
import jax.numpy as jnp
from jax import lax
from jax.experimental import pallas as pl
from jax.experimental.pallas import tpu as pltpu

D = 1024
HEADS = 4
DK = 128
DV = 256
GLA_DK = HEADS * DK
GLA_DV = HEADS * DV
RANK = 16
GATE_NORM = 16.0
CHUNK = 64
FN_GROUPS = 4
FN_GW = 128
FNET_W = FN_GROUPS * FN_GW
D_FF = 2816
N_EXP = 8
D_EXP = 3584
EPS = 1e-6
N_MOD = 6

LANE = 128
SUBLANE = 8
VMEM_LIMIT = 56 * 1024 * 1024
TM = 512
ROW_PART = 256
GLA_BLK = 512
MOE_TILE = 512
DISP_BLK = 512
SEG_ALIGN = 16
LROWS = 2 * DISP_BLK + N_EXP * SEG_ALIGN
FF_CHUNK_DENSE = 256
FF_CHUNK_MOE = 256
N1 = 128

F32 = jnp.float32
BF16 = jnp.bfloat16
_HI = lax.Precision.HIGHEST


def _cparams(sem):
    return pltpu.CompilerParams(dimension_semantics=sem, vmem_limit_bytes=VMEM_LIMIT)


def _dot(a, b):
    return jnp.dot(a, b, preferred_element_type=F32)


def _dot_hi(a, b):
    return jnp.dot(a, b, precision=_HI, preferred_element_type=F32)


def _split_bf16(x):
    hi = x.astype(BF16)
    lo = (x - hi.astype(F32)).astype(BF16)
    return hi, lo


def _dot3(a_hi, a_lo, b_hi, b_lo):
    return _dot(a_hi, b_hi) + (_dot(a_hi, b_lo) + _dot(a_lo, b_hi))


class _Stream:
    def __init__(self, b, s, db, ds):
        self.b, self.s, self.db, self.ds = b, s, db, ds
        self.t0 = b * s
        self.t = b * s + db * ds
        self.nbatch = b + db

    def batch_of_block(self, i, rows):
        nb0, p0, p1 = self.t0 // rows, self.s // rows, self.ds // rows
        return jnp.where(i < nb0, i // p0, self.b + (i - nb0) // p1)

    def block_in_seq(self, i, rows):
        nb0, p0, p1 = self.t0 // rows, self.s // rows, self.ds // rows
        return jnp.where(i < nb0, i % p0, (i - nb0) % p1), jnp.where(i < nb0, p0, p1)


def _mod_kernel(c_ref, w_ref, b_ref, o_ref):
    c = c_ref[...]
    cs = c * jax.nn.sigmoid(c)
    o_ref[0] = _dot_hi(cs, w_ref[0]) + b_ref[0]


def _modulation(c_pad, w_ada, b_ada):
    depth = w_ada.shape[0]
    nb = c_pad.shape[0]
    return pl.pallas_call(
        _mod_kernel,
        grid=(depth, N_MOD),
        in_specs=[
            pl.BlockSpec((nb, D), lambda l, j: (0, 0)),
            pl.BlockSpec((1, D, D), lambda l, j: (l, 0, j)),
            pl.BlockSpec((1, 1, D), lambda l, j: (l, 0, j)),
        ],
        out_specs=pl.BlockSpec((1, nb, D), lambda l, j: (l, 0, j)),
        out_shape=jax.ShapeDtypeStruct((depth, nb, N_MOD * D), F32),
        compiler_params=_cparams(("arbitrary", "arbitrary")),
        name="adaln_mod",
    )(c_pad, w_ada, b_ada.reshape(depth, 1, N_MOD * D))


def _rms_mod(x, g, sc, sh):
    ms = jnp.mean(x * x, axis=-1, keepdims=True)
    return (x * lax.rsqrt(ms + EPS) * g) * (1.0 + sc) + sh


_C_Q, _C_K, _C_V, _C_R, _C_A, _C_F, _C_GA, _C_GB, _C_END = np.cumsum(
    [0, GLA_DK, GLA_DK, GLA_DV, GLA_DV, LANE, FNET_W, D, D]).tolist()


def _stream_specs(st, xs, rows):
    width = xs[0].shape[1]
    if len(xs) == 1:
        return [pl.BlockSpec((rows, width), lambda i, *_: (i, 0))]
    nb0 = st.t0 // rows
    return [pl.BlockSpec((rows, width), lambda i, *_: (jnp.minimum(i, nb0 - 1), 0)),
            pl.BlockSpec((rows, width), lambda i, *_: (jnp.maximum(i - nb0, 0), 0))]


def _stream_block(st, x_refs, rows, part=slice(None)):
    if len(x_refs) == 1:
        return x_refs[0][part, :]
    return jnp.where(pl.program_id(0) < st.t0 // rows, x_refs[0][part, :], x_refs[1][part, :])


def _row_parts(rows):
    return [slice(p * ROW_PART, (p + 1) * ROW_PART) for p in range(rows // ROW_PART)]


def _inproj_kernel(*refs, st, nx):
    x_refs = refs[:nx]
    (g_ref, sc_ref, sh_ref, w_ref,
     q_ref, k_ref, v_ref, r_ref, a_ref, f_ref, ga_ref, gb_ref) = refs[nx:]
    for part in _row_parts(TM):
        hb = _rms_mod(_stream_block(st, x_refs, TM, part), g_ref[...], sc_ref[0], sh_ref[0]).astype(BF16)

        def proj(lo, hi):
            return _dot(hb, w_ref[:, lo:hi])

        q_ref[part, :] = (proj(_C_Q, _C_K) * (DK ** -0.5)).astype(BF16)
        k_ref[part, :] = proj(_C_K, _C_V).astype(BF16)
        v_ref[part, :] = proj(_C_V, _C_R).astype(BF16)
        zr = proj(_C_R, _C_A)
        r_ref[part, :] = (zr * jax.nn.sigmoid(zr)).astype(BF16)
        a_ref[part, :] = proj(_C_A, _C_F)
        f_ref[part, :] = proj(_C_F, _C_GA)
        ga_ref[part, :] = jax.nn.sigmoid(proj(_C_GA, _C_GB)).astype(BF16)
        gb_ref[part, :] = jax.nn.sigmoid(proj(_C_GB, _C_END)).astype(BF16)


def _inproj(st, xs, g, sc, sh, w):
    t = st.t
    row = lambda i: (i, 0)
    bat = lambda i: (st.batch_of_block(i, TM), 0, 0)
    widths = [(GLA_DK, BF16), (GLA_DK, BF16), (GLA_DV, BF16), (GLA_DV, BF16),
              (LANE, F32), (FNET_W, F32), (D, BF16), (D, BF16)]
    return pl.pallas_call(
        functools.partial(_inproj_kernel, st=st, nx=len(xs)),
        grid=(t // TM,),
        in_specs=_stream_specs(st, xs, TM) + [
            pl.BlockSpec((1, D), lambda i: (0, 0)),
            pl.BlockSpec((1, 1, D), bat),
            pl.BlockSpec((1, 1, D), bat),
            pl.BlockSpec((D, _C_END), lambda i: (0, 0)),
        ],
        out_specs=[pl.BlockSpec((TM, w_), row) for w_, _ in widths],
        out_shape=[jax.ShapeDtypeStruct((t, w_), dt) for w_, dt in widths],
        compiler_params=_cparams(("parallel",)),
        name="inproj",
    )(*xs, g, sc, sh, w)


def _log_sigmoid(x):
    return jnp.minimum(x, 0.0) - jnp.log(1.0 + jnp.exp(-jnp.abs(x)))


def _gla_kernel(*refs, bwd, st):
    if bwd:
        (q_ref, k_ref, v_ref, za_ref, wa_ref, ba_ref, of_ref, r_ref, ng_ref,
         o_ref, s_scr, qi_scr, ki_scr, qin_scr, kout_scr, att_scr, kv_scr, dec_scr) = refs
    else:
        (q_ref, k_ref, v_ref, za_ref, wa_ref, ba_ref,
         o_ref, s_scr, qi_scr, ki_scr, qin_scr, kout_scr, att_scr, kv_scr, dec_scr) = refs
    i = pl.program_id(0)
    nblk = pl.num_programs(0)
    blk = (nblk - 1 - i) if bwd else i
    local, per = st.block_in_seq(blk, GLA_BLK)
    at_boundary = (local == per - 1) if bwd else (local == 0)

    @pl.when(at_boundary)
    def _():
        s_scr[...] = jnp.zeros_like(s_scr)

    nch = GLA_BLK // CHUNK
    rr = lax.broadcasted_iota(jnp.int32, (CHUNK, CHUNK), 0)
    cc = lax.broadcasted_iota(jnp.int32, (CHUNK, CHUNK), 1)
    keep = (rr <= cc) if bwd else (rr >= cc)
    tri = keep.astype(BF16)
    last = 0 if bwd else CHUNK - 1
    mid = CHUNK // 2 if bwd else CHUNK // 2 - 1

    lane = lax.broadcasted_iota(jnp.int32, (CHUNK, LANE), 1)
    low_part_lanes = jnp.logical_and(lane >= 2 * RANK, lane < 4 * RANK)

    log_decay = {}

    def gate_chunk(c):
        rows = slice(c * CHUNK, (c + 1) * CHUNK)
        zh, zl = _split_bf16(za_ref[rows, :])
        pre = _dot(jnp.where(low_part_lanes, zl, zh), wa_ref[...]) + ba_ref[...]
        g = _log_sigmoid(pre) * (1.0 / GATE_NORM)
        log_decay[c] = _split_bf16(g)

    def scale_chunk(c):
        rows = slice(c * CHUNK, (c + 1) * CHUNK)
        g0, g1 = log_decay.pop(c)
        b = _dot(tri, g0) + _dot(tri, g1)
        b_last = b[last:last + 1, :]
        b_ref = b[mid:mid + 1, :]
        dec_scr[c] = jnp.broadcast_to(jnp.exp(b_last), (LANE, GLA_DK)).T
        q = q_ref[rows, :].astype(F32)
        k = k_ref[rows, :].astype(F32)
        qi_scr[rows, :] = (q * jnp.exp(b - b_ref)).astype(BF16)
        ki_scr[rows, :] = (k * jnp.exp(b_ref - b)).astype(BF16)
        qin_scr[rows, :] = (q * jnp.exp(b)).astype(BF16)
        kout_scr[rows, :] = (k * jnp.exp(b_last - b)).astype(BF16)

    def stateless_products(c):
        rows = slice(c * CHUNK, (c + 1) * CHUNK)
        for h in range(HEADS):
            ks = slice(h * DK, (h + 1) * DK)
            att = lax.dot_general(qi_scr[rows, ks], ki_scr[rows, ks], (((1,), (1,)), ((), ())),
                                  preferred_element_type=F32)
            att_scr[c * HEADS + h] = jnp.where(keep, att, 0.0).astype(BF16)
            kv_scr[c * HEADS + h] = lax.dot_general(
                kout_scr[rows, ks], v_ref[rows, h * DV:(h + 1) * DV], (((0,), (0,)), ((), ())),
                preferred_element_type=F32)

    state = [s_scr[h] for h in range(HEADS)]

    def scan_chunk(c):
        rows = slice(c * CHUNK, (c + 1) * CHUNK)
        for h in range(HEADS):
            ks = slice(h * DK, (h + 1) * DK)
            vs = slice(h * DV, (h + 1) * DV)
            o = _dot(jnp.concatenate([qin_scr[rows, ks], att_scr[c * HEADS + h]], axis=1),
                     jnp.concatenate([state[h].astype(BF16), v_ref[rows, vs]], axis=0))
            decay = dec_scr[c, ks, :]
            state[h] = state[h] * jnp.concatenate([decay] * (DV // LANE), axis=1) + kv_scr[c * HEADS + h]
            if bwd:
                o = o + of_ref[rows, vs]
                o = o * lax.rsqrt(jnp.mean(o * o, axis=-1, keepdims=True) + EPS) * ng_ref[...]
                o_ref[rows, vs] = (o * r_ref[rows, vs].astype(F32)).astype(o_ref.dtype)
            else:
                o_ref[rows, vs] = o

    order = list(reversed(range(nch))) if bwd else list(range(nch))
    stages = (gate_chunk, scale_chunk, stateless_products, scan_chunk)
    for step in range(nch + len(stages) - 1):
        for lag, stage in enumerate(stages):
            if 0 <= step - lag < nch:
                stage(order[step - lag])
    for h in range(HEADS):
        s_scr[h] = state[h]


def _gla_direction(st, bwd, q, k, v, za, wa, ba, extra=()):
    t = st.t
    nblk = t // GLA_BLK
    row = (lambda i: (nblk - 1 - i, 0)) if bwd else (lambda i: (i, 0))
    const = lambda i: (0, 0)
    in_specs = [
        pl.BlockSpec((GLA_BLK, GLA_DK), row),
        pl.BlockSpec((GLA_BLK, GLA_DK), row),
        pl.BlockSpec((GLA_BLK, GLA_DV), row),
        pl.BlockSpec((GLA_BLK, LANE), row),
        pl.BlockSpec((LANE, GLA_DK), const),
        pl.BlockSpec((1, GLA_DK), const),
    ]
    if bwd:
        in_specs += [pl.BlockSpec((GLA_BLK, GLA_DV), row), pl.BlockSpec((GLA_BLK, GLA_DV), row),
                     pl.BlockSpec((1, DV), const)]
    scaled = pltpu.VMEM((GLA_BLK, GLA_DK), BF16)
    nch = GLA_BLK // CHUNK
    return pl.pallas_call(
        functools.partial(_gla_kernel, bwd=bwd, st=st),
        grid=(nblk,),
        in_specs=in_specs,
        out_specs=pl.BlockSpec((GLA_BLK, GLA_DV), row),
        out_shape=jax.ShapeDtypeStruct((t, GLA_DV), BF16 if bwd else F32),
        scratch_shapes=[pltpu.VMEM((HEADS, DK, DV), F32), scaled, scaled, scaled, scaled,
                        pltpu.VMEM((nch * HEADS, CHUNK, CHUNK), BF16),
                        pltpu.VMEM((nch * HEADS, DK, DV), F32),
                        pltpu.VMEM((nch, GLA_DK, LANE), F32)],
        compiler_params=_cparams(("arbitrary",)),
        name="gla_bwd" if bwd else "gla_fwd",
    )(q, k, v, za, wa, ba, *extra)


def _dft_cos_sin(n):
    idx = np.arange(n, dtype=np.float64)
    ang = 2.0 * np.pi * np.outer(idx, idx) / n
    return np.cos(ang) / np.sqrt(n), np.sin(ang) / np.sqrt(n)


def _fnet_stage1_kernel(x_ref, m_ref, y_ref):
    n2 = x_ref.shape[0]
    x = x_ref[...].reshape(n2 * SUBLANE, FNET_W).astype(BF16)
    y_ref[...] = _dot(m_ref[...], x).reshape(2, n2, SUBLANE, FNET_W)


def _fnet_stage1(zf3, bn, n2, seq0, m1):
    return pl.pallas_call(
        _fnet_stage1_kernel,
        grid=(bn, N1 // SUBLANE),
        in_specs=[
            pl.BlockSpec((n2, SUBLANE, FNET_W), lambda b, j: (seq0 + b, j, 0)),
            pl.BlockSpec(m1.shape, lambda b, j: (0, 0)),
        ],
        out_specs=pl.BlockSpec((2, n2, SUBLANE, FNET_W), lambda b, j: (0, b, j, 0)),
        out_shape=jax.ShapeDtypeStruct((2, bn * n2, N1, FNET_W), F32),
        compiler_params=_cparams(("parallel", "parallel")),
        name="fnet_stage1",
    )(zf3, m1)


def _fnet_stage2_kernel(y_ref, tc_ref, ts_ref, m2_ref, m3_ref, o_ref, z_scr):
    for g in range(SUBLANE):
        rows = slice(g * N1, (g + 1) * N1)
        yr = y_ref[0, rows, :]
        yi = y_ref[1, rows, :]
        tc = tc_ref[g]
        ts = ts_ref[g]
        stack = jnp.concatenate([yr * tc + yi * ts, yi * tc - yr * ts], axis=0)
        z = _dot(m2_ref[...], stack.astype(BF16))
        z_scr[rows, 0:FNET_W] = z[:N1].astype(BF16)
        z_scr[rows, FNET_W:2 * FNET_W] = z[N1:].astype(BF16)
    out = _dot(z_scr[...], m3_ref[...])
    for g in range(SUBLANE):
        o_ref[:, g, :] = out[g * N1:(g + 1) * N1]


def _fnet_stage2(y2, bn, n2, tc, ts, m2, m3):
    rows = SUBLANE * N1
    per_seq = n2 // SUBLANE
    const = lambda b, j: (0, 0)
    return pl.pallas_call(
        _fnet_stage2_kernel,
        grid=(bn, per_seq),
        in_specs=[
            pl.BlockSpec((2, rows, FNET_W), lambda b, j: (0, b * per_seq + j, 0)),
            pl.BlockSpec((SUBLANE, N1, 1), lambda b, j: (j, 0, 0)),
            pl.BlockSpec((SUBLANE, N1, 1), lambda b, j: (j, 0, 0)),
            pl.BlockSpec((2 * N1, 2 * N1), const),
            pl.BlockSpec((2 * FNET_W, FNET_W), const),
        ],
        out_specs=pl.BlockSpec((N1, SUBLANE, FNET_W), lambda b, j: (b, j, 0)),
        out_shape=jax.ShapeDtypeStruct((bn * N1, n2, FNET_W), F32),
        scratch_shapes=[pltpu.VMEM((rows, 2 * FNET_W), BF16)],
        compiler_params=_cparams(("parallel", "parallel")),
        name="fnet_stage2",
    )(y2, tc, ts, m2, m3).reshape(bn * n2 * N1, FNET_W)


def _fnet(st, zf):
    as_bf16 = lambda m: jnp.asarray(np.asarray(m, np.float32).astype(BF16))
    c1, s1 = _dft_cos_sin(N1)
    m2 = as_bf16(np.block([[c1, s1], [-s1, c1]]))
    cc, sc = _dft_cos_sin(FN_GW)
    eye = np.eye(FN_GROUPS)
    m3 = as_bf16(np.concatenate([np.kron(eye, cc), np.kron(eye, sc)], axis=0))
    zf3 = zf.reshape(st.t // N1, N1, FNET_W)
    outs = []
    for bn, s, row0 in ((st.b, st.s, 0), (st.db, st.ds, st.t0)):
        n2 = s // N1
        c2, s2 = _dft_cos_sin(n2)
        m1 = as_bf16(np.kron(np.concatenate([c2, -s2], axis=0), np.eye(SUBLANE)))
        ang = 2.0 * np.pi * np.outer(np.arange(n2), np.arange(N1)) / s
        tc = jnp.asarray(np.cos(ang)[:, :, None], F32)
        ts = jnp.asarray(np.sin(ang)[:, :, None], F32)
        y = _fnet_stage1(zf3, bn, n2, row0 // s, m1)
        outs.append(_fnet_stage2(y.reshape(2, bn * s, FNET_W), bn, n2, tc, ts, m2, m3))
    return tuple(outs)


def _merge_kernel(*refs, st, nx, router):
    x_refs = refs[:nx]
    yf_refs = refs[nx:nx + 2]
    (og_ref, ga_ref, gb_ref, gt_ref, g2_ref, sc_ref, sh_ref,
     wog_ref, wof_ref, wout_ref) = refs[nx + 2:nx + 12]
    rest = refs[nx + 12:]
    if router:
        wrh_ref, wrl_ref, x1_ref, h2_ref, lg_ref = rest
    else:
        x1_ref, h2_ref = rest
    for part in _row_parts(TM):
        ya = _dot(og_ref[part, :], wog_ref[...])
        yb = _dot(_stream_block(st, yf_refs, TM, part).astype(BF16), wof_ref[...])
        m = ga_ref[part, :].astype(F32) * ya + gb_ref[part, :].astype(F32) * yb
        x1 = _stream_block(st, x_refs, TM, part) + gt_ref[0] * _dot(m.astype(BF16), wout_ref[...])
        h2 = _rms_mod(x1, g2_ref[...], sc_ref[0], sh_ref[0])
        if router:
            lg_ref[part, :] = _dot3(*_split_bf16(h2), wrh_ref[...], wrl_ref[...])
        x1_ref[part, :] = x1
        h2_ref[part, :] = h2.astype(BF16)


def _merge(st, og, yfs, ga, gb, xs, gt, g2, sc, sh, wog, wof, wout, wr=None):
    t = st.t
    row = lambda i: (i, 0)
    bat = lambda i: (st.batch_of_block(i, TM), 0, 0)
    const = lambda i: (0, 0)
    in_specs = _stream_specs(st, xs, TM) + _stream_specs(st, yfs, TM) + [
        pl.BlockSpec((TM, GLA_DV), row),
        pl.BlockSpec((TM, D), row), pl.BlockSpec((TM, D), row),
        pl.BlockSpec((1, 1, D), bat), pl.BlockSpec((1, D), const),
        pl.BlockSpec((1, 1, D), bat), pl.BlockSpec((1, 1, D), bat),
        pl.BlockSpec((GLA_DV, D), const), pl.BlockSpec((FNET_W, D), const), pl.BlockSpec((D, D), const),
    ]
    out_specs = [pl.BlockSpec((TM, D), row), pl.BlockSpec((TM, D), row)]
    out_shape = [jax.ShapeDtypeStruct((t, D), F32), jax.ShapeDtypeStruct((t, D), BF16)]
    args = [*xs, *yfs, og, ga, gb, gt, g2, sc, sh, wog, wof, wout]
    if wr is not None:
        in_specs += [pl.BlockSpec((D, LANE), const), pl.BlockSpec((D, LANE), const)]
        out_specs.append(pl.BlockSpec((TM, LANE), row))
        out_shape.append(jax.ShapeDtypeStruct((t, LANE), F32))
        args += list(_split_bf16(wr))
    return pl.pallas_call(
        functools.partial(_merge_kernel, st=st, nx=len(xs), router=wr is not None),
        grid=(t // TM,),
        in_specs=in_specs, out_specs=out_specs, out_shape=out_shape,
        compiler_params=_cparams(("parallel",)),
        name="merge_router" if wr is not None else "merge",
    )(*args)


def _swiglu_into(acc_ref, h, wg_ref, wu_ref, wd_ref, width, chunk, lead=()):
    for c in range(width // chunk):
        cols = slice(c * chunk, (c + 1) * chunk)
        g = _dot(h, wg_ref[lead + (slice(None), cols)])
        u = _dot(h, wu_ref[lead + (slice(None), cols)])
        a = (g * jax.nn.sigmoid(g) * u).astype(BF16)
        part = _dot(a, wd_ref[lead + (cols, slice(None))])
        if c == 0:
            acc_ref[...] = part
        else:
            acc_ref[...] += part


def _ffn_dense_kernel(h_ref, x_ref, gt_ref, wg_ref, wu_ref, wd_ref, o_ref, acc_ref):
    _swiglu_into(acc_ref, h_ref[...], wg_ref, wu_ref, wd_ref, D_FF, FF_CHUNK_DENSE)
    o_ref[...] = x_ref[...] + gt_ref[0] * acc_ref[...]


def _ffn_dense(st, h, x, gt, wg, wu, wd):
    t = st.t
    row = lambda i: (i, 0)
    const = lambda i: (0, 0)
    once = pl.Buffered(1)
    return pl.pallas_call(
        _ffn_dense_kernel,
        grid=(t // TM,),
        in_specs=[
            pl.BlockSpec((TM, D), row), pl.BlockSpec((TM, D), row),
            pl.BlockSpec((1, 1, D), lambda i: (st.batch_of_block(i, TM), 0, 0)),
            pl.BlockSpec((D, D_FF), const, pipeline_mode=once),
            pl.BlockSpec((D, D_FF), const, pipeline_mode=once),
            pl.BlockSpec((D_FF, D), const, pipeline_mode=once),
        ],
        out_specs=pl.BlockSpec((TM, D), row),
        out_shape=jax.ShapeDtypeStruct((t, D), F32),
        scratch_shapes=[pltpu.VMEM((TM, D), F32)],
        compiler_params=_cparams(("parallel",)),
        name="ffn_dense",
    )(h, x, gt, wg, wu, wd)


def _ffn_expert_kernel(te_ref, tv_ref, x_ref, wg_ref, wu_ref, wd_ref, o_ref, acc_ref):
    k = pl.program_id(0)

    @pl.when(tv_ref[k] == 1)
    def _():
        _swiglu_into(acc_ref, x_ref[...], wg_ref, wu_ref, wd_ref, D_EXP, FF_CHUNK_MOE, lead=(0,))
        o_ref[...] = acc_ref[...].astype(o_ref.dtype)

    @pl.when(tv_ref[k] == 0)
    def _():
        o_ref[...] = jnp.zeros_like(o_ref)


def _ffn_experts(tile_e, tile_v, xp, wg, wu, wd):
    ntiles = xp.shape[0] // MOE_TILE
    once = pl.Buffered(1)
    wmap = lambda k, te, tv: (te[k], 0, 0)
    xmap = lambda k, te, tv: (k * tv[k], 0)
    grid_spec = pltpu.PrefetchScalarGridSpec(
        num_scalar_prefetch=2,
        grid=(ntiles,),
        in_specs=[
            pl.BlockSpec((MOE_TILE, D), xmap),
            pl.BlockSpec((1, D, D_EXP), wmap, pipeline_mode=once),
            pl.BlockSpec((1, D, D_EXP), wmap, pipeline_mode=once),
            pl.BlockSpec((1, D_EXP, D), wmap, pipeline_mode=once),
        ],
        out_specs=pl.BlockSpec((MOE_TILE, D), lambda k, te, tv: (k, 0)),
        scratch_shapes=[pltpu.VMEM((MOE_TILE, D), F32)],
    )
    return pl.pallas_call(
        _ffn_expert_kernel,
        grid_spec=grid_spec,
        out_shape=jax.ShapeDtypeStruct(xp.shape, BF16),
        compiler_params=_cparams(("arbitrary",)),
        name="ffn_experts",
    )(tile_e, tile_v, xp, wg, wu, wd)


def _router_kernel(lg_ref, info_ref, cnt_ref):
    lane = lax.broadcasted_iota(jnp.int32, (DISP_BLK, LANE), 1).astype(F32)
    neg = jnp.float32(-jnp.inf)
    lg = jnp.where(lane < N_EXP, lg_ref[...], neg)
    v1 = jnp.max(lg, axis=1, keepdims=True)
    i1 = jnp.min(jnp.where(lg == v1, lane, float(LANE)), axis=1, keepdims=True)
    oh1 = lane == i1
    lg2 = jnp.where(oh1, neg, lg)
    v2 = jnp.max(lg2, axis=1, keepdims=True)
    i2 = jnp.min(jnp.where(lg2 == v2, lane, float(LANE)), axis=1, keepdims=True)
    oh2 = lane == i2
    e = jnp.exp(v2 - v1)
    w1 = 1.0 / (1.0 + e)
    w2 = e / (1.0 + e)
    oh = oh1.astype(F32) + oh2.astype(F32)
    rr = lax.broadcasted_iota(jnp.int32, (DISP_BLK, DISP_BLK), 0)
    cc = lax.broadcasted_iota(jnp.int32, (DISP_BLK, DISP_BLK), 1)
    earlier = (cc < rr).astype(BF16)
    before = _dot(earlier, oh.astype(BF16))
    pos1 = jnp.sum(jnp.where(oh1, before, 0.0), axis=1, keepdims=True)
    pos2 = jnp.sum(jnp.where(oh2, before, 0.0), axis=1, keepdims=True)
    blk_cnt = jnp.sum(oh, axis=0, keepdims=True)
    cnt_ref[0] = jnp.broadcast_to(blk_cnt, (SUBLANE, LANE))
    packed = jnp.where(lane == 0, i1,
             jnp.where(lane == 1, i2,
             jnp.where(lane == 2, w1,
             jnp.where(lane == 3, w2,
             jnp.where(lane == 4, pos1,
             jnp.where(lane == 5, pos2, 0.0))))))
    info_ref[0] = packed.T[0:SUBLANE, :]


def _router(logits):
    t = logits.shape[0]
    nblk = t // DISP_BLK
    return pl.pallas_call(
        _router_kernel,
        grid=(nblk,),
        in_specs=[pl.BlockSpec((DISP_BLK, LANE), lambda i: (i, 0))],
        out_specs=[pl.BlockSpec((1, SUBLANE, DISP_BLK), lambda i: (i, 0, 0)),
                   pl.BlockSpec((1, SUBLANE, LANE), lambda i: (i, 0, 0))],
        out_shape=[jax.ShapeDtypeStruct((nblk, SUBLANE, DISP_BLK), F32),
                   jax.ShapeDtypeStruct((nblk, SUBLANE, LANE), F32)],
        compiler_params=_cparams(("parallel",)),
        name="router",
    )(logits)


def _dispatch_plan(info, cnt):
    nblk = info.shape[0]
    t = nblk * DISP_BLK
    max_rows = 2 * t + nblk * N_EXP * (SEG_ALIGN - 1) + N_EXP * (MOE_TILE - 1)
    ntiles = -(-max_rows // MOE_TILE)
    e12 = info[:, 0:2, :].astype(jnp.int32)
    w12 = info[:, 2:4, :]
    rank12 = info[:, 4:6, :].astype(jnp.int32)
    bc = cnt[:, 0, :N_EXP].astype(jnp.int32)
    pc = (bc + SEG_ALIGN - 1) // SEG_ALIGN * SEG_ALIGN
    loff = jnp.cumsum(pc, axis=1) - pc
    etot = jnp.sum(pc, axis=0)
    epad = (etot + MOE_TILE - 1) // MOE_TILE * MOE_TILE
    eend = jnp.cumsum(epad)
    goff = (eend - epad)[None, :] + jnp.cumsum(pc, axis=0) - pc
    experts = jnp.arange(N_EXP, dtype=jnp.int32)
    lpos = jnp.sum(jnp.where(e12[..., None] == experts, loff[:, None, None, :], 0), axis=-1) + rank12
    tile_row = jnp.arange(ntiles, dtype=jnp.int32) * MOE_TILE
    tile_e = jnp.minimum(jnp.sum((eend[None, :] <= tile_row[:, None]).astype(jnp.int32), axis=1),
                         N_EXP - 1)
    tile_v = (tile_row < eend[-1]).astype(jnp.int32)
    tail_end = eend.at[N_EXP - 1].set(ntiles * MOE_TILE)
    tail_off = eend - epad + etot
    return dict(lpos=lpos, w12=w12, tile_e=tile_e, tile_v=tile_v, ntiles=ntiles,
                segs=(goff.reshape(-1), loff.reshape(-1), (pc // SEG_ALIGN).reshape(-1)),
                tails=(tail_off, (tail_end - tail_off) // SEG_ALIGN))


def _for_each_segment_group(goff_ref, loff_ref, ngrp_ref, blk, fn):
    for e in range(N_EXP):
        idx = blk * N_EXP + e
        lo, go = loff_ref[idx], goff_ref[idx]

        def body(i, carry):
            fn(pl.multiple_of(lo + i * SEG_ALIGN, SEG_ALIGN), pl.multiple_of(go + i * SEG_ALIGN, SEG_ALIGN))
            return carry

        lax.fori_loop(0, ngrp_ref[idx], body, 0)


def _local_rows(lpos_ref):
    rows = lax.broadcasted_iota(jnp.int32, (LROWS, DISP_BLK), 0)
    return rows == lpos_ref[0, 0:1, :], rows == lpos_ref[0, 1:2, :]


def _gather_kernel(goff_ref, loff_ref, ngrp_ref, toff_ref, tgrp_ref, h_ref, lpos_ref,
                   xp_ref, xl_ref, zero_ref, sem):
    b = pl.program_id(0)
    nblk = pl.num_programs(0)
    slot = b % 2

    def copy(slot_, lrow, srow):
        return pltpu.make_async_copy(xl_ref.at[slot_, pl.ds(lrow, SEG_ALIGN), :],
                                     xp_ref.at[pl.ds(srow, SEG_ALIGN), :], sem.at[slot_])

    def each(blk, fn):
        _for_each_segment_group(goff_ref, loff_ref, ngrp_ref, blk, fn)

    def each_tail(fn):
        for e in range(N_EXP):
            def body(i, carry):
                row = pl.multiple_of(toff_ref[e] + i * SEG_ALIGN, SEG_ALIGN)
                fn(pltpu.make_async_copy(zero_ref, xp_ref.at[pl.ds(row, SEG_ALIGN), :], sem.at[2]))
                return carry

            lax.fori_loop(0, tgrp_ref[e], body, 0)

    @pl.when(b == 0)
    def _():
        zero_ref[...] = jnp.zeros_like(zero_ref)
        each_tail(lambda cp: cp.start())

    @pl.when(b >= 2)
    def _():
        each(b - 2, lambda lrow, srow: copy(slot, lrow, srow).wait())

    hit1, hit2 = _local_rows(lpos_ref)
    sel = jnp.logical_or(hit1, hit2).astype(BF16)
    xl_ref[slot] = _dot(sel, h_ref[...]).astype(BF16)
    each(b, lambda lrow, srow: copy(slot, lrow, srow).start())

    @pl.when(b == nblk - 1)
    def _():
        each(b, lambda lrow, srow: copy(slot, lrow, srow).wait())
        each_tail(lambda cp: cp.wait())

        @pl.when(b >= 1)
        def _():
            each(b - 1, lambda lrow, srow: copy(1 - slot, lrow, srow).wait())


def _gather_rows(plan, h):
    nrows = plan["ntiles"] * MOE_TILE
    nblk = h.shape[0] // DISP_BLK
    grid_spec = pltpu.PrefetchScalarGridSpec(
        num_scalar_prefetch=5,
        grid=(nblk,),
        in_specs=[
            pl.BlockSpec((DISP_BLK, D), lambda b, *_: (b, 0)),
            pl.BlockSpec((1, 2, DISP_BLK), lambda b, *_: (b, 0, 0)),
        ],
        out_specs=pl.BlockSpec(memory_space=pl.ANY),
        scratch_shapes=[pltpu.VMEM((2, LROWS, D), BF16), pltpu.VMEM((SEG_ALIGN, D), BF16),
                        pltpu.SemaphoreType.DMA((3,))],
    )
    return pl.pallas_call(
        _gather_kernel,
        grid_spec=grid_spec,
        out_shape=jax.ShapeDtypeStruct((nrows, D), BF16),
        compiler_params=_cparams(("arbitrary",)),
        name="moe_gather",
    )(*plan["segs"], *plan["tails"], h, plan["lpos"])


def _combine_kernel(goff_ref, loff_ref, ngrp_ref, yp_ref, lpos_ref, w_ref, x_ref, gt_ref, gf_ref,
                    op_ref, os_ref, yl_ref, sem, *, nb0):
    b = pl.program_id(0)
    nblk = pl.num_programs(0)
    slot = b % 2

    def copy(slot_, lrow, srow):
        return pltpu.make_async_copy(yp_ref.at[pl.ds(srow, SEG_ALIGN), :],
                                     yl_ref.at[slot_, pl.ds(lrow, SEG_ALIGN), :], sem.at[slot_])

    def each(blk, fn):
        _for_each_segment_group(goff_ref, loff_ref, ngrp_ref, blk, fn)

    @pl.when(b == 0)
    def _():
        yl_ref[...] = jnp.zeros_like(yl_ref)
        each(0, lambda lrow, srow: copy(0, lrow, srow).start())

    @pl.when(b + 1 < nblk)
    def _():
        each(b + 1, lambda lrow, srow: copy(1 - slot, lrow, srow).start())

    each(b, lambda lrow, srow: copy(slot, lrow, srow).wait())

    hit1, hit2 = _local_rows(lpos_ref)
    wsel = jnp.where(hit1, w_ref[0, 0:1, :], 0.0) + jnp.where(hit2, w_ref[0, 1:2, :], 0.0)
    wh, wl = _split_bf16(wsel)
    tn = (((0,), (0,)), ((), ()))
    yl = yl_ref[slot]
    y = (lax.dot_general(wh, yl, tn, preferred_element_type=F32)
         + lax.dot_general(wl, yl, tn, preferred_element_type=F32))
    x2 = x_ref[...] + gt_ref[0] * y
    ms = jnp.mean(x2 * x2, axis=-1, keepdims=True)
    out = x2 * lax.rsqrt(ms + EPS) * gf_ref[...]

    @pl.when(b < nb0)
    def _():
        op_ref[...] = out

    @pl.when(b >= nb0)
    def _():
        os_ref[...] = out


def _combine(st, plan, yp, x, gt, gf):
    nblk = st.t // DISP_BLK
    nb0 = st.t0 // DISP_BLK
    blk3 = lambda b, *_: (b, 0, 0)
    grid_spec = pltpu.PrefetchScalarGridSpec(
        num_scalar_prefetch=3,
        grid=(nblk,),
        in_specs=[
            pl.BlockSpec(memory_space=pl.ANY),
            pl.BlockSpec((1, 2, DISP_BLK), blk3),
            pl.BlockSpec((1, 2, DISP_BLK), blk3),
            pl.BlockSpec((DISP_BLK, D), lambda b, *_: (b, 0)),
            pl.BlockSpec((1, 1, D), lambda b, *_: (st.batch_of_block(b, DISP_BLK), 0, 0)),
            pl.BlockSpec((1, D), lambda b, *_: (0, 0)),
        ],
        out_specs=[pl.BlockSpec((DISP_BLK, D), lambda b, *_: (jnp.minimum(b, nb0 - 1), 0)),
                   pl.BlockSpec((DISP_BLK, D), lambda b, *_: (jnp.maximum(b - nb0, 0), 0))],
        scratch_shapes=[pltpu.VMEM((2, LROWS, D), BF16), pltpu.SemaphoreType.DMA((2,))],
    )
    return pl.pallas_call(
        functools.partial(_combine_kernel, nb0=nb0),
        grid_spec=grid_spec,
        out_shape=[jax.ShapeDtypeStruct((st.t0, D), F32), jax.ShapeDtypeStruct((st.t - st.t0, D), F32)],
        compiler_params=_cparams(("arbitrary",)),
        name="moe_combine_norm",
    )(*plan["segs"], yp, plan["lpos"], plan["w12"], x, gt, gf)


def _pack_w_in(w):
    o = np.cumsum([0, GLA_DK, GLA_DK, GLA_DV, GLA_DV, RANK, RANK, FNET_W, D, D]).tolist()
    gates = jnp.pad(jnp.tile(w[:, o[4]:o[6]], (1, 3)), ((0, 0), (0, LANE - 6 * RANK)))
    return jnp.concatenate([w[:, o[0]:o[4]], gates, w[:, o[6]:o[9]]], axis=1).astype(BF16)


def _gate_weights(w_a2, offset):
    hi, lo = _split_bf16(w_a2)
    out = jnp.zeros((LANE, GLA_DK), BF16)
    for copy, part in enumerate((hi, hi, lo)):
        out = lax.dynamic_update_slice(out, part, (copy * 2 * RANK + offset, 0))
    return out


def kernel(x_prompt, x_sample, c_prompt, c_sample, norm1_g, norm2_g, w_ada, b_ada, w_in, w_af2, b_af, w_ab2, b_ab, gla_norm_g, w_o_gla, w_o_fnet, w_out, w_ff_gate, w_ff_up, w_ff_down, w_router, w_e_gate, w_e_up, w_e_down, final_norm_g):
    b, s, _ = x_prompt.shape
    db, ds, _ = x_sample.shape
    depth = w_in.shape[0]
    assert depth == 2 and s % max(GLA_BLK, TM, DISP_BLK, SUBLANE * N1) == 0
    assert ds % max(GLA_BLK, TM, DISP_BLK, SUBLANE * N1) == 0 and (b * s) % ds == 0
    st = _Stream(b, s, db, ds)

    xs = (x_prompt.reshape(b * s, D), x_sample.reshape(db * ds, D))
    c = jnp.concatenate([c_prompt, c_sample], axis=0)
    nb_pad = -(-st.nbatch // SUBLANE) * SUBLANE
    c = jnp.pad(c, ((0, nb_pad - st.nbatch), (0, 0)))
    mod = _modulation(c, w_ada, b_ada)

    def mod_piece(l, j):
        return mod[l, :, j * D:(j + 1) * D].reshape(nb_pad, 1, D)

    for l in range(depth):
        sh1, sc1, gt1, sh2, sc2, gt2 = [mod_piece(l, j) for j in range(N_MOD)]
        q, k, v, r, za, zf, ga, gb = _inproj(st, xs, norm1_g[l].reshape(1, D), sc1, sh1, _pack_w_in(w_in[l]))

        wa_f = _gate_weights(w_af2[l], 0)
        wa_b = _gate_weights(w_ab2[l], RANK)
        o_f = _gla_direction(st, False, q, k, v, za, wa_f, b_af[l].reshape(1, GLA_DK))
        o_gla = _gla_direction(st, True, q, k, v, za, wa_b, b_ab[l].reshape(1, GLA_DK),
                               extra=(o_f, r, gla_norm_g[l].reshape(1, DV)))

        yf = _fnet(st, zf)

        wog, wof, wout = w_o_gla[l].astype(BF16), w_o_fnet[l].astype(BF16), w_out[l].astype(BF16)
        g2 = norm2_g[l].reshape(1, D)
        if l % 2 == 0:
            x1, h2 = _merge(st, o_gla, yf, ga, gb, xs, gt1, g2, sc2, sh2, wog, wof, wout)
            xs = (_ffn_dense(st, h2, x1, gt2, w_ff_gate[l // 2].astype(BF16),
                             w_ff_up[l // 2].astype(BF16), w_ff_down[l // 2].astype(BF16)),)
        else:
            wr = jnp.pad(w_router[l // 2], ((0, 0), (0, LANE - N_EXP)))
            x1, h2, logits = _merge(st, o_gla, yf, ga, gb, xs, gt1, g2, sc2, sh2, wog, wof, wout, wr)
            info, cnt = _router(logits)
            plan = _dispatch_plan(info, cnt)
            xp = _gather_rows(plan, h2)
            yp = _ffn_experts(plan["tile_e"], plan["tile_v"], xp, w_e_gate[l // 2].astype(BF16),
                              w_e_up[l // 2].astype(BF16), w_e_down[l // 2].astype(BF16))
            y_prompt, y_sample = _combine(st, plan, yp, x1, gt2, final_norm_g.reshape(1, D))

    return y_prompt.reshape(b, s, D), y_sample.reshape(db, ds, D)
```

```python
import functools

import numpy as np
import jax
import jax.numpy as jnp
from jax import lax
from jax.experimental import pallas as pl
from jax.experimental.pallas import tpu as pltpu

D = 1024
HEADS = 4
DK = 128
DV = 256
GLA_DK = HEADS * DK
GLA_DV = HEADS * DV
RANK = 16
GATE_NORM = 16.0
CHUNK = 64
FN_GROUPS = 4
FN_GW = 128
FNET_W = FN_GROUPS * FN_GW
D_FF = 2816
N_EXP = 8
D_EXP = 3584
EPS = 1e-6
N_MOD = 6

LANE = 128
SUBLANE = 8
VMEM_LIMIT = 56 * 1024 * 1024
TM = 512
ROW_PART = 256
GLA_BLK = 1024
MOE_TILE = 512
DISP_BLK = 512
SEG_ALIGN = 16
SEG_PIECE = 128
LROWS = 2 * DISP_BLK + N_EXP * SEG_ALIGN
FF_CHUNK_DENSE = 256
FF_CHUNK_MOE = 256
N1 = 128

F32 = jnp.float32
BF16 = jnp.bfloat16
_HI = lax.Precision.HIGHEST


def _cparams(sem):
    return pltpu.CompilerParams(dimension_semantics=sem, vmem_limit_bytes=VMEM_LIMIT)


def _dot(a, b):
    return jnp.dot(a, b, preferred_element_type=F32)


def _dot_hi(a, b):
    return jnp.dot(a, b, precision=_HI, preferred_element_type=F32)


def _split_bf16(x):
    hi = x.astype(BF16)
    lo = (x - hi.astype(F32)).astype(BF16)
    return hi, lo


def _dot3(a_hi, a_lo, b_hi, b_lo):
    return _dot(a_hi, b_hi) + (_dot(a_hi, b_lo) + _dot(a_lo, b_hi))


class _Stream:
    def __init__(self, b, s, db, ds):
        self.b, self.s, self.db, self.ds = b, s, db, ds
        self.t0 = b * s
        self.t = b * s + db * ds
        self.nbatch = b + db

    def batch_of_block(self, i, rows):
        nb0, p0, p1 = self.t0 // rows, self.s // rows, self.ds // rows
        return jnp.where(i < nb0, i // p0, self.b + (i - nb0) // p1)

    def block_in_seq(self, i, rows):
        nb0, p0, p1 = self.t0 // rows, self.s // rows, self.ds // rows
        return jnp.where(i < nb0, i % p0, (i - nb0) % p1), jnp.where(i < nb0, p0, p1)


def _mod_kernel(c_ref, w_ref, b_ref, o_ref):
    c = c_ref[...]
    cs = c * jax.nn.sigmoid(c)
    o_ref[0] = _dot_hi(cs, w_ref[0]) + b_ref[0]


def _modulation(c_pad, w_ada, b_ada):
    depth = w_ada.shape[0]
    nb = c_pad.shape[0]
    return pl.pallas_call(
        _mod_kernel,
        grid=(depth, N_MOD),
        in_specs=[
            pl.BlockSpec((nb, D), lambda l, j: (0, 0)),
            pl.BlockSpec((1, D, D), lambda l, j: (l, 0, j)),
            pl.BlockSpec((1, 1, D), lambda l, j: (l, 0, j)),
        ],
        out_specs=pl.BlockSpec((1, nb, D), lambda l, j: (l, 0, j)),
        out_shape=jax.ShapeDtypeStruct((depth, nb, N_MOD * D), F32),
        compiler_params=_cparams(("arbitrary", "arbitrary")),
        name="adaln_mod",
    )(c_pad, w_ada, b_ada.reshape(depth, 1, N_MOD * D))


def _rms_mod(x, g, sc, sh):
    ms = jnp.mean(x * x, axis=-1, keepdims=True)
    return (x * lax.rsqrt(ms + EPS) * g) * (1.0 + sc) + sh


_C_Q, _C_K, _C_V, _C_R, _C_A, _C_F, _C_GA, _C_GB, _C_END = np.cumsum(
    [0, GLA_DK, GLA_DK, GLA_DV, GLA_DV, LANE, FNET_W, D, D]).tolist()


def _stream_specs(st, xs, rows):
    width = xs[0].shape[1]
    if len(xs) == 1:
        return [pl.BlockSpec((rows, width), lambda i, *_: (i, 0))]
    nb0 = st.t0 // rows
    return [pl.BlockSpec((rows, width), lambda i, *_: (jnp.minimum(i, nb0 - 1), 0)),
            pl.BlockSpec((rows, width), lambda i, *_: (jnp.maximum(i - nb0, 0), 0))]


def _stream_block(st, x_refs, rows, part=slice(None)):
    if len(x_refs) == 1:
        return x_refs[0][part, :]
    return jnp.where(pl.program_id(0) < st.t0 // rows, x_refs[0][part, :], x_refs[1][part, :])


def _row_parts(rows):
    return [slice(p * ROW_PART, (p + 1) * ROW_PART) for p in range(rows // ROW_PART)]


def _inproj_kernel(*refs, st, nx):
    x_refs = refs[:nx]
    (g_ref, sc_ref, sh_ref, w_ref,
     q_ref, k_ref, v_ref, r_ref, a_ref, f_ref, ga_ref, gb_ref) = refs[nx:]
    for part in _row_parts(TM):
        hb = _rms_mod(_stream_block(st, x_refs, TM, part), g_ref[...], sc_ref[0], sh_ref[0]).astype(BF16)

        def proj(lo, hi):
            return _dot(hb, w_ref[:, lo:hi])

        q_ref[part, :] = (proj(_C_Q, _C_K) * (DK ** -0.5)).astype(BF16)
        k_ref[part, :] = proj(_C_K, _C_V).astype(BF16)
        v_ref[part, :] = proj(_C_V, _C_R).astype(BF16)
        zr = proj(_C_R, _C_A)
        r_ref[part, :] = (zr * jax.nn.sigmoid(zr)).astype(BF16)
        a_ref[part, :] = proj(_C_A, _C_F)
        f_ref[part, :] = proj(_C_F, _C_GA)
        ga_ref[part, :] = jax.nn.sigmoid(proj(_C_GA, _C_GB)).astype(BF16)
        gb_ref[part, :] = jax.nn.sigmoid(proj(_C_GB, _C_END)).astype(BF16)


def _inproj(st, xs, g, sc, sh, w):
    t = st.t
    row = lambda i: (i, 0)
    bat = lambda i: (st.batch_of_block(i, TM), 0, 0)
    widths = [(GLA_DK, BF16), (GLA_DK, BF16), (GLA_DV, BF16), (GLA_DV, BF16),
              (LANE, F32), (FNET_W, F32), (D, BF16), (D, BF16)]
    return pl.pallas_call(
        functools.partial(_inproj_kernel, st=st, nx=len(xs)),
        grid=(t // TM,),
        in_specs=_stream_specs(st, xs, TM) + [
            pl.BlockSpec((1, D), lambda i: (0, 0)),
            pl.BlockSpec((1, 1, D), bat),
            pl.BlockSpec((1, 1, D), bat),
            pl.BlockSpec((D, _C_END), lambda i: (0, 0)),
        ],
        out_specs=[pl.BlockSpec((TM, w_), row) for w_, _ in widths],
        out_shape=[jax.ShapeDtypeStruct((t, w_), dt) for w_, dt in widths],
        compiler_params=_cparams(("parallel",)),
        name="inproj",
    )(*xs, g, sc, sh, w)


def _log_sigmoid(x):
    return jnp.minimum(x, 0.0) - jnp.log(1.0 + jnp.exp(-jnp.abs(x)))


def _gla_kernel(*refs, bwd, st):
    if bwd:
        (q_ref, k_ref, v_ref, za_ref, wa_ref, ba_ref, of_ref, r_ref, ng_ref,
         o_ref, s_scr, qi_scr, ki_scr, qin_scr, kout_scr, att_scr, kv_scr, dec_scr) = refs
    else:
        (q_ref, k_ref, v_ref, za_ref, wa_ref, ba_ref,
         o_ref, s_scr, qi_scr, ki_scr, qin_scr, kout_scr, att_scr, kv_scr, dec_scr) = refs
    i = pl.program_id(0)
    nblk = pl.num_programs(0)
    blk = (nblk - 1 - i) if bwd else i
    local, per = st.block_in_seq(blk, GLA_BLK)
    at_boundary = (local == per - 1) if bwd else (local == 0)

    @pl.when(at_boundary)
    def _():
        s_scr[...] = jnp.zeros_like(s_scr)

    nch = GLA_BLK // CHUNK
    rr = lax.broadcasted_iota(jnp.int32, (CHUNK, CHUNK), 0)
    cc = lax.broadcasted_iota(jnp.int32, (CHUNK, CHUNK), 1)
    keep = (rr <= cc) if bwd else (rr >= cc)
    tri = keep.astype(BF16)
    last = 0 if bwd else CHUNK - 1
    mid = CHUNK // 2 if bwd else CHUNK // 2 - 1

    lane = lax.broadcasted_iota(jnp.int32, (CHUNK, LANE), 1)
    low_part_lanes = jnp.logical_and(lane >= 2 * RANK, lane < 4 * RANK)

    log_decay = {}

    def gate_chunk(c):
        rows = slice(c * CHUNK, (c + 1) * CHUNK)
        zh, zl = _split_bf16(za_ref[rows, :])
        pre = _dot(jnp.where(low_part_lanes, zl, zh), wa_ref[...]) + ba_ref[...]
        g = _log_sigmoid(pre) * (1.0 / GATE_NORM)
        log_decay[c] = _split_bf16(g)

    def scale_chunk(c):
        rows = slice(c * CHUNK, (c + 1) * CHUNK)
        g0, g1 = log_decay.pop(c)
        b = _dot(tri, g0) + _dot(tri, g1)
        b_last = b[last:last + 1, :]
        b_ref = b[mid:mid + 1, :]
        dec_scr[c] = jnp.broadcast_to(jnp.exp(b_last), (LANE, GLA_DK)).T
        q = q_ref[rows, :].astype(F32)
        k = k_ref[rows, :].astype(F32)
        qi_scr[rows, :] = (q * jnp.exp(b - b_ref)).astype(BF16)
        ki_scr[rows, :] = (k * jnp.exp(b_ref - b)).astype(BF16)
        qin_scr[rows, :] = (q * jnp.exp(b)).astype(BF16)
        kout_scr[rows, :] = (k * jnp.exp(b_last - b)).astype(BF16)

    def stateless_products(c):
        rows = slice(c * CHUNK, (c + 1) * CHUNK)
        for h in range(HEADS):
            ks = slice(h * DK, (h + 1) * DK)
            att = lax.dot_general(qi_scr[rows, ks], ki_scr[rows, ks], (((1,), (1,)), ((), ())),
                                  preferred_element_type=F32)
            att_scr[c * HEADS + h] = jnp.where(keep, att, 0.0).astype(BF16)
            kv_scr[c * HEADS + h] = lax.dot_general(
                kout_scr[rows, ks], v_ref[rows, h * DV:(h + 1) * DV], (((0,), (0,)), ((), ())),
                preferred_element_type=F32)

    state = [s_scr[h] for h in range(HEADS)]

    def scan_chunk(c):
        rows = slice(c * CHUNK, (c + 1) * CHUNK)
        for h in range(HEADS):
            ks = slice(h * DK, (h + 1) * DK)
            vs = slice(h * DV, (h + 1) * DV)
            o = _dot(jnp.concatenate([qin_scr[rows, ks], att_scr[c * HEADS + h]], axis=1),
                     jnp.concatenate([state[h].astype(BF16), v_ref[rows, vs]], axis=0))
            decay = dec_scr[c, ks, :]
            state[h] = state[h] * jnp.concatenate([decay] * (DV // LANE), axis=1) + kv_scr[c * HEADS + h]
            if bwd:
                o = o + of_ref[rows, vs]
                o = o * lax.rsqrt(jnp.mean(o * o, axis=-1, keepdims=True) + EPS) * ng_ref[...]
                o_ref[rows, vs] = (o * r_ref[rows, vs].astype(F32)).astype(o_ref.dtype)
            else:
                o_ref[rows, vs] = o

    order = list(reversed(range(nch))) if bwd else list(range(nch))
    stages = (gate_chunk, scale_chunk, stateless_products, scan_chunk)
    for step in range(nch + len(stages) - 1):
        for lag, stage in enumerate(stages):
            if 0 <= step - lag < nch:
                stage(order[step - lag])
    for h in range(HEADS):
        s_scr[h] = state[h]


def _gla_direction(st, bwd, q, k, v, za, wa, ba, extra=()):
    t = st.t
    nblk = t // GLA_BLK
    row = (lambda i: (nblk - 1 - i, 0)) if bwd else (lambda i: (i, 0))
    const = lambda i: (0, 0)
    in_specs = [
        pl.BlockSpec((GLA_BLK, GLA_DK), row),
        pl.BlockSpec((GLA_BLK, GLA_DK), row),
        pl.BlockSpec((GLA_BLK, GLA_DV), row),
        pl.BlockSpec((GLA_BLK, LANE), row),
        pl.BlockSpec((LANE, GLA_DK), const),
        pl.BlockSpec((1, GLA_DK), const),
    ]
    if bwd:
        in_specs += [pl.BlockSpec((GLA_BLK, GLA_DV), row), pl.BlockSpec((GLA_BLK, GLA_DV), row),
                     pl.BlockSpec((1, DV), const)]
    scaled = pltpu.VMEM((GLA_BLK, GLA_DK), BF16)
    nch = GLA_BLK // CHUNK
    return pl.pallas_call(
        functools.partial(_gla_kernel, bwd=bwd, st=st),
        grid=(nblk,),
        in_specs=in_specs,
        out_specs=pl.BlockSpec((GLA_BLK, GLA_DV), row),
        out_shape=jax.ShapeDtypeStruct((t, GLA_DV), BF16 if bwd else F32),
        scratch_shapes=[pltpu.VMEM((HEADS, DK, DV), F32), scaled, scaled, scaled, scaled,
                        pltpu.VMEM((nch * HEADS, CHUNK, CHUNK), BF16),
                        pltpu.VMEM((nch * HEADS, DK, DV), F32),
                        pltpu.VMEM((nch, GLA_DK, LANE), F32)],
        compiler_params=_cparams(("arbitrary",)),
        name="gla_bwd" if bwd else "gla_fwd",
    )(q, k, v, za, wa, ba, *extra)


def _dft_cos_sin(n):
    idx = np.arange(n, dtype=np.float64)
    ang = 2.0 * np.pi * np.outer(idx, idx) / n
    return np.cos(ang) / np.sqrt(n), np.sin(ang) / np.sqrt(n)


def _fnet_stage1_kernel(x_ref, m_ref, y_ref):
    n2 = x_ref.shape[0]
    x = x_ref[...].reshape(n2 * SUBLANE, FNET_W).astype(BF16)
    y_ref[...] = _dot(m_ref[...], x).reshape(2, n2, SUBLANE, FNET_W)


def _fnet_stage1(zf3, bn, n2, seq0, m1):
    return pl.pallas_call(
        _fnet_stage1_kernel,
        grid=(bn, N1 // SUBLANE),
        in_specs=[
            pl.BlockSpec((n2, SUBLANE, FNET_W), lambda b, j: (seq0 + b, j, 0)),
            pl.BlockSpec(m1.shape, lambda b, j: (0, 0)),
        ],
        out_specs=pl.BlockSpec((2, n2, SUBLANE, FNET_W), lambda b, j: (0, b, j, 0)),
        out_shape=jax.ShapeDtypeStruct((2, bn * n2, N1, FNET_W), F32),
        compiler_params=_cparams(("parallel", "parallel")),
        name="fnet_stage1",
    )(zf3, m1)


def _fnet_stage2_kernel(y_ref, tc_ref, ts_ref, m2_ref, m3_ref, o_ref, z_scr):
    for g in range(SUBLANE):
        rows = slice(g * N1, (g + 1) * N1)
        yr = y_ref[0, rows, :]
        yi = y_ref[1, rows, :]
        tc = tc_ref[g]
        ts = ts_ref[g]
        stack = jnp.concatenate([yr * tc + yi * ts, yi * tc - yr * ts], axis=0)
        z = _dot(m2_ref[...], stack.astype(BF16))
        z_scr[rows, 0:FNET_W] = z[:N1].astype(BF16)
        z_scr[rows, FNET_W:2 * FNET_W] = z[N1:].astype(BF16)
    out = _dot(z_scr[...], m3_ref[...])
    for g in range(SUBLANE):
        o_ref[:, g, :] = out[g * N1:(g + 1) * N1]


def _fnet_stage2(y2, bn, n2, tc, ts, m2, m3):
    rows = SUBLANE * N1
    per_seq = n2 // SUBLANE
    const = lambda b, j: (0, 0)
    return pl.pallas_call(
        _fnet_stage2_kernel,
        grid=(bn, per_seq),
        in_specs=[
            pl.BlockSpec((2, rows, FNET_W), lambda b, j: (0, b * per_seq + j, 0)),
            pl.BlockSpec((SUBLANE, N1, 1), lambda b, j: (j, 0, 0)),
            pl.BlockSpec((SUBLANE, N1, 1), lambda b, j: (j, 0, 0)),
            pl.BlockSpec((2 * N1, 2 * N1), const),
            pl.BlockSpec((2 * FNET_W, FNET_W), const),
        ],
        out_specs=pl.BlockSpec((N1, SUBLANE, FNET_W), lambda b, j: (b, j, 0)),
        out_shape=jax.ShapeDtypeStruct((bn * N1, n2, FNET_W), F32),
        scratch_shapes=[pltpu.VMEM((rows, 2 * FNET_W), BF16)],
        compiler_params=_cparams(("parallel", "parallel")),
        name="fnet_stage2",
    )(y2, tc, ts, m2, m3).reshape(bn * n2 * N1, FNET_W)


def _fnet(st, zf):
    as_bf16 = lambda m: jnp.asarray(np.asarray(m, np.float32).astype(BF16))
    c1, s1 = _dft_cos_sin(N1)
    m2 = as_bf16(np.block([[c1, s1], [-s1, c1]]))
    cc, sc = _dft_cos_sin(FN_GW)
    eye = np.eye(FN_GROUPS)
    m3 = as_bf16(np.concatenate([np.kron(eye, cc), np.kron(eye, sc)], axis=0))
    zf3 = zf.reshape(st.t // N1, N1, FNET_W)
    outs = []
    for bn, s, row0 in ((st.b, st.s, 0), (st.db, st.ds, st.t0)):
        n2 = s // N1
        c2, s2 = _dft_cos_sin(n2)
        m1 = as_bf16(np.kron(np.concatenate([c2, -s2], axis=0), np.eye(SUBLANE)))
        ang = 2.0 * np.pi * np.outer(np.arange(n2), np.arange(N1)) / s
        tc = jnp.asarray(np.cos(ang)[:, :, None], F32)
        ts = jnp.asarray(np.sin(ang)[:, :, None], F32)
        y = _fnet_stage1(zf3, bn, n2, row0 // s, m1)
        outs.append(_fnet_stage2(y.reshape(2, bn * s, FNET_W), bn, n2, tc, ts, m2, m3))
    return tuple(outs)


def _merge_kernel(*refs, st, nx, router):
    x_refs = refs[:nx]
    yf_refs = refs[nx:nx + 2]
    (og_ref, ga_ref, gb_ref, gt_ref, g2_ref, sc_ref, sh_ref,
     wog_ref, wof_ref, wout_ref) = refs[nx + 2:nx + 12]
    rest = refs[nx + 12:]
    if router:
        wrh_ref, wrl_ref, x1_ref, h2_ref, lg_ref = rest
    else:
        x1_ref, h2_ref = rest
    for part in _row_parts(TM):
        ya = _dot(og_ref[part, :], wog_ref[...])
        yb = _dot(_stream_block(st, yf_refs, TM, part).astype(BF16), wof_ref[...])
        m = ga_ref[part, :].astype(F32) * ya + gb_ref[part, :].astype(F32) * yb
        x1 = _stream_block(st, x_refs, TM, part) + gt_ref[0] * _dot(m.astype(BF16), wout_ref[...])
        h2 = _rms_mod(x1, g2_ref[...], sc_ref[0], sh_ref[0])
        if router:
            lg_ref[part, :] = _dot3(*_split_bf16(h2), wrh_ref[...], wrl_ref[...])
        x1_ref[part, :] = x1
        h2_ref[part, :] = h2.astype(BF16)


def _merge(st, og, yfs, ga, gb, xs, gt, g2, sc, sh, wog, wof, wout, wr=None):
    t = st.t
    row = lambda i: (i, 0)
    bat = lambda i: (st.batch_of_block(i, TM), 0, 0)
    const = lambda i: (0, 0)
    in_specs = _stream_specs(st, xs, TM) + _stream_specs(st, yfs, TM) + [
        pl.BlockSpec((TM, GLA_DV), row),
        pl.BlockSpec((TM, D), row), pl.BlockSpec((TM, D), row),
        pl.BlockSpec((1, 1, D), bat), pl.BlockSpec((1, D), const),
        pl.BlockSpec((1, 1, D), bat), pl.BlockSpec((1, 1, D), bat),
        pl.BlockSpec((GLA_DV, D), const), pl.BlockSpec((FNET_W, D), const), pl.BlockSpec((D, D), const),
    ]
    out_specs = [pl.BlockSpec((TM, D), row), pl.BlockSpec((TM, D), row)]
    out_shape = [jax.ShapeDtypeStruct((t, D), F32), jax.ShapeDtypeStruct((t, D), BF16)]
    args = [*xs, *yfs, og, ga, gb, gt, g2, sc, sh, wog, wof, wout]
    if wr is not None:
        in_specs += [pl.BlockSpec((D, LANE), const), pl.BlockSpec((D, LANE), const)]
        out_specs.append(pl.BlockSpec((TM, LANE), row))
        out_shape.append(jax.ShapeDtypeStruct((t, LANE), F32))
        args += list(_split_bf16(wr))
    return pl.pallas_call(
        functools.partial(_merge_kernel, st=st, nx=len(xs), router=wr is not None),
        grid=(t // TM,),
        in_specs=in_specs, out_specs=out_specs, out_shape=out_shape,
        compiler_params=_cparams(("parallel",)),
        name="merge_router" if wr is not None else "merge",
    )(*args)


def _swiglu_into(acc_ref, h, wg_ref, wu_ref, wd_ref, width, chunk, lead=()):
    for c in range(width // chunk):
        cols = slice(c * chunk, (c + 1) * chunk)
        g = _dot(h, wg_ref[lead + (slice(None), cols)])
        u = _dot(h, wu_ref[lead + (slice(None), cols)])
        a = (g * jax.nn.sigmoid(g) * u).astype(BF16)
        part = _dot(a, wd_ref[lead + (cols, slice(None))])
        if c == 0:
            acc_ref[...] = part
        else:
            acc_ref[...] += part


def _ffn_dense_kernel(h_ref, x_ref, gt_ref, wg_ref, wu_ref, wd_ref, o_ref, acc_ref):
    _swiglu_into(acc_ref, h_ref[...], wg_ref, wu_ref, wd_ref, D_FF, FF_CHUNK_DENSE)
    o_ref[...] = x_ref[...] + gt_ref[0] * acc_ref[...]


def _ffn_dense(st, h, x, gt, wg, wu, wd):
    t = st.t
    row = lambda i: (i, 0)
    const = lambda i: (0, 0)
    once = pl.Buffered(1)
    return pl.pallas_call(
        _ffn_dense_kernel,
        grid=(t // TM,),
        in_specs=[
            pl.BlockSpec((TM, D), row), pl.BlockSpec((TM, D), row),
            pl.BlockSpec((1, 1, D), lambda i: (st.batch_of_block(i, TM), 0, 0)),
            pl.BlockSpec((D, D_FF), const, pipeline_mode=once),
            pl.BlockSpec((D, D_FF), const, pipeline_mode=once),
            pl.BlockSpec((D_FF, D), const, pipeline_mode=once),
        ],
        out_specs=pl.BlockSpec((TM, D), row),
        out_shape=jax.ShapeDtypeStruct((t, D), F32),
        scratch_shapes=[pltpu.VMEM((TM, D), F32)],
        compiler_params=_cparams(("parallel",)),
        name="ffn_dense",
    )(h, x, gt, wg, wu, wd)


def _ffn_expert_kernel(te_ref, tv_ref, x_ref, wg_ref, wu_ref, wd_ref, o_ref, acc_ref):
    k = pl.program_id(0)

    @pl.when(tv_ref[k] == 1)
    def _():
        _swiglu_into(acc_ref, x_ref[...], wg_ref, wu_ref, wd_ref, D_EXP, FF_CHUNK_MOE, lead=(0,))
        o_ref[...] = acc_ref[...].astype(o_ref.dtype)

    @pl.when(tv_ref[k] == 0)
    def _():
        o_ref[...] = jnp.zeros_like(o_ref)


def _ffn_experts(tile_e, tile_v, xp, wg, wu, wd):
    ntiles = xp.shape[0] // MOE_TILE
    once = pl.Buffered(1)
    wmap = lambda k, te, tv: (te[k], 0, 0)
    xmap = lambda k, te, tv: (k * tv[k], 0)
    grid_spec = pltpu.PrefetchScalarGridSpec(
        num_scalar_prefetch=2,
        grid=(ntiles,),
        in_specs=[
            pl.BlockSpec((MOE_TILE, D), xmap),
            pl.BlockSpec((1, D, D_EXP), wmap, pipeline_mode=once),
            pl.BlockSpec((1, D, D_EXP), wmap, pipeline_mode=once),
            pl.BlockSpec((1, D_EXP, D), wmap, pipeline_mode=once),
        ],
        out_specs=pl.BlockSpec((MOE_TILE, D), lambda k, te, tv: (k, 0)),
        scratch_shapes=[pltpu.VMEM((MOE_TILE, D), F32)],
    )
    return pl.pallas_call(
        _ffn_expert_kernel,
        grid_spec=grid_spec,
        out_shape=jax.ShapeDtypeStruct(xp.shape, BF16),
        compiler_params=_cparams(("arbitrary",)),
        name="ffn_experts",
    )(tile_e, tile_v, xp, wg, wu, wd)


def _router_kernel(lg_ref, info_ref, cnt_ref):
    lane = lax.broadcasted_iota(jnp.int32, (DISP_BLK, LANE), 1).astype(F32)
    neg = jnp.float32(-jnp.inf)
    lg = jnp.where(lane < N_EXP, lg_ref[...], neg)
    v1 = jnp.max(lg, axis=1, keepdims=True)
    i1 = jnp.min(jnp.where(lg == v1, lane, float(LANE)), axis=1, keepdims=True)
    oh1 = lane == i1
    lg2 = jnp.where(oh1, neg, lg)
    v2 = jnp.max(lg2, axis=1, keepdims=True)
    i2 = jnp.min(jnp.where(lg2 == v2, lane, float(LANE)), axis=1, keepdims=True)
    oh2 = lane == i2
    e = jnp.exp(v2 - v1)
    w1 = 1.0 / (1.0 + e)
    w2 = e / (1.0 + e)
    oh = oh1.astype(F32) + oh2.astype(F32)
    rr = lax.broadcasted_iota(jnp.int32, (DISP_BLK, DISP_BLK), 0)
    cc = lax.broadcasted_iota(jnp.int32, (DISP_BLK, DISP_BLK), 1)
    earlier = (cc < rr).astype(BF16)
    before = _dot(earlier, oh.astype(BF16))
    pos1 = jnp.sum(jnp.where(oh1, before, 0.0), axis=1, keepdims=True)
    pos2 = jnp.sum(jnp.where(oh2, before, 0.0), axis=1, keepdims=True)
    blk_cnt = jnp.sum(oh, axis=0, keepdims=True)
    cnt_ref[0] = jnp.broadcast_to(blk_cnt, (SUBLANE, LANE))
    packed = jnp.where(lane == 0, i1,
             jnp.where(lane == 1, i2,
             jnp.where(lane == 2, w1,
             jnp.where(lane == 3, w2,
             jnp.where(lane == 4, pos1,
             jnp.where(lane == 5, pos2, 0.0))))))
    info_ref[0] = packed.T[0:SUBLANE, :]


def _router(logits):
    t = logits.shape[0]
    nblk = t // DISP_BLK
    return pl.pallas_call(
        _router_kernel,
        grid=(nblk,),
        in_specs=[pl.BlockSpec((DISP_BLK, LANE), lambda i: (i, 0))],
        out_specs=[pl.BlockSpec((1, SUBLANE, DISP_BLK), lambda i: (i, 0, 0)),
                   pl.BlockSpec((1, SUBLANE, LANE), lambda i: (i, 0, 0))],
        out_shape=[jax.ShapeDtypeStruct((nblk, SUBLANE, DISP_BLK), F32),
                   jax.ShapeDtypeStruct((nblk, SUBLANE, LANE), F32)],
        compiler_params=_cparams(("parallel",)),
        name="router",
    )(logits)


def _dispatch_plan(info, cnt):
    nblk = info.shape[0]
    t = nblk * DISP_BLK
    max_rows = 2 * t + nblk * N_EXP * (SEG_ALIGN - 1) + N_EXP * (MOE_TILE - 1)
    ntiles = -(-max_rows // MOE_TILE)
    e12 = info[:, 0:2, :].astype(jnp.int32)
    w12 = info[:, 2:4, :]
    rank12 = info[:, 4:6, :].astype(jnp.int32)
    bc = cnt[:, 0, :N_EXP].astype(jnp.int32)
    pc = (bc + SEG_ALIGN - 1) // SEG_ALIGN * SEG_ALIGN
    loff = jnp.cumsum(pc, axis=1) - pc
    etot = jnp.sum(pc, axis=0)
    epad = (etot + MOE_TILE - 1) // MOE_TILE * MOE_TILE
    eend = jnp.cumsum(epad)
    goff = (eend - epad)[None, :] + jnp.cumsum(pc, axis=0) - pc
    experts = jnp.arange(N_EXP, dtype=jnp.int32)
    lpos = jnp.sum(jnp.where(e12[..., None] == experts, loff[:, None, None, :], 0), axis=-1) + rank12
    tile_row = jnp.arange(ntiles, dtype=jnp.int32) * MOE_TILE
    tile_e = jnp.minimum(jnp.sum((eend[None, :] <= tile_row[:, None]).astype(jnp.int32), axis=1),
                         N_EXP - 1)
    tile_v = (tile_row < eend[-1]).astype(jnp.int32)
    tail_end = eend.at[N_EXP - 1].set(ntiles * MOE_TILE)
    tail_off = eend - epad + etot
    return dict(lpos=lpos, w12=w12, tile_e=tile_e, tile_v=tile_v, ntiles=ntiles,
                segs=(goff.reshape(-1), loff.reshape(-1), (pc // SEG_ALIGN).reshape(-1)),
                tails=(tail_off, (tail_end - tail_off) // SEG_ALIGN))


def _for_each_segment_group(goff_ref, loff_ref, ngrp_ref, blk, fn):
    per_piece = SEG_PIECE // SEG_ALIGN
    for e in range(N_EXP):
        idx = blk * N_EXP + e
        lo, go, ngrp = loff_ref[idx], goff_ref[idx], ngrp_ref[idx]
        nbig = ngrp // per_piece

        def big(i, carry):
            fn(pl.multiple_of(lo + i * SEG_PIECE, SEG_ALIGN), pl.multiple_of(go + i * SEG_PIECE, SEG_ALIGN),
               SEG_PIECE)
            return carry

        def small(i, carry):
            fn(pl.multiple_of(lo + i * SEG_ALIGN, SEG_ALIGN), pl.multiple_of(go + i * SEG_ALIGN, SEG_ALIGN),
               SEG_ALIGN)
            return carry

        lax.fori_loop(0, nbig, big, 0)
        lax.fori_loop(nbig * per_piece, ngrp, small, 0)


def _local_rows(lpos_ref):
    rows = lax.broadcasted_iota(jnp.int32, (LROWS, DISP_BLK), 0)
    return rows == lpos_ref[0, 0:1, :], rows == lpos_ref[0, 1:2, :]


def _gather_kernel(goff_ref, loff_ref, ngrp_ref, toff_ref, tgrp_ref, h_ref, lpos_ref,
                   xp_ref, xl_ref, zero_ref, sem):
    b = pl.program_id(0)
    nblk = pl.num_programs(0)
    slot = b % 2

    def copy(slot_, lrow, srow, nrows):
        return pltpu.make_async_copy(xl_ref.at[slot_, pl.ds(lrow, nrows), :],
                                     xp_ref.at[pl.ds(srow, nrows), :], sem.at[slot_])

    def each(blk, slot_, start):
        _for_each_segment_group(
            goff_ref, loff_ref, ngrp_ref, blk,
            lambda lrow, srow, n: copy(slot_, lrow, srow, n).start() if start else copy(slot_, lrow, srow, n).wait())

    def each_tail(fn):
        for e in range(N_EXP):
            def body(i, carry):
                row = pl.multiple_of(toff_ref[e] + i * SEG_ALIGN, SEG_ALIGN)
                fn(pltpu.make_async_copy(zero_ref, xp_ref.at[pl.ds(row, SEG_ALIGN), :], sem.at[2]))
                return carry

            lax.fori_loop(0, tgrp_ref[e], body, 0)

    @pl.when(b == 0)
    def _():
        zero_ref[...] = jnp.zeros_like(zero_ref)
        each_tail(lambda cp: cp.start())

    @pl.when(b >= 2)
    def _():
        each(b - 2, slot, start=False)

    hit1, hit2 = _local_rows(lpos_ref)
    sel = jnp.logical_or(hit1, hit2).astype(BF16)
    xl_ref[slot] = _dot(sel, h_ref[...]).astype(BF16)
    each(b, slot, start=True)

    @pl.when(b == nblk - 1)
    def _():
        each(b, slot, start=False)
        each_tail(lambda cp: cp.wait())

        @pl.when(b >= 1)
        def _():
            each(b - 1, 1 - slot, start=False)


def _gather_rows(plan, h):
    nrows = plan["ntiles"] * MOE_TILE
    nblk = h.shape[0] // DISP_BLK
    grid_spec = pltpu.PrefetchScalarGridSpec(
        num_scalar_prefetch=5,
        grid=(nblk,),
        in_specs=[
            pl.BlockSpec((DISP_BLK, D), lambda b, *_: (b, 0)),
            pl.BlockSpec((1, 2, DISP_BLK), lambda b, *_: (b, 0, 0)),
        ],
        out_specs=pl.BlockSpec(memory_space=pl.ANY),
        scratch_shapes=[pltpu.VMEM((2, LROWS, D), BF16), pltpu.VMEM((SEG_ALIGN, D), BF16),
                        pltpu.SemaphoreType.DMA((3,))],
    )
    return pl.pallas_call(
        _gather_kernel,
        grid_spec=grid_spec,
        out_shape=jax.ShapeDtypeStruct((nrows, D), BF16),
        compiler_params=_cparams(("arbitrary",)),
        name="moe_gather",
    )(*plan["segs"], *plan["tails"], h, plan["lpos"])


def _combine_kernel(goff_ref, loff_ref, ngrp_ref, yp_ref, lpos_ref, w_ref, x_ref, gt_ref, gf_ref,
                    op_ref, os_ref, yl_ref, sem, *, nb0):
    b = pl.program_id(0)
    nblk = pl.num_programs(0)
    slot = b % 2

    def copy(slot_, lrow, srow, nrows):
        return pltpu.make_async_copy(yp_ref.at[pl.ds(srow, nrows), :],
                                     yl_ref.at[slot_, pl.ds(lrow, nrows), :], sem.at[slot_])

    def each(blk, slot_, start):
        _for_each_segment_group(
            goff_ref, loff_ref, ngrp_ref, blk,
            lambda lrow, srow, n: copy(slot_, lrow, srow, n).start() if start else copy(slot_, lrow, srow, n).wait())

    @pl.when(b == 0)
    def _():
        yl_ref[...] = jnp.zeros_like(yl_ref)
        each(0, 0, start=True)

    @pl.when(b + 1 < nblk)
    def _():
        each(b + 1, 1 - slot, start=True)

    each(b, slot, start=False)

    hit1, hit2 = _local_rows(lpos_ref)
    wsel = jnp.where(hit1, w_ref[0, 0:1, :], 0.0) + jnp.where(hit2, w_ref[0, 1:2, :], 0.0)
    wh, wl = _split_bf16(wsel)
    tn = (((0,), (0,)), ((), ()))
    yl = yl_ref[slot]
    y = (lax.dot_general(wh, yl, tn, preferred_element_type=F32)
         + lax.dot_general(wl, yl, tn, preferred_element_type=F32))
    x2 = x_ref[...] + gt_ref[0] * y
    ms = jnp.mean(x2 * x2, axis=-1, keepdims=True)
    out = x2 * lax.rsqrt(ms + EPS) * gf_ref[...]

    @pl.when(b < nb0)
    def _():
        op_ref[...] = out

    @pl.when(b >= nb0)
    def _():
        os_ref[...] = out


def _combine(st, plan, yp, x, gt, gf):
    nblk = st.t // DISP_BLK
    nb0 = st.t0 // DISP_BLK
    blk3 = lambda b, *_: (b, 0, 0)
    grid_spec = pltpu.PrefetchScalarGridSpec(
        num_scalar_prefetch=3,
        grid=(nblk,),
        in_specs=[
            pl.BlockSpec(memory_space=pl.ANY),
            pl.BlockSpec((1, 2, DISP_BLK), blk3),
            pl.BlockSpec((1, 2, DISP_BLK), blk3),
            pl.BlockSpec((DISP_BLK, D), lambda b, *_: (b, 0)),
            pl.BlockSpec((1, 1, D), lambda b, *_: (st.batch_of_block(b, DISP_BLK), 0, 0)),
            pl.BlockSpec((1, D), lambda b, *_: (0, 0)),
        ],
        out_specs=[pl.BlockSpec((DISP_BLK, D), lambda b, *_: (jnp.minimum(b, nb0 - 1), 0)),
                   pl.BlockSpec((DISP_BLK, D), lambda b, *_: (jnp.maximum(b - nb0, 0), 0))],
        scratch_shapes=[pltpu.VMEM((2, LROWS, D), BF16), pltpu.SemaphoreType.DMA((2,))],
    )
    return pl.pallas_call(
        functools.partial(_combine_kernel, nb0=nb0),
        grid_spec=grid_spec,
        out_shape=[jax.ShapeDtypeStruct((st.t0, D), F32), jax.ShapeDtypeStruct((st.t - st.t0, D), F32)],
        compiler_params=_cparams(("arbitrary",)),
        name="moe_combine_norm",
    )(*plan["segs"], yp, plan["lpos"], plan["w12"], x, gt, gf)


def _pack_w_in(w):
    o = np.cumsum([0, GLA_DK, GLA_DK, GLA_DV, GLA_DV, RANK, RANK, FNET_W, D, D]).tolist()
    gates = jnp.pad(jnp.tile(w[:, o[4]:o[6]], (1, 3)), ((0, 0), (0, LANE - 6 * RANK)))
    return jnp.concatenate([w[:, o[0]:o[4]], gates, w[:, o[6]:o[9]]], axis=1).astype(BF16)


def _gate_weights(w_a2, offset):
    hi, lo = _split_bf16(w_a2)
    out = jnp.zeros((LANE, GLA_DK), BF16)
    for copy, part in enumerate((hi, hi, lo)):
        out = lax.dynamic_update_slice(out, part, (copy * 2 * RANK + offset, 0))
    return out


def kernel(x_prompt, x_sample, c_prompt, c_sample, norm1_g, norm2_g, w_ada, b_ada, w_in, w_af2, b_af, w_ab2, b_ab, gla_norm_g, w_o_gla, w_o_fnet, w_out, w_ff_gate, w_ff_up, w_ff_down, w_router, w_e_gate, w_e_up, w_e_down, final_norm_g):
    b, s, _ = x_prompt.shape
    db, ds, _ = x_sample.shape
    depth = w_in.shape[0]
    assert depth == 2 and s % max(GLA_BLK, TM, DISP_BLK, SUBLANE * N1) == 0
    assert ds % max(GLA_BLK, TM, DISP_BLK, SUBLANE * N1) == 0 and (b * s) % ds == 0
    st = _Stream(b, s, db, ds)

    xs = (x_prompt.reshape(b * s, D), x_sample.reshape(db * ds, D))
    c = jnp.concatenate([c_prompt, c_sample], axis=0)
    nb_pad = -(-st.nbatch // SUBLANE) * SUBLANE
    c = jnp.pad(c, ((0, nb_pad - st.nbatch), (0, 0)))
    mod = _modulation(c, w_ada, b_ada)

    def mod_piece(l, j):
        return mod[l, :, j * D:(j + 1) * D].reshape(nb_pad, 1, D)

    for l in range(depth):
        sh1, sc1, gt1, sh2, sc2, gt2 = [mod_piece(l, j) for j in range(N_MOD)]
        q, k, v, r, za, zf, ga, gb = _inproj(st, xs, norm1_g[l].reshape(1, D), sc1, sh1, _pack_w_in(w_in[l]))

        wa_f = _gate_weights(w_af2[l], 0)
        wa_b = _gate_weights(w_ab2[l], RANK)
        o_f = _gla_direction(st, False, q, k, v, za, wa_f, b_af[l].reshape(1, GLA_DK))
        o_gla = _gla_direction(st, True, q, k, v, za, wa_b, b_ab[l].reshape(1, GLA_DK),
                               extra=(o_f, r, gla_norm_g[l].reshape(1, DV)))

        yf = _fnet(st, zf)

        wog, wof, wout = w_o_gla[l].astype(BF16), w_o_fnet[l].astype(BF16), w_out[l].astype(BF16)
        g2 = norm2_g[l].reshape(1, D)
        if l % 2 == 0:
            x1, h2 = _merge(st, o_gla, yf, ga, gb, xs, gt1, g2, sc2, sh2, wog, wof, wout)
            xs = (_ffn_dense(st, h2, x1, gt2, w_ff_gate[l // 2].astype(BF16),
                             w_ff_up[l // 2].astype(BF16), w_ff_down[l // 2].astype(BF16)),)
        else:
            wr = jnp.pad(w_router[l // 2], ((0, 0), (0, LANE - N_EXP)))
            x1, h2, logits = _merge(st, o_gla, yf, ga, gb, xs, gt1, g2, sc2, sh2, wog, wof, wout, wr)
            info, cnt = _router(logits)
            plan = _dispatch_plan(info, cnt)
            xp = _gather_rows(plan, h2)
            yp = _ffn_experts(plan["tile_e"], plan["tile_v"], xp, w_e_gate[l // 2].astype(BF16),
                              w_e_up[l // 2].astype(BF16), w_e_down[l // 2].astype(BF16))
            y_prompt, y_sample = _combine(st, plan, yp, x1, gt2, final_norm_g.reshape(1, D))

    return y_prompt.reshape(b, s, D), y_sample.reshape(db, ds, D)
```

```python
import functools

import numpy as np
import jax
import jax.numpy as jnp
from jax import lax
from jax.experimental import pallas as pl
from jax.experimental.pallas import tpu as pltpu

D = 1024
HEADS = 4
DK = 128
DV = 256
GLA_DK = HEADS * DK
GLA_DV = HEADS * DV
RANK = 16
GATE_NORM = 16.0
CHUNK = 64
FN_GROUPS = 4
FN_GW = 128
FNET_W = FN_GROUPS * FN_GW
D_FF = 2816
N_EXP = 8
D_EXP = 3584
EPS = 1e-6
N_MOD = 6

LANE = 128
SUBLANE = 8
VMEM_LIMIT = 56 * 1024 * 1024
TM = 512
ROW_PART = 256
GLA_BLK = 1024
MOE_TILE = 512
DISP_BLK = 512
SEG_ALIGN = 16
SEG_PIECE = 128
LROWS = 2 * DISP_BLK + N_EXP * SEG_ALIGN
FF_CHUNK_DENSE = 256
FF_CHUNK_MOE = 256
N1 = 128

F32 = jnp.float32
BF16 = jnp.bfloat16
_HI = lax.Precision.HIGHEST


def _cparams(sem):
    return pltpu.CompilerParams(dimension_semantics=sem, vmem_limit_bytes=VMEM_LIMIT)


def _dot(a, b):
    return jnp.dot(a, b, preferred_element_type=F32)


def _dot_hi(a, b):
    return jnp.dot(a, b, precision=_HI, preferred_element_type=F32)


def _split_bf16(x):
    hi = x.astype(BF16)
    lo = (x - hi.astype(F32)).astype(BF16)
    return hi, lo


def _dot3(a_hi, a_lo, b_hi, b_lo):
    return _dot(a_hi, b_hi) + (_dot(a_hi, b_lo) + _dot(a_lo, b_hi))


class _Stream:
    def __init__(self, b, s, db, ds):
        self.b, self.s, self.db, self.ds = b, s, db, ds
        self.t0 = b * s
        self.t = b * s + db * ds
        self.nbatch = b + db

    def batch_of_block(self, i, rows):
        nb0, p0, p1 = self.t0 // rows, self.s // rows, self.ds // rows
        return jnp.where(i < nb0, i // p0, self.b + (i - nb0) // p1)

    def block_in_seq(self, i, rows):
        nb0, p0, p1 = self.t0 // rows, self.s // rows, self.ds // rows
        return jnp.where(i < nb0, i % p0, (i - nb0) % p1), jnp.where(i < nb0, p0, p1)


def _mod_kernel(c_ref, w_ref, b_ref, o_ref):
    c = c_ref[...]
    cs = c * jax.nn.sigmoid(c)
    o_ref[0] = _dot_hi(cs, w_ref[0]) + b_ref[0]


def _modulation(c_pad, w_ada, b_ada):
    depth = w_ada.shape[0]
    nb = c_pad.shape[0]
    return pl.pallas_call(
        _mod_kernel,
        grid=(depth, N_MOD),
        in_specs=[
            pl.BlockSpec((nb, D), lambda l, j: (0, 0)),
            pl.BlockSpec((1, D, D), lambda l, j: (l, 0, j)),
            pl.BlockSpec((1, 1, D), lambda l, j: (l, 0, j)),
        ],
        out_specs=pl.BlockSpec((1, nb, D), lambda l, j: (l, 0, j)),
        out_shape=jax.ShapeDtypeStruct((depth, nb, N_MOD * D), F32),
        compiler_params=_cparams(("arbitrary", "arbitrary")),
        name="adaln_mod",
    )(c_pad, w_ada, b_ada.reshape(depth, 1, N_MOD * D))


def _rms_mod(x, g, sc, sh):
    ms = jnp.mean(x * x, axis=-1, keepdims=True)
    return (x * lax.rsqrt(ms + EPS) * g) * (1.0 + sc) + sh


_C_Q, _C_K, _C_V, _C_R, _C_A, _C_F, _C_GA, _C_GB, _C_END = np.cumsum(
    [0, GLA_DK, GLA_DK, GLA_DV, GLA_DV, LANE, FNET_W, D, D]).tolist()


def _stream_specs(st, xs, rows):
    width = xs[0].shape[1]
    if len(xs) == 1:
        return [pl.BlockSpec((rows, width), lambda i, *_: (i, 0))]
    nb0 = st.t0 // rows
    return [pl.BlockSpec((rows, width), lambda i, *_: (jnp.minimum(i, nb0 - 1), 0)),
            pl.BlockSpec((rows, width), lambda i, *_: (jnp.maximum(i - nb0, 0), 0))]


def _stream_block(st, x_refs, rows, part=slice(None)):
    if len(x_refs) == 1:
        return x_refs[0][part, :]
    return jnp.where(pl.program_id(0) < st.t0 // rows, x_refs[0][part, :], x_refs[1][part, :])


def _row_parts(rows):
    return [slice(p * ROW_PART, (p + 1) * ROW_PART) for p in range(rows // ROW_PART)]


def _inproj_kernel(*refs, st, nx):
    x_refs = refs[:nx]
    (g_ref, sc_ref, sh_ref, w_ref,
     q_ref, k_ref, v_ref, r_ref, a_ref, f_ref, ga_ref, gb_ref) = refs[nx:]
    for part in _row_parts(TM):
        hb = _rms_mod(_stream_block(st, x_refs, TM, part), g_ref[...], sc_ref[0], sh_ref[0]).astype(BF16)

        def proj(lo, hi):
            return _dot(hb, w_ref[:, lo:hi])

        q_ref[part, :] = (proj(_C_Q, _C_K) * (DK ** -0.5)).astype(BF16)
        k_ref[part, :] = proj(_C_K, _C_V).astype(BF16)
        v_ref[part, :] = proj(_C_V, _C_R).astype(BF16)
        zr = proj(_C_R, _C_A)
        r_ref[part, :] = (zr * jax.nn.sigmoid(zr)).astype(BF16)
        a_ref[part, :] = proj(_C_A, _C_F)
        f_ref[part, :] = proj(_C_F, _C_GA)
        ga_ref[part, :] = jax.nn.sigmoid(proj(_C_GA, _C_GB)).astype(BF16)
        gb_ref[part, :] = jax.nn.sigmoid(proj(_C_GB, _C_END)).astype(BF16)


def _inproj(st, xs, g, sc, sh, w):
    t = st.t
    row = lambda i: (i, 0)
    bat = lambda i: (st.batch_of_block(i, TM), 0, 0)
    widths = [(GLA_DK, BF16), (GLA_DK, BF16), (GLA_DV, BF16), (GLA_DV, BF16),
              (LANE, F32), (FNET_W, F32), (D, BF16), (D, BF16)]
    return pl.pallas_call(
        functools.partial(_inproj_kernel, st=st, nx=len(xs)),
        grid=(t // TM,),
        in_specs=_stream_specs(st, xs, TM) + [
            pl.BlockSpec((1, D), lambda i: (0, 0)),
            pl.BlockSpec((1, 1, D), bat),
            pl.BlockSpec((1, 1, D), bat),
            pl.BlockSpec((D, _C_END), lambda i: (0, 0)),
        ],
        out_specs=[pl.BlockSpec((TM, w_), row) for w_, _ in widths],
        out_shape=[jax.ShapeDtypeStruct((t, w_), dt) for w_, dt in widths],
        compiler_params=_cparams(("parallel",)),
        name="inproj",
    )(*xs, g, sc, sh, w)


def _log_sigmoid(x):
    return jnp.minimum(x, 0.0) - jnp.log(1.0 + jnp.exp(-jnp.abs(x)))


def _gla_kernel(q_ref, k_ref, v_ref, za_ref, wa_ref, ba_ref,
                o_ref, s_scr, qi_scr, ki_scr, qin_scr, kout_scr, att_scr, kv_scr, dec_scr, *, bwd, st):
    i = pl.program_id(0)
    nblk = pl.num_programs(0)
    blk = (nblk - 1 - i) if bwd else i
    local, per = st.block_in_seq(blk, GLA_BLK)
    at_boundary = (local == per - 1) if bwd else (local == 0)

    @pl.when(at_boundary)
    def _():
        s_scr[...] = jnp.zeros_like(s_scr)

    nch = GLA_BLK // CHUNK
    rr = lax.broadcasted_iota(jnp.int32, (CHUNK, CHUNK), 0)
    cc = lax.broadcasted_iota(jnp.int32, (CHUNK, CHUNK), 1)
    keep = (rr <= cc) if bwd else (rr >= cc)
    tri = keep.astype(BF16)
    last = 0 if bwd else CHUNK - 1
    mid = CHUNK // 2 if bwd else CHUNK // 2 - 1

    lane = lax.broadcasted_iota(jnp.int32, (CHUNK, LANE), 1)
    low_part_lanes = jnp.logical_and(lane >= 2 * RANK, lane < 4 * RANK)

    log_decay = {}

    def gate_chunk(c):
        rows = slice(c * CHUNK, (c + 1) * CHUNK)
        zh, zl = _split_bf16(za_ref[rows, :])
        pre = _dot(jnp.where(low_part_lanes, zl, zh), wa_ref[...]) + ba_ref[...]
        g = _log_sigmoid(pre) * (1.0 / GATE_NORM)
        log_decay[c] = _split_bf16(g)

    def scale_chunk(c):
        rows = slice(c * CHUNK, (c + 1) * CHUNK)
        g0, g1 = log_decay.pop(c)
        b = _dot(tri, g0) + _dot(tri, g1)
        b_last = b[last:last + 1, :]
        b_ref = b[mid:mid + 1, :]
        dec_scr[c] = jnp.broadcast_to(jnp.exp(b_last), (LANE, GLA_DK)).T
        q = q_ref[rows, :].astype(F32)
        k = k_ref[rows, :].astype(F32)
        qi_scr[rows, :] = (q * jnp.exp(b - b_ref)).astype(BF16)
        ki_scr[rows, :] = (k * jnp.exp(b_ref - b)).astype(BF16)
        qin_scr[rows, :] = (q * jnp.exp(b)).astype(BF16)
        kout_scr[rows, :] = (k * jnp.exp(b_last - b)).astype(BF16)

    def stateless_products(c):
        rows = slice(c * CHUNK, (c + 1) * CHUNK)
        for h in range(HEADS):
            ks = slice(h * DK, (h + 1) * DK)
            att = lax.dot_general(qi_scr[rows, ks], ki_scr[rows, ks], (((1,), (1,)), ((), ())),
                                  preferred_element_type=F32)
            att_scr[c * HEADS + h] = jnp.where(keep, att, 0.0).astype(BF16)
            kv_scr[c * HEADS + h] = lax.dot_general(
                kout_scr[rows, ks], v_ref[rows, h * DV:(h + 1) * DV], (((0,), (0,)), ((), ())),
                preferred_element_type=F32)

    state = [s_scr[h] for h in range(HEADS)]

    def scan_chunk(c):
        rows = slice(c * CHUNK, (c + 1) * CHUNK)
        for h in range(HEADS):
            ks = slice(h * DK, (h + 1) * DK)
            vs = slice(h * DV, (h + 1) * DV)
            o = _dot(jnp.concatenate([qin_scr[rows, ks], att_scr[c * HEADS + h]], axis=1),
                     jnp.concatenate([state[h].astype(BF16), v_ref[rows, vs]], axis=0))
            decay = dec_scr[c, ks, :]
            state[h] = state[h] * jnp.concatenate([decay] * (DV // LANE), axis=1) + kv_scr[c * HEADS + h]
            o_ref[rows, vs] = o.astype(o_ref.dtype)

    order = list(reversed(range(nch))) if bwd else list(range(nch))
    stages = (gate_chunk, scale_chunk, stateless_products, scan_chunk)
    for step in range(nch + len(stages) - 1):
        for lag, stage in enumerate(stages):
            if 0 <= step - lag < nch:
                stage(order[step - lag])
    for h in range(HEADS):
        s_scr[h] = state[h]


def _gla_direction(st, bwd, q, k, v, za, wa, ba):
    t = st.t
    nblk = t // GLA_BLK
    row = (lambda i: (nblk - 1 - i, 0)) if bwd else (lambda i: (i, 0))
    const = lambda i: (0, 0)
    in_specs = [
        pl.BlockSpec((GLA_BLK, GLA_DK), row),
        pl.BlockSpec((GLA_BLK, GLA_DK), row),
        pl.BlockSpec((GLA_BLK, GLA_DV), row),
        pl.BlockSpec((GLA_BLK, LANE), row),
        pl.BlockSpec((LANE, GLA_DK), const),
        pl.BlockSpec((1, GLA_DK), const),
    ]
    scaled = pltpu.VMEM((GLA_BLK, GLA_DK), BF16)
    nch = GLA_BLK // CHUNK
    return pl.pallas_call(
        functools.partial(_gla_kernel, bwd=bwd, st=st),
        grid=(nblk,),
        in_specs=in_specs,
        out_specs=pl.BlockSpec((GLA_BLK, GLA_DV), row),
        out_shape=jax.ShapeDtypeStruct((t, GLA_DV), BF16),
        scratch_shapes=[pltpu.VMEM((HEADS, DK, DV), F32), scaled, scaled, scaled, scaled,
                        pltpu.VMEM((nch * HEADS, CHUNK, CHUNK), BF16),
                        pltpu.VMEM((nch * HEADS, DK, DV), F32),
                        pltpu.VMEM((nch, GLA_DK, LANE), F32)],
        compiler_params=_cparams(("arbitrary",)),
        name="gla_bwd" if bwd else "gla_fwd",
    )(q, k, v, za, wa, ba)


def _dft_cos_sin(n):
    idx = np.arange(n, dtype=np.float64)
    ang = 2.0 * np.pi * np.outer(idx, idx) / n
    return np.cos(ang) / np.sqrt(n), np.sin(ang) / np.sqrt(n)


def _fnet_stage1_kernel(x_ref, m_ref, y_ref):
    n2 = x_ref.shape[0]
    x = x_ref[...].reshape(n2 * SUBLANE, FNET_W).astype(BF16)
    y_ref[...] = _dot(m_ref[...], x).reshape(2, n2, SUBLANE, FNET_W)


def _fnet_stage1(zf3, bn, n2, seq0, m1):
    return pl.pallas_call(
        _fnet_stage1_kernel,
        grid=(bn, N1 // SUBLANE),
        in_specs=[
            pl.BlockSpec((n2, SUBLANE, FNET_W), lambda b, j: (seq0 + b, j, 0)),
            pl.BlockSpec(m1.shape, lambda b, j: (0, 0)),
        ],
        out_specs=pl.BlockSpec((2, n2, SUBLANE, FNET_W), lambda b, j: (0, b, j, 0)),
        out_shape=jax.ShapeDtypeStruct((2, bn * n2, N1, FNET_W), F32),
        compiler_params=_cparams(("parallel", "parallel")),
        name="fnet_stage1",
    )(zf3, m1)


def _fnet_stage2_kernel(y_ref, tc_ref, ts_ref, m2_ref, m3_ref, o_ref, z_scr):
    for g in range(SUBLANE):
        rows = slice(g * N1, (g + 1) * N1)
        yr = y_ref[0, rows, :]
        yi = y_ref[1, rows, :]
        tc = tc_ref[g]
        ts = ts_ref[g]
        stack = jnp.concatenate([yr * tc + yi * ts, yi * tc - yr * ts], axis=0)
        z = _dot(m2_ref[...], stack.astype(BF16))
        z_scr[rows, 0:FNET_W] = z[:N1].astype(BF16)
        z_scr[rows, FNET_W:2 * FNET_W] = z[N1:].astype(BF16)
    out = _dot(z_scr[...], m3_ref[...])
    for g in range(SUBLANE):
        o_ref[:, g, :] = out[g * N1:(g + 1) * N1]


def _fnet_stage2(y2, bn, n2, tc, ts, m2, m3):
    rows = SUBLANE * N1
    per_seq = n2 // SUBLANE
    const = lambda b, j: (0, 0)
    return pl.pallas_call(
        _fnet_stage2_kernel,
        grid=(bn, per_seq),
        in_specs=[
            pl.BlockSpec((2, rows, FNET_W), lambda b, j: (0, b * per_seq + j, 0)),
            pl.BlockSpec((SUBLANE, N1, 1), lambda b, j: (j, 0, 0)),
            pl.BlockSpec((SUBLANE, N1, 1), lambda b, j: (j, 0, 0)),
            pl.BlockSpec((2 * N1, 2 * N1), const),
            pl.BlockSpec((2 * FNET_W, FNET_W), const),
        ],
        out_specs=pl.BlockSpec((N1, SUBLANE, FNET_W), lambda b, j: (b, j, 0)),
        out_shape=jax.ShapeDtypeStruct((bn * N1, n2, FNET_W), F32),
        scratch_shapes=[pltpu.VMEM((rows, 2 * FNET_W), BF16)],
        compiler_params=_cparams(("parallel", "parallel")),
        name="fnet_stage2",
    )(y2, tc, ts, m2, m3).reshape(bn * n2 * N1, FNET_W)


def _fnet(st, zf):
    as_bf16 = lambda m: jnp.asarray(np.asarray(m, np.float32).astype(BF16))
    c1, s1 = _dft_cos_sin(N1)
    m2 = as_bf16(np.block([[c1, s1], [-s1, c1]]))
    cc, sc = _dft_cos_sin(FN_GW)
    eye = np.eye(FN_GROUPS)
    m3 = as_bf16(np.concatenate([np.kron(eye, cc), np.kron(eye, sc)], axis=0))
    zf3 = zf.reshape(st.t // N1, N1, FNET_W)
    outs = []
    for bn, s, row0 in ((st.b, st.s, 0), (st.db, st.ds, st.t0)):
        n2 = s // N1
        c2, s2 = _dft_cos_sin(n2)
        m1 = as_bf16(np.kron(np.concatenate([c2, -s2], axis=0), np.eye(SUBLANE)))
        ang = 2.0 * np.pi * np.outer(np.arange(n2), np.arange(N1)) / s
        tc = jnp.asarray(np.cos(ang)[:, :, None], F32)
        ts = jnp.asarray(np.sin(ang)[:, :, None], F32)
        y = _fnet_stage1(zf3, bn, n2, row0 // s, m1)
        outs.append(_fnet_stage2(y.reshape(2, bn * s, FNET_W), bn, n2, tc, ts, m2, m3))
    return tuple(outs)


def _merge_kernel(*refs, st, nx, router):
    x_refs = refs[:nx]
    yf_refs = refs[nx:nx + 2]
    (of_ref, ob_ref, r_ref, ng_ref, ga_ref, gb_ref, gt_ref, g2_ref, sc_ref, sh_ref,
     wog_ref, wof_ref, wout_ref) = refs[nx + 2:nx + 15]
    rest = refs[nx + 15:]
    if router:
        wrh_ref, wrl_ref, x1_ref, h2_ref, lg_ref = rest
    else:
        x1_ref, h2_ref = rest
    for part in _row_parts(TM):
        o = of_ref[part, :].astype(F32) + ob_ref[part, :].astype(F32)
        heads = []
        for h in range(HEADS):
            oh = o[:, h * DV:(h + 1) * DV]
            heads.append(oh * lax.rsqrt(jnp.mean(oh * oh, axis=-1, keepdims=True) + EPS) * ng_ref[...])
        og = (jnp.concatenate(heads, axis=1) * r_ref[part, :].astype(F32)).astype(BF16)
        ya = _dot(og, wog_ref[...])
        yb = _dot(_stream_block(st, yf_refs, TM, part).astype(BF16), wof_ref[...])
        m = ga_ref[part, :].astype(F32) * ya + gb_ref[part, :].astype(F32) * yb
        x1 = _stream_block(st, x_refs, TM, part) + gt_ref[0] * _dot(m.astype(BF16), wout_ref[...])
        h2 = _rms_mod(x1, g2_ref[...], sc_ref[0], sh_ref[0])
        if router:
            lg_ref[part, :] = _dot3(*_split_bf16(h2), wrh_ref[...], wrl_ref[...])
        x1_ref[part, :] = x1
        h2_ref[part, :] = h2.astype(BF16)


def _merge(st, og, yfs, ga, gb, xs, gt, g2, sc, sh, wog, wof, wout, wr=None):
    t = st.t
    row = lambda i: (i, 0)
    bat = lambda i: (st.batch_of_block(i, TM), 0, 0)
    const = lambda i: (0, 0)
    in_specs = _stream_specs(st, xs, TM) + _stream_specs(st, yfs, TM) + [
        pl.BlockSpec((TM, GLA_DV), row), pl.BlockSpec((TM, GLA_DV), row), pl.BlockSpec((TM, GLA_DV), row),
        pl.BlockSpec((1, DV), const),
        pl.BlockSpec((TM, D), row), pl.BlockSpec((TM, D), row),
        pl.BlockSpec((1, 1, D), bat), pl.BlockSpec((1, D), const),
        pl.BlockSpec((1, 1, D), bat), pl.BlockSpec((1, 1, D), bat),
        pl.BlockSpec((GLA_DV, D), const), pl.BlockSpec((FNET_W, D), const), pl.BlockSpec((D, D), const),
    ]
    out_specs = [pl.BlockSpec((TM, D), row), pl.BlockSpec((TM, D), row)]
    out_shape = [jax.ShapeDtypeStruct((t, D), F32), jax.ShapeDtypeStruct((t, D), BF16)]
    args = [*xs, *yfs, *og, ga, gb, gt, g2, sc, sh, wog, wof, wout]
    if wr is not None:
        in_specs += [pl.BlockSpec((D, LANE), const), pl.BlockSpec((D, LANE), const)]
        out_specs.append(pl.BlockSpec((TM, LANE), row))
        out_shape.append(jax.ShapeDtypeStruct((t, LANE), F32))
        args += list(_split_bf16(wr))
    return pl.pallas_call(
        functools.partial(_merge_kernel, st=st, nx=len(xs), router=wr is not None),
        grid=(t // TM,),
        in_specs=in_specs, out_specs=out_specs, out_shape=out_shape,
        compiler_params=_cparams(("parallel",)),
        name="merge_router" if wr is not None else "merge",
    )(*args)


def _swiglu_into(acc_ref, h, wg_ref, wu_ref, wd_ref, width, chunk, lead=()):
    for c in range(width // chunk):
        cols = slice(c * chunk, (c + 1) * chunk)
        g = _dot(h, wg_ref[lead + (slice(None), cols)])
        u = _dot(h, wu_ref[lead + (slice(None), cols)])
        a = (g * jax.nn.sigmoid(g) * u).astype(BF16)
        part = _dot(a, wd_ref[lead + (cols, slice(None))])
        if c == 0:
            acc_ref[...] = part
        else:
            acc_ref[...] += part


def _ffn_dense_kernel(h_ref, x_ref, gt_ref, wg_ref, wu_ref, wd_ref, o_ref, acc_ref):
    _swiglu_into(acc_ref, h_ref[...], wg_ref, wu_ref, wd_ref, D_FF, FF_CHUNK_DENSE)
    o_ref[...] = x_ref[...] + gt_ref[0] * acc_ref[...]


def _ffn_dense(st, h, x, gt, wg, wu, wd):
    t = st.t
    row = lambda i: (i, 0)
    const = lambda i: (0, 0)
    once = pl.Buffered(1)
    return pl.pallas_call(
        _ffn_dense_kernel,
        grid=(t // TM,),
        in_specs=[
            pl.BlockSpec((TM, D), row), pl.BlockSpec((TM, D), row),
            pl.BlockSpec((1, 1, D), lambda i: (st.batch_of_block(i, TM), 0, 0)),
            pl.BlockSpec((D, D_FF), const, pipeline_mode=once),
            pl.BlockSpec((D, D_FF), const, pipeline_mode=once),
            pl.BlockSpec((D_FF, D), const, pipeline_mode=once),
        ],
        out_specs=pl.BlockSpec((TM, D), row),
        out_shape=jax.ShapeDtypeStruct((t, D), F32),
        scratch_shapes=[pltpu.VMEM((TM, D), F32)],
        compiler_params=_cparams(("parallel",)),
        name="ffn_dense",
    )(h, x, gt, wg, wu, wd)


def _ffn_expert_kernel(te_ref, tv_ref, x_ref, wg_ref, wu_ref, wd_ref, o_ref, acc_ref):
    k = pl.program_id(0)

    @pl.when(tv_ref[k] == 1)
    def _():
        _swiglu_into(acc_ref, x_ref[...], wg_ref, wu_ref, wd_ref, D_EXP, FF_CHUNK_MOE, lead=(0,))
        o_ref[...] = acc_ref[...].astype(o_ref.dtype)

    @pl.when(tv_ref[k] == 0)
    def _():
        o_ref[...] = jnp.zeros_like(o_ref)


def _ffn_experts(tile_e, tile_v, xp, wg, wu, wd):
    ntiles = xp.shape[0] // MOE_TILE
    once = pl.Buffered(1)
    wmap = lambda k, te, tv: (te[k], 0, 0)
    xmap = lambda k, te, tv: (k * tv[k], 0)
    grid_spec = pltpu.PrefetchScalarGridSpec(
        num_scalar_prefetch=2,
        grid=(ntiles,),
        in_specs=[
            pl.BlockSpec((MOE_TILE, D), xmap),
            pl.BlockSpec((1, D, D_EXP), wmap, pipeline_mode=once),
            pl.BlockSpec((1, D, D_EXP), wmap, pipeline_mode=once),
            pl.BlockSpec((1, D_EXP, D), wmap, pipeline_mode=once),
        ],
        out_specs=pl.BlockSpec((MOE_TILE, D), lambda k, te, tv: (k, 0)),
        scratch_shapes=[pltpu.VMEM((MOE_TILE, D), F32)],
    )
    return pl.pallas_call(
        _ffn_expert_kernel,
        grid_spec=grid_spec,
        out_shape=jax.ShapeDtypeStruct(xp.shape, BF16),
        compiler_params=_cparams(("arbitrary",)),
        name="ffn_experts",
    )(tile_e, tile_v, xp, wg, wu, wd)


def _router_kernel(lg_ref, info_ref, cnt_ref):
    lane = lax.broadcasted_iota(jnp.int32, (DISP_BLK, LANE), 1).astype(F32)
    neg = jnp.float32(-jnp.inf)
    lg = jnp.where(lane < N_EXP, lg_ref[...], neg)
    v1 = jnp.max(lg, axis=1, keepdims=True)
    i1 = jnp.min(jnp.where(lg == v1, lane, float(LANE)), axis=1, keepdims=True)
    oh1 = lane == i1
    lg2 = jnp.where(oh1, neg, lg)
    v2 = jnp.max(lg2, axis=1, keepdims=True)
    i2 = jnp.min(jnp.where(lg2 == v2, lane, float(LANE)), axis=1, keepdims=True)
    oh2 = lane == i2
    e = jnp.exp(v2 - v1)
    w1 = 1.0 / (1.0 + e)
    w2 = e / (1.0 + e)
    oh = oh1.astype(F32) + oh2.astype(F32)
    rr = lax.broadcasted_iota(jnp.int32, (DISP_BLK, DISP_BLK), 0)
    cc = lax.broadcasted_iota(jnp.int32, (DISP_BLK, DISP_BLK), 1)
    earlier = (cc < rr).astype(BF16)
    before = _dot(earlier, oh.astype(BF16))
    pos1 = jnp.sum(jnp.where(oh1, before, 0.0), axis=1, keepdims=True)
    pos2 = jnp.sum(jnp.where(oh2, before, 0.0), axis=1, keepdims=True)
    blk_cnt = jnp.sum(oh, axis=0, keepdims=True)
    cnt_ref[0] = jnp.broadcast_to(blk_cnt, (SUBLANE, LANE))
    packed = jnp.where(lane == 0, i1,
             jnp.where(lane == 1, i2,
             jnp.where(lane == 2, w1,
             jnp.where(lane == 3, w2,
             jnp.where(lane == 4, pos1,
             jnp.where(lane == 5, pos2, 0.0))))))
    info_ref[0] = packed.T[0:SUBLANE, :]


def _router(logits):
    t = logits.shape[0]
    nblk = t // DISP_BLK
    return pl.pallas_call(
        _router_kernel,
        grid=(nblk,),
        in_specs=[pl.BlockSpec((DISP_BLK, LANE), lambda i: (i, 0))],
        out_specs=[pl.BlockSpec((1, SUBLANE, DISP_BLK), lambda i: (i, 0, 0)),
                   pl.BlockSpec((1, SUBLANE, LANE), lambda i: (i, 0, 0))],
        out_shape=[jax.ShapeDtypeStruct((nblk, SUBLANE, DISP_BLK), F32),
                   jax.ShapeDtypeStruct((nblk, SUBLANE, LANE), F32)],
        compiler_params=_cparams(("parallel",)),
        name="router",
    )(logits)


def _dispatch_plan(info, cnt):
    nblk = info.shape[0]
    t = nblk * DISP_BLK
    max_rows = 2 * t + nblk * N_EXP * (SEG_ALIGN - 1) + N_EXP * (MOE_TILE - 1)
    ntiles = -(-max_rows // MOE_TILE)
    e12 = info[:, 0:2, :].astype(jnp.int32)
    w12 = info[:, 2:4, :]
    rank12 = info[:, 4:6, :].astype(jnp.int32)
    bc = cnt[:, 0, :N_EXP].astype(jnp.int32)
    pc = (bc + SEG_ALIGN - 1) // SEG_ALIGN * SEG_ALIGN
    loff = jnp.cumsum(pc, axis=1) - pc
    etot = jnp.sum(pc, axis=0)
    epad = (etot + MOE_TILE - 1) // MOE_TILE * MOE_TILE
    eend = jnp.cumsum(epad)
    goff = (eend - epad)[None, :] + jnp.cumsum(pc, axis=0) - pc
    experts = jnp.arange(N_EXP, dtype=jnp.int32)
    lpos = jnp.sum(jnp.where(e12[..., None] == experts, loff[:, None, None, :], 0), axis=-1) + rank12
    tile_row = jnp.arange(ntiles, dtype=jnp.int32) * MOE_TILE
    tile_e = jnp.minimum(jnp.sum((eend[None, :] <= tile_row[:, None]).astype(jnp.int32), axis=1),
                         N_EXP - 1)
    tile_v = (tile_row < eend[-1]).astype(jnp.int32)
    tail_end = eend.at[N_EXP - 1].set(ntiles * MOE_TILE)
    tail_off = eend - epad + etot
    return dict(lpos=lpos, w12=w12, tile_e=tile_e, tile_v=tile_v, ntiles=ntiles,
                segs=(goff.reshape(-1), loff.reshape(-1), (pc // SEG_ALIGN).reshape(-1)),
                tails=(tail_off, (tail_end - tail_off) // SEG_ALIGN))


def _for_each_segment_group(goff_ref, loff_ref, ngrp_ref, blk, fn):
    per_piece = SEG_PIECE // SEG_ALIGN
    for e in range(N_EXP):
        idx = blk * N_EXP + e
        lo, go, ngrp = loff_ref[idx], goff_ref[idx], ngrp_ref[idx]
        nbig = ngrp // per_piece

        def big(i, carry):
            fn(pl.multiple_of(lo + i * SEG_PIECE, SEG_ALIGN), pl.multiple_of(go + i * SEG_PIECE, SEG_ALIGN),
               SEG_PIECE)
            return carry

        def small(i, carry):
            fn(pl.multiple_of(lo + i * SEG_ALIGN, SEG_ALIGN), pl.multiple_of(go + i * SEG_ALIGN, SEG_ALIGN),
               SEG_ALIGN)
            return carry

        lax.fori_loop(0, nbig, big, 0)
        lax.fori_loop(nbig * per_piece, ngrp, small, 0)


def _local_rows(lpos_ref):
    rows = lax.broadcasted_iota(jnp.int32, (LROWS, DISP_BLK), 0)
    return rows == lpos_ref[0, 0:1, :], rows == lpos_ref[0, 1:2, :]


def _gather_kernel(goff_ref, loff_ref, ngrp_ref, toff_ref, tgrp_ref, h_ref, lpos_ref,
                   xp_ref, xl_ref, zero_ref, sem):
    b = pl.program_id(0)
    nblk = pl.num_programs(0)
    slot = b % 2

    def copy(slot_, lrow, srow, nrows):
        return pltpu.make_async_copy(xl_ref.at[slot_, pl.ds(lrow, nrows), :],
                                     xp_ref.at[pl.ds(srow, nrows), :], sem.at[slot_])

    def each(blk, slot_, start):
        _for_each_segment_group(
            goff_ref, loff_ref, ngrp_ref, blk,
            lambda lrow, srow, n: copy(slot_, lrow, srow, n).start() if start else copy(slot_, lrow, srow, n).wait())

    def each_tail(fn):
        for e in range(N_EXP):
            def body(i, carry):
                row = pl.multiple_of(toff_ref[e] + i * SEG_ALIGN, SEG_ALIGN)
                fn(pltpu.make_async_copy(zero_ref, xp_ref.at[pl.ds(row, SEG_ALIGN), :], sem.at[2]))
                return carry

            lax.fori_loop(0, tgrp_ref[e], body, 0)

    @pl.when(b == 0)
    def _():
        zero_ref[...] = jnp.zeros_like(zero_ref)
        each_tail(lambda cp: cp.start())

    @pl.when(b >= 2)
    def _():
        each(b - 2, slot, start=False)

    hit1, hit2 = _local_rows(lpos_ref)
    sel = jnp.logical_or(hit1, hit2).astype(BF16)
    xl_ref[slot] = _dot(sel, h_ref[...]).astype(BF16)
    each(b, slot, start=True)

    @pl.when(b == nblk - 1)
    def _():
        each(b, slot, start=False)
        each_tail(lambda cp: cp.wait())

        @pl.when(b >= 1)
        def _():
            each(b - 1, 1 - slot, start=False)


def _gather_rows(plan, h):
    nrows = plan["ntiles"] * MOE_TILE
    nblk = h.shape[0] // DISP_BLK
    grid_spec = pltpu.PrefetchScalarGridSpec(
        num_scalar_prefetch=5,
        grid=(nblk,),
        in_specs=[
            pl.BlockSpec((DISP_BLK, D), lambda b, *_: (b, 0)),
            pl.BlockSpec((1, 2, DISP_BLK), lambda b, *_: (b, 0, 0)),
        ],
        out_specs=pl.BlockSpec(memory_space=pl.ANY),
        scratch_shapes=[pltpu.VMEM((2, LROWS, D), BF16), pltpu.VMEM((SEG_ALIGN, D), BF16),
                        pltpu.SemaphoreType.DMA((3,))],
    )
    return pl.pallas_call(
        _gather_kernel,
        grid_spec=grid_spec,
        out_shape=jax.ShapeDtypeStruct((nrows, D), BF16),
        compiler_params=_cparams(("arbitrary",)),
        name="moe_gather",
    )(*plan["segs"], *plan["tails"], h, plan["lpos"])


def _combine_kernel(goff_ref, loff_ref, ngrp_ref, yp_ref, lpos_ref, w_ref, x_ref, gt_ref, gf_ref,
                    op_ref, os_ref, yl_ref, sem, *, nb0):
    b = pl.program_id(0)
    nblk = pl.num_programs(0)
    slot = b % 2

    def copy(slot_, lrow, srow, nrows):
        return pltpu.make_async_copy(yp_ref.at[pl.ds(srow, nrows), :],
                                     yl_ref.at[slot_, pl.ds(lrow, nrows), :], sem.at[slot_])

    def each(blk, slot_, start):
        _for_each_segment_group(
            goff_ref, loff_ref, ngrp_ref, blk,
            lambda lrow, srow, n: copy(slot_, lrow, srow, n).start() if start else copy(slot_, lrow, srow, n).wait())

    @pl.when(b == 0)
    def _():
        yl_ref[...] = jnp.zeros_like(yl_ref)
        each(0, 0, start=True)

    @pl.when(b + 1 < nblk)
    def _():
        each(b + 1, 1 - slot, start=True)

    each(b, slot, start=False)

    hit1, hit2 = _local_rows(lpos_ref)
    wsel = jnp.where(hit1, w_ref[0, 0:1, :], 0.0) + jnp.where(hit2, w_ref[0, 1:2, :], 0.0)
    wh, wl = _split_bf16(wsel)
    tn = (((0,), (0,)), ((), ()))
    yl = yl_ref[slot]
    y = (lax.dot_general(wh, yl, tn, preferred_element_type=F32)
         + lax.dot_general(wl, yl, tn, preferred_element_type=F32))
    x2 = x_ref[...] + gt_ref[0] * y
    ms = jnp.mean(x2 * x2, axis=-1, keepdims=True)
    out = x2 * lax.rsqrt(ms + EPS) * gf_ref[...]

    @pl.when(b < nb0)
    def _():
        op_ref[...] = out

    @pl.when(b >= nb0)
    def _():
        os_ref[...] = out


def _combine(st, plan, yp, x, gt, gf):
    nblk = st.t // DISP_BLK
    nb0 = st.t0 // DISP_BLK
    blk3 = lambda b, *_: (b, 0, 0)
    grid_spec = pltpu.PrefetchScalarGridSpec(
        num_scalar_prefetch=3,
        grid=(nblk,),
        in_specs=[
            pl.BlockSpec(memory_space=pl.ANY),
            pl.BlockSpec((1, 2, DISP_BLK), blk3),
            pl.BlockSpec((1, 2, DISP_BLK), blk3),
            pl.BlockSpec((DISP_BLK, D), lambda b, *_: (b, 0)),
            pl.BlockSpec((1, 1, D), lambda b, *_: (st.batch_of_block(b, DISP_BLK), 0, 0)),
            pl.BlockSpec((1, D), lambda b, *_: (0, 0)),
        ],
        out_specs=[pl.BlockSpec((DISP_BLK, D), lambda b, *_: (jnp.minimum(b, nb0 - 1), 0)),
                   pl.BlockSpec((DISP_BLK, D), lambda b, *_: (jnp.maximum(b - nb0, 0), 0))],
        scratch_shapes=[pltpu.VMEM((2, LROWS, D), BF16), pltpu.SemaphoreType.DMA((2,))],
    )
    return pl.pallas_call(
        functools.partial(_combine_kernel, nb0=nb0),
        grid_spec=grid_spec,
        out_shape=[jax.ShapeDtypeStruct((st.t0, D), F32), jax.ShapeDtypeStruct((st.t - st.t0, D), F32)],
        compiler_params=_cparams(("arbitrary",)),
        name="moe_combine_norm",
    )(*plan["segs"], yp, plan["lpos"], plan["w12"], x, gt, gf)


def _pack_w_in(w):
    o = np.cumsum([0, GLA_DK, GLA_DK, GLA_DV, GLA_DV, RANK, RANK, FNET_W, D, D]).tolist()
    gates = jnp.pad(jnp.tile(w[:, o[4]:o[6]], (1, 3)), ((0, 0), (0, LANE - 6 * RANK)))
    return jnp.concatenate([w[:, o[0]:o[4]], gates, w[:, o[6]:o[9]]], axis=1).astype(BF16)


def _gate_weights(w_a2, offset):
    hi, lo = _split_bf16(w_a2)
    out = jnp.zeros((LANE, GLA_DK), BF16)
    for copy, part in enumerate((hi, hi, lo)):
        out = lax.dynamic_update_slice(out, part, (copy * 2 * RANK + offset, 0))
    return out


def kernel(x_prompt, x_sample, c_prompt, c_sample, norm1_g, norm2_g, w_ada, b_ada, w_in, w_af2, b_af, w_ab2, b_ab, gla_norm_g, w_o_gla, w_o_fnet, w_out, w_ff_gate, w_ff_up, w_ff_down, w_router, w_e_gate, w_e_up, w_e_down, final_norm_g):
    b, s, _ = x_prompt.shape
    db, ds, _ = x_sample.shape
    depth = w_in.shape[0]
    assert depth == 2 and s % max(GLA_BLK, TM, DISP_BLK, SUBLANE * N1) == 0
    assert ds % max(GLA_BLK, TM, DISP_BLK, SUBLANE * N1) == 0 and (b * s) % ds == 0
    st = _Stream(b, s, db, ds)

    xs = (x_prompt.reshape(b * s, D), x_sample.reshape(db * ds, D))
    c = jnp.concatenate([c_prompt, c_sample], axis=0)
    nb_pad = -(-st.nbatch // SUBLANE) * SUBLANE
    c = jnp.pad(c, ((0, nb_pad - st.nbatch), (0, 0)))
    mod = _modulation(c, w_ada, b_ada)

    def mod_piece(l, j):
        return mod[l, :, j * D:(j + 1) * D].reshape(nb_pad, 1, D)

    for l in range(depth):
        sh1, sc1, gt1, sh2, sc2, gt2 = [mod_piece(l, j) for j in range(N_MOD)]
        q, k, v, r, za, zf, ga, gb = _inproj(st, xs, norm1_g[l].reshape(1, D), sc1, sh1, _pack_w_in(w_in[l]))

        wa_f = _gate_weights(w_af2[l], 0)
        wa_b = _gate_weights(w_ab2[l], RANK)
        o_gla = (_gla_direction(st, False, q, k, v, za, wa_f, b_af[l].reshape(1, GLA_DK)),
                 _gla_direction(st, True, q, k, v, za, wa_b, b_ab[l].reshape(1, GLA_DK)),
                 r, gla_norm_g[l].reshape(1, DV))

        yf = _fnet(st, zf)

        wog, wof, wout = w_o_gla[l].astype(BF16), w_o_fnet[l].astype(BF16), w_out[l].astype(BF16)
        g2 = norm2_g[l].reshape(1, D)
        if l % 2 == 0:
            x1, h2 = _merge(st, o_gla, yf, ga, gb, xs, gt1, g2, sc2, sh2, wog, wof, wout)
            xs = (_ffn_dense(st, h2, x1, gt2, w_ff_gate[l // 2].astype(BF16),
                             w_ff_up[l // 2].astype(BF16), w_ff_down[l // 2].astype(BF16)),)
        else:
            wr = jnp.pad(w_router[l // 2], ((0, 0), (0, LANE - N_EXP)))
            x1, h2, logits = _merge(st, o_gla, yf, ga, gb, xs, gt1, g2, sc2, sh2, wog, wof, wout, wr)
            info, cnt = _router(logits)
            plan = _dispatch_plan(info, cnt)
            xp = _gather_rows(plan, h2)
            yp = _ffn_experts(plan["tile_e"], plan["tile_v"], xp, w_e_gate[l // 2].astype(BF16),
                              w_e_up[l // 2].astype(BF16), w_e_down[l // 2].astype(BF16))
            y_prompt, y_sample = _combine(st, plan, yp, x1, gt2, final_norm_g.reshape(1, D))

    return y_prompt.reshape(b, s, D), y_sample.reshape(db, ds, D)
```

```python
import functools

import numpy as np
import jax
import jax.numpy as jnp
from jax import lax
from jax.experimental import pallas as pl
from jax.experimental.pallas import tpu as pltpu

D = 1024
HEADS = 4
DK = 128
DV = 256
GLA_DK = HEADS * DK
GLA_DV = HEADS * DV
RANK = 16
GATE_NORM = 16.0
CHUNK = 64
FN_GROUPS = 4
FN_GW = 128
FNET_W = FN_GROUPS * FN_GW
D_FF = 2816
N_EXP = 8
D_EXP = 3584
EPS = 1e-6
N_MOD = 6

LANE = 128
SUBLANE = 8
VMEM_LIMIT = 56 * 1024 * 1024
TM = 512
ROW_PART = 256
GLA_BLK = 1024
MOE_TILE = 512
DISP_BLK = 512
SEG_ALIGN = 16
SEG_PIECE = 128
LROWS = 2 * DISP_BLK + N_EXP * SEG_ALIGN
FF_CHUNK_DENSE = 256
FF_CHUNK_MOE = 256
N1 = 128
FN_ROWS = 16

F32 = jnp.float32
BF16 = jnp.bfloat16
_HI = lax.Precision.HIGHEST


def _cparams(sem):
    return pltpu.CompilerParams(dimension_semantics=sem, vmem_limit_bytes=VMEM_LIMIT)


def _dot(a, b):
    return jnp.dot(a, b, preferred_element_type=F32)


def _dot_hi(a, b):
    return jnp.dot(a, b, precision=_HI, preferred_element_type=F32)


def _split_bf16(x):
    hi = x.astype(BF16)
    lo = (x - hi.astype(F32)).astype(BF16)
    return hi, lo


def _dot3(a_hi, a_lo, b_hi, b_lo):
    return _dot(a_hi, b_hi) + (_dot(a_hi, b_lo) + _dot(a_lo, b_hi))


class _Stream:
    def __init__(self, b, s, db, ds):
        self.b, self.s, self.db, self.ds = b, s, db, ds
        self.t0 = b * s
        self.t = b * s + db * ds
        self.nbatch = b + db

    def batch_of_block(self, i, rows):
        nb0, p0, p1 = self.t0 // rows, self.s // rows, self.ds // rows
        return jnp.where(i < nb0, i // p0, self.b + (i - nb0) // p1)

    def block_in_seq(self, i, rows):
        nb0, p0, p1 = self.t0 // rows, self.s // rows, self.ds // rows
        return jnp.where(i < nb0, i % p0, (i - nb0) % p1), jnp.where(i < nb0, p0, p1)


def _mod_kernel(c_ref, w_ref, b_ref, o_ref):
    c = c_ref[...]
    cs = c * jax.nn.sigmoid(c)
    o_ref[0] = _dot_hi(cs, w_ref[0]) + b_ref[0]


def _modulation(c_pad, w_ada, b_ada):
    depth = w_ada.shape[0]
    nb = c_pad.shape[0]
    return pl.pallas_call(
        _mod_kernel,
        grid=(depth, N_MOD),
        in_specs=[
            pl.BlockSpec((nb, D), lambda l, j: (0, 0)),
            pl.BlockSpec((1, D, D), lambda l, j: (l, 0, j)),
            pl.BlockSpec((1, 1, D), lambda l, j: (l, 0, j)),
        ],
        out_specs=pl.BlockSpec((1, nb, D), lambda l, j: (l, 0, j)),
        out_shape=jax.ShapeDtypeStruct((depth, nb, N_MOD * D), F32),
        compiler_params=_cparams(("arbitrary", "arbitrary")),
        name="adaln_mod",
    )(c_pad, w_ada, b_ada.reshape(depth, 1, N_MOD * D))


def _rms_mod(x, g, sc, sh):
    ms = jnp.mean(x * x, axis=-1, keepdims=True)
    return (x * lax.rsqrt(ms + EPS) * g) * (1.0 + sc) + sh


_C_Q, _C_K, _C_V, _C_R, _C_A, _C_F, _C_GA, _C_GB, _C_END = np.cumsum(
    [0, GLA_DK, GLA_DK, GLA_DV, GLA_DV, LANE, FNET_W, D, D]).tolist()


def _stream_specs(st, xs, rows):
    width = xs[0].shape[1]
    if len(xs) == 1:
        return [pl.BlockSpec((rows, width), lambda i, *_: (i, 0))]
    nb0 = st.t0 // rows
    return [pl.BlockSpec((rows, width), lambda i, *_: (jnp.minimum(i, nb0 - 1), 0)),
            pl.BlockSpec((rows, width), lambda i, *_: (jnp.maximum(i - nb0, 0), 0))]


def _stream_block(st, x_refs, rows, part=slice(None)):
    if len(x_refs) == 1:
        return x_refs[0][part, :]
    return jnp.where(pl.program_id(0) < st.t0 // rows, x_refs[0][part, :], x_refs[1][part, :])


def _row_parts(rows):
    return [slice(p * ROW_PART, (p + 1) * ROW_PART) for p in range(rows // ROW_PART)]


def _inproj_kernel(*refs, st, nx):
    x_refs = refs[:nx]
    (g_ref, sc_ref, sh_ref, w_ref,
     q_ref, k_ref, v_ref, r_ref, a_ref, f_ref, ga_ref, gb_ref) = refs[nx:]
    for part in _row_parts(TM):
        hb = _rms_mod(_stream_block(st, x_refs, TM, part), g_ref[...], sc_ref[0], sh_ref[0]).astype(BF16)

        def proj(lo, hi):
            return _dot(hb, w_ref[:, lo:hi])

        q_ref[part, :] = (proj(_C_Q, _C_K) * (DK ** -0.5)).astype(BF16)
        k_ref[part, :] = proj(_C_K, _C_V).astype(BF16)
        v_ref[part, :] = proj(_C_V, _C_R).astype(BF16)
        zr = proj(_C_R, _C_A)
        r_ref[part, :] = (zr * jax.nn.sigmoid(zr)).astype(BF16)
        a_ref[part, :] = proj(_C_A, _C_F)
        f_ref[part, :] = proj(_C_F, _C_GA)
        ga_ref[part, :] = jax.nn.sigmoid(proj(_C_GA, _C_GB)).astype(BF16)
        gb_ref[part, :] = jax.nn.sigmoid(proj(_C_GB, _C_END)).astype(BF16)


def _inproj(st, xs, g, sc, sh, w):
    t = st.t
    row = lambda i: (i, 0)
    bat = lambda i: (st.batch_of_block(i, TM), 0, 0)
    widths = [(GLA_DK, BF16), (GLA_DK, BF16), (GLA_DV, BF16), (GLA_DV, BF16),
              (LANE, F32), (FNET_W, F32), (D, BF16), (D, BF16)]
    return pl.pallas_call(
        functools.partial(_inproj_kernel, st=st, nx=len(xs)),
        grid=(t // TM,),
        in_specs=_stream_specs(st, xs, TM) + [
            pl.BlockSpec((1, D), lambda i: (0, 0)),
            pl.BlockSpec((1, 1, D), bat),
            pl.BlockSpec((1, 1, D), bat),
            pl.BlockSpec((D, _C_END), lambda i: (0, 0)),
        ],
        out_specs=[pl.BlockSpec((TM, w_), row) for w_, _ in widths],
        out_shape=[jax.ShapeDtypeStruct((t, w_), dt) for w_, dt in widths],
        compiler_params=_cparams(("parallel",)),
        name="inproj",
    )(*xs, g, sc, sh, w)


def _log_sigmoid(x):
    return jnp.minimum(x, 0.0) - jnp.log(1.0 + jnp.exp(-jnp.abs(x)))


def _gla_kernel(*refs, bwd, st):
    if bwd:
        (q_ref, k_ref, v_ref, za_ref, wa_ref, ba_ref, of_ref, r_ref, ng_ref,
         o_ref, s_scr, qi_scr, ki_scr, qin_scr, kout_scr, att_scr, kv_scr, dec_scr) = refs
    else:
        (q_ref, k_ref, v_ref, za_ref, wa_ref, ba_ref,
         o_ref, s_scr, qi_scr, ki_scr, qin_scr, kout_scr, att_scr, kv_scr, dec_scr) = refs
    i = pl.program_id(0)
    nblk = pl.num_programs(0)
    blk = (nblk - 1 - i) if bwd else i
    local, per = st.block_in_seq(blk, GLA_BLK)
    at_boundary = (local == per - 1) if bwd else (local == 0)

    @pl.when(at_boundary)
    def _():
        s_scr[...] = jnp.zeros_like(s_scr)

    nch = GLA_BLK // CHUNK
    rr = lax.broadcasted_iota(jnp.int32, (CHUNK, CHUNK), 0)
    cc = lax.broadcasted_iota(jnp.int32, (CHUNK, CHUNK), 1)
    keep = (rr <= cc) if bwd else (rr >= cc)
    tri = keep.astype(BF16)
    last = 0 if bwd else CHUNK - 1
    mid = CHUNK // 2 if bwd else CHUNK // 2 - 1

    lane = lax.broadcasted_iota(jnp.int32, (CHUNK, LANE), 1)
    low_part_lanes = jnp.logical_and(lane >= 2 * RANK, lane < 4 * RANK)

    log_decay = {}

    def gate_chunk(c):
        rows = slice(c * CHUNK, (c + 1) * CHUNK)
        zh, zl = _split_bf16(za_ref[rows, :])
        pre = _dot(jnp.where(low_part_lanes, zl, zh), wa_ref[...]) + ba_ref[...]
        g = _log_sigmoid(pre) * (1.0 / GATE_NORM)
        log_decay[c] = _split_bf16(g)

    def scale_chunk(c):
        rows = slice(c * CHUNK, (c + 1) * CHUNK)
        g0, g1 = log_decay.pop(c)
        b = _dot(tri, g0) + _dot(tri, g1)
        b_last = b[last:last + 1, :]
        b_ref = b[mid:mid + 1, :]
        dec_scr[c] = jnp.broadcast_to(jnp.exp(b_last), (LANE, GLA_DK)).T
        q = q_ref[rows, :].astype(F32)
        k = k_ref[rows, :].astype(F32)
        qi_scr[rows, :] = (q * jnp.exp(b - b_ref)).astype(BF16)
        ki_scr[rows, :] = (k * jnp.exp(b_ref - b)).astype(BF16)
        qin_scr[rows, :] = (q * jnp.exp(b)).astype(BF16)
        kout_scr[rows, :] = (k * jnp.exp(b_last - b)).astype(BF16)

    def stateless_products(c):
        rows = slice(c * CHUNK, (c + 1) * CHUNK)
        for h in range(HEADS):
            ks = slice(h * DK, (h + 1) * DK)
            att = lax.dot_general(qi_scr[rows, ks], ki_scr[rows, ks], (((1,), (1,)), ((), ())),
                                  preferred_element_type=F32)
            att_scr[c * HEADS + h] = jnp.where(keep, att, 0.0).astype(BF16)
            kv_scr[c * HEADS + h] = lax.dot_general(
                kout_scr[rows, ks], v_ref[rows, h * DV:(h + 1) * DV], (((0,), (0,)), ((), ())),
                preferred_element_type=F32)

    state = [s_scr[h] for h in range(HEADS)]

    def scan_chunk(c):
        rows = slice(c * CHUNK, (c + 1) * CHUNK)
        for h in range(HEADS):
            ks = slice(h * DK, (h + 1) * DK)
            vs = slice(h * DV, (h + 1) * DV)
            o = _dot(jnp.concatenate([qin_scr[rows, ks], att_scr[c * HEADS + h]], axis=1),
                     jnp.concatenate([state[h].astype(BF16), v_ref[rows, vs]], axis=0))
            decay = dec_scr[c, ks, :]
            state[h] = state[h] * jnp.concatenate([decay] * (DV // LANE), axis=1) + kv_scr[c * HEADS + h]
            if bwd:
                o = o + of_ref[rows, vs]
                o = o * lax.rsqrt(jnp.mean(o * o, axis=-1, keepdims=True) + EPS) * ng_ref[...]
                o_ref[rows, vs] = (o * r_ref[rows, vs].astype(F32)).astype(o_ref.dtype)
            else:
                o_ref[rows, vs] = o

    order = list(reversed(range(nch))) if bwd else list(range(nch))
    stages = (gate_chunk, scale_chunk, stateless_products, scan_chunk)
    for step in range(nch + len(stages) - 1):
        for lag, stage in enumerate(stages):
            if 0 <= step - lag < nch:
                stage(order[step - lag])
    for h in range(HEADS):
        s_scr[h] = state[h]


def _gla_direction(st, bwd, q, k, v, za, wa, ba, extra=()):
    t = st.t
    nblk = t // GLA_BLK
    row = (lambda i: (nblk - 1 - i, 0)) if bwd else (lambda i: (i, 0))
    const = lambda i: (0, 0)
    in_specs = [
        pl.BlockSpec((GLA_BLK, GLA_DK), row),
        pl.BlockSpec((GLA_BLK, GLA_DK), row),
        pl.BlockSpec((GLA_BLK, GLA_DV), row),
        pl.BlockSpec((GLA_BLK, LANE), row),
        pl.BlockSpec((LANE, GLA_DK), const),
        pl.BlockSpec((1, GLA_DK), const),
    ]
    if bwd:
        in_specs += [pl.BlockSpec((GLA_BLK, GLA_DV), row), pl.BlockSpec((GLA_BLK, GLA_DV), row),
                     pl.BlockSpec((1, DV), const)]
    scaled = pltpu.VMEM((GLA_BLK, GLA_DK), BF16)
    nch = GLA_BLK // CHUNK
    return pl.pallas_call(
        functools.partial(_gla_kernel, bwd=bwd, st=st),
        grid=(nblk,),
        in_specs=in_specs,
        out_specs=pl.BlockSpec((GLA_BLK, GLA_DV), row),
        out_shape=jax.ShapeDtypeStruct((t, GLA_DV), BF16 if bwd else F32),
        scratch_shapes=[pltpu.VMEM((HEADS, DK, DV), F32), scaled, scaled, scaled, scaled,
                        pltpu.VMEM((nch * HEADS, CHUNK, CHUNK), BF16),
                        pltpu.VMEM((nch * HEADS, DK, DV), F32),
                        pltpu.VMEM((nch, GLA_DK, LANE), F32)],
        compiler_params=_cparams(("arbitrary",)),
        name="gla_bwd" if bwd else "gla_fwd",
    )(q, k, v, za, wa, ba, *extra)


def _dft_cos_sin(n):
    idx = np.arange(n, dtype=np.float64)
    ang = 2.0 * np.pi * np.outer(idx, idx) / n
    return np.cos(ang) / np.sqrt(n), np.sin(ang) / np.sqrt(n)


def _fnet_stage1_kernel(x_ref, m_ref, y_ref):
    n2 = x_ref.shape[0]
    for half in range(FN_ROWS // SUBLANE):
        rows = slice(half * SUBLANE, (half + 1) * SUBLANE)
        x = x_ref[:, rows, :].reshape(n2 * SUBLANE, FNET_W).astype(BF16)
        y = _dot(m_ref[...], x).reshape(2, n2, SUBLANE, FNET_W)
        y_ref[:, :, rows, :] = y.astype(BF16)


def _fnet_stage1(zf3, bn, n2, seq0, m1):
    return pl.pallas_call(
        _fnet_stage1_kernel,
        grid=(bn, N1 // FN_ROWS),
        in_specs=[
            pl.BlockSpec((n2, FN_ROWS, FNET_W), lambda b, j: (seq0 + b, j, 0)),
            pl.BlockSpec(m1.shape, lambda b, j: (0, 0)),
        ],
        out_specs=pl.BlockSpec((2, n2, FN_ROWS, FNET_W), lambda b, j: (0, b, j, 0)),
        out_shape=jax.ShapeDtypeStruct((2, bn * n2, N1, FNET_W), BF16),
        compiler_params=_cparams(("parallel", "parallel")),
        name="fnet_stage1",
    )(zf3, m1)


def _fnet_stage2_kernel(y_ref, tc_ref, ts_ref, m2_ref, m3_ref, o_ref, z_scr):
    for g in range(SUBLANE):
        rows = slice(g * N1, (g + 1) * N1)
        yr = y_ref[0, rows, :].astype(F32)
        yi = y_ref[1, rows, :].astype(F32)
        tc = tc_ref[g]
        ts = ts_ref[g]
        stack = jnp.concatenate([yr * tc + yi * ts, yi * tc - yr * ts], axis=0)
        z = _dot(m2_ref[...], stack.astype(BF16))
        z_scr[rows, 0:FNET_W] = z[:N1].astype(BF16)
        z_scr[rows, FNET_W:2 * FNET_W] = z[N1:].astype(BF16)
    out = _dot(z_scr[...], m3_ref[...])
    for g in range(SUBLANE):
        o_ref[:, g, :] = out[g * N1:(g + 1) * N1]


def _fnet_stage2(y2, bn, n2, tc, ts, m2, m3):
    rows = SUBLANE * N1
    per_seq = n2 // SUBLANE
    const = lambda b, j: (0, 0)
    return pl.pallas_call(
        _fnet_stage2_kernel,
        grid=(bn, per_seq),
        in_specs=[
            pl.BlockSpec((2, rows, FNET_W), lambda b, j: (0, b * per_seq + j, 0)),
            pl.BlockSpec((SUBLANE, N1, 1), lambda b, j: (j, 0, 0)),
            pl.BlockSpec((SUBLANE, N1, 1), lambda b, j: (j, 0, 0)),
            pl.BlockSpec((2 * N1, 2 * N1), const),
            pl.BlockSpec((2 * FNET_W, FNET_W), const),
        ],
        out_specs=pl.BlockSpec((N1, SUBLANE, FNET_W), lambda b, j: (b, j, 0)),
        out_shape=jax.ShapeDtypeStruct((bn * N1, n2, FNET_W), F32),
        scratch_shapes=[pltpu.VMEM((rows, 2 * FNET_W), BF16)],
        compiler_params=_cparams(("parallel", "parallel")),
        name="fnet_stage2",
    )(y2, tc, ts, m2, m3).reshape(bn * n2 * N1, FNET_W)


def _fnet(st, zf):
    as_bf16 = lambda m: jnp.asarray(np.asarray(m, np.float32).astype(BF16))
    c1, s1 = _dft_cos_sin(N1)
    m2 = as_bf16(np.block([[c1, s1], [-s1, c1]]))
    cc, sc = _dft_cos_sin(FN_GW)
    eye = np.eye(FN_GROUPS)
    m3 = as_bf16(np.concatenate([np.kron(eye, cc), np.kron(eye, sc)], axis=0))
    zf3 = zf.reshape(st.t // N1, N1, FNET_W)
    outs = []
    for bn, s, row0 in ((st.b, st.s, 0), (st.db, st.ds, st.t0)):
        n2 = s // N1
        c2, s2 = _dft_cos_sin(n2)
        m1 = as_bf16(np.kron(np.concatenate([c2, -s2], axis=0), np.eye(SUBLANE)))
        ang = 2.0 * np.pi * np.outer(np.arange(n2), np.arange(N1)) / s
        tc = jnp.asarray(np.cos(ang)[:, :, None], F32)
        ts = jnp.asarray(np.sin(ang)[:, :, None], F32)
        y = _fnet_stage1(zf3, bn, n2, row0 // s, m1)
        outs.append(_fnet_stage2(y.reshape(2, bn * s, FNET_W), bn, n2, tc, ts, m2, m3))
    return tuple(outs)


def _merge_kernel(*refs, st, nx, router):
    x_refs = refs[:nx]
    yf_refs = refs[nx:nx + 2]
    (og_ref, ga_ref, gb_ref, gt_ref, g2_ref, sc_ref, sh_ref,
     wog_ref, wof_ref, wout_ref) = refs[nx + 2:nx + 12]
    rest = refs[nx + 12:]
    if router:
        wrh_ref, wrl_ref, x1_ref, h2_ref, lg_ref = rest
    else:
        x1_ref, h2_ref = rest
    for part in _row_parts(TM):
        ya = _dot(og_ref[part, :], wog_ref[...])
        yb = _dot(_stream_block(st, yf_refs, TM, part).astype(BF16), wof_ref[...])
        m = ga_ref[part, :].astype(F32) * ya + gb_ref[part, :].astype(F32) * yb
        x1 = _stream_block(st, x_refs, TM, part) + gt_ref[0] * _dot(m.astype(BF16), wout_ref[...])
        h2 = _rms_mod(x1, g2_ref[...], sc_ref[0], sh_ref[0])
        if router:
            lg_ref[part, :] = _dot3(*_split_bf16(h2), wrh_ref[...], wrl_ref[...])
        x1_ref[part, :] = x1
        h2_ref[part, :] = h2.astype(BF16)


def _merge(st, og, yfs, ga, gb, xs, gt, g2, sc, sh, wog, wof, wout, wr=None):
    t = st.t
    row = lambda i: (i, 0)
    bat = lambda i: (st.batch_of_block(i, TM), 0, 0)
    const = lambda i: (0, 0)
    in_specs = _stream_specs(st, xs, TM) + _stream_specs(st, yfs, TM) + [
        pl.BlockSpec((TM, GLA_DV), row),
        pl.BlockSpec((TM, D), row), pl.BlockSpec((TM, D), row),
        pl.BlockSpec((1, 1, D), bat), pl.BlockSpec((1, D), const),
        pl.BlockSpec((1, 1, D), bat), pl.BlockSpec((1, 1, D), bat),
        pl.BlockSpec((GLA_DV, D), const), pl.BlockSpec((FNET_W, D), const), pl.BlockSpec((D, D), const),
    ]
    out_specs = [pl.BlockSpec((TM, D), row), pl.BlockSpec((TM, D), row)]
    out_shape = [jax.ShapeDtypeStruct((t, D), F32), jax.ShapeDtypeStruct((t, D), BF16)]
    args = [*xs, *yfs, og, ga, gb, gt, g2, sc, sh, wog, wof, wout]
    if wr is not None:
        in_specs += [pl.BlockSpec((D, LANE), const), pl.BlockSpec((D, LANE), const)]
        out_specs.append(pl.BlockSpec((TM, LANE), row))
        out_shape.append(jax.ShapeDtypeStruct((t, LANE), F32))
        args += list(_split_bf16(wr))
    return pl.pallas_call(
        functools.partial(_merge_kernel, st=st, nx=len(xs), router=wr is not None),
        grid=(t // TM,),
        in_specs=in_specs, out_specs=out_specs, out_shape=out_shape,
        compiler_params=_cparams(("parallel",)),
        name="merge_router" if wr is not None else "merge",
    )(*args)


def _swiglu_into(acc_ref, h, wg_ref, wu_ref, wd_ref, width, chunk, lead=()):
    for c in range(width // chunk):
        cols = slice(c * chunk, (c + 1) * chunk)
        g = _dot(h, wg_ref[lead + (slice(None), cols)])
        u = _dot(h, wu_ref[lead + (slice(None), cols)])
        a = (g * jax.nn.sigmoid(g) * u).astype(BF16)
        part = _dot(a, wd_ref[lead + (cols, slice(None))])
        if c == 0:
            acc_ref[...] = part
        else:
            acc_ref[...] += part


def _ffn_dense_kernel(h_ref, x_ref, gt_ref, wg_ref, wu_ref, wd_ref, o_ref, acc_ref):
    _swiglu_into(acc_ref, h_ref[...], wg_ref, wu_ref, wd_ref, D_FF, FF_CHUNK_DENSE)
    o_ref[...] = x_ref[...] + gt_ref[0] * acc_ref[...]


def _ffn_dense(st, h, x, gt, wg, wu, wd):
    t = st.t
    row = lambda i: (i, 0)
    const = lambda i: (0, 0)
    once = pl.Buffered(1)
    return pl.pallas_call(
        _ffn_dense_kernel,
        grid=(t // TM,),
        in_specs=[
            pl.BlockSpec((TM, D), row), pl.BlockSpec((TM, D), row),
            pl.BlockSpec((1, 1, D), lambda i: (st.batch_of_block(i, TM), 0, 0)),
            pl.BlockSpec((D, D_FF), const, pipeline_mode=once),
            pl.BlockSpec((D, D_FF), const, pipeline_mode=once),
            pl.BlockSpec((D_FF, D), const, pipeline_mode=once),
        ],
        out_specs=pl.BlockSpec((TM, D), row),
        out_shape=jax.ShapeDtypeStruct((t, D), F32),
        scratch_shapes=[pltpu.VMEM((TM, D), F32)],
        compiler_params=_cparams(("parallel",)),
        name="ffn_dense",
    )(h, x, gt, wg, wu, wd)


def _ffn_expert_kernel(te_ref, tv_ref, x_ref, wg_ref, wu_ref, wd_ref, o_ref, acc_ref):
    k = pl.program_id(0)

    @pl.when(tv_ref[k] == 1)
    def _():
        _swiglu_into(acc_ref, x_ref[...], wg_ref, wu_ref, wd_ref, D_EXP, FF_CHUNK_MOE, lead=(0,))
        o_ref[...] = acc_ref[...].astype(o_ref.dtype)

    @pl.when(tv_ref[k] == 0)
    def _():
        o_ref[...] = jnp.zeros_like(o_ref)


def _ffn_experts(tile_e, tile_v, xp, wg, wu, wd):
    ntiles = xp.shape[0] // MOE_TILE
    once = pl.Buffered(1)
    wmap = lambda k, te, tv: (te[k], 0, 0)
    xmap = lambda k, te, tv: (k * tv[k], 0)
    grid_spec = pltpu.PrefetchScalarGridSpec(
        num_scalar_prefetch=2,
        grid=(ntiles,),
        in_specs=[
            pl.BlockSpec((MOE_TILE, D), xmap),
            pl.BlockSpec((1, D, D_EXP), wmap, pipeline_mode=once),
            pl.BlockSpec((1, D, D_EXP), wmap, pipeline_mode=once),
            pl.BlockSpec((1, D_EXP, D), wmap, pipeline_mode=once),
        ],
        out_specs=pl.BlockSpec((MOE_TILE, D), lambda k, te, tv: (k, 0)),
        scratch_shapes=[pltpu.VMEM((MOE_TILE, D), F32)],
    )
    return pl.pallas_call(
        _ffn_expert_kernel,
        grid_spec=grid_spec,
        out_shape=jax.ShapeDtypeStruct(xp.shape, BF16),
        compiler_params=_cparams(("arbitrary",)),
        name="ffn_experts",
    )(tile_e, tile_v, xp, wg, wu, wd)


def _router_kernel(lg_ref, info_ref, cnt_ref):
    lane = lax.broadcasted_iota(jnp.int32, (DISP_BLK, LANE), 1).astype(F32)
    neg = jnp.float32(-jnp.inf)
    lg = jnp.where(lane < N_EXP, lg_ref[...], neg)
    v1 = jnp.max(lg, axis=1, keepdims=True)
    i1 = jnp.min(jnp.where(lg == v1, lane, float(LANE)), axis=1, keepdims=True)
    oh1 = lane == i1
    lg2 = jnp.where(oh1, neg, lg)
    v2 = jnp.max(lg2, axis=1, keepdims=True)
    i2 = jnp.min(jnp.where(lg2 == v2, lane, float(LANE)), axis=1, keepdims=True)
    oh2 = lane == i2
    e = jnp.exp(v2 - v1)
    w1 = 1.0 / (1.0 + e)
    w2 = e / (1.0 + e)
    oh = oh1.astype(F32) + oh2.astype(F32)
    rr = lax.broadcasted_iota(jnp.int32, (DISP_BLK, DISP_BLK), 0)
    cc = lax.broadcasted_iota(jnp.int32, (DISP_BLK, DISP_BLK), 1)
    earlier = (cc < rr).astype(BF16)
    before = _dot(earlier, oh.astype(BF16))
    pos1 = jnp.sum(jnp.where(oh1, before, 0.0), axis=1, keepdims=True)
    pos2 = jnp.sum(jnp.where(oh2, before, 0.0), axis=1, keepdims=True)
    blk_cnt = jnp.sum(oh, axis=0, keepdims=True)
    cnt_ref[0] = jnp.broadcast_to(blk_cnt, (SUBLANE, LANE))
    packed = jnp.where(lane == 0, i1,
             jnp.where(lane == 1, i2,
             jnp.where(lane == 2, w1,
             jnp.where(lane == 3, w2,
             jnp.where(lane == 4, pos1,
             jnp.where(lane == 5, pos2, 0.0))))))
    info_ref[0] = packed.T[0:SUBLANE, :]


def _router(logits):
    t = logits.shape[0]
    nblk = t // DISP_BLK
    return pl.pallas_call(
        _router_kernel,
        grid=(nblk,),
        in_specs=[pl.BlockSpec((DISP_BLK, LANE), lambda i: (i, 0))],
        out_specs=[pl.BlockSpec((1, SUBLANE, DISP_BLK), lambda i: (i, 0, 0)),
                   pl.BlockSpec((1, SUBLANE, LANE), lambda i: (i, 0, 0))],
        out_shape=[jax.ShapeDtypeStruct((nblk, SUBLANE, DISP_BLK), F32),
                   jax.ShapeDtypeStruct((nblk, SUBLANE, LANE), F32)],
        compiler_params=_cparams(("parallel",)),
        name="router",
    )(logits)


def _dispatch_plan(info, cnt):
    nblk = info.shape[0]
    t = nblk * DISP_BLK
    max_rows = 2 * t + nblk * N_EXP * (SEG_ALIGN - 1) + N_EXP * (MOE_TILE - 1)
    ntiles = -(-max_rows // MOE_TILE)
    e12 = info[:, 0:2, :].astype(jnp.int32)
    w12 = info[:, 2:4, :]
    rank12 = info[:, 4:6, :].astype(jnp.int32)
    bc = cnt[:, 0, :N_EXP].astype(jnp.int32)
    pc = (bc + SEG_ALIGN - 1) // SEG_ALIGN * SEG_ALIGN
    loff = jnp.cumsum(pc, axis=1) - pc
    etot = jnp.sum(pc, axis=0)
    epad = (etot + MOE_TILE - 1) // MOE_TILE * MOE_TILE
    eend = jnp.cumsum(epad)
    goff = (eend - epad)[None, :] + jnp.cumsum(pc, axis=0) - pc
    experts = jnp.arange(N_EXP, dtype=jnp.int32)
    lpos = jnp.sum(jnp.where(e12[..., None] == experts, loff[:, None, None, :], 0), axis=-1) + rank12
    tile_row = jnp.arange(ntiles, dtype=jnp.int32) * MOE_TILE
    tile_e = jnp.minimum(jnp.sum((eend[None, :] <= tile_row[:, None]).astype(jnp.int32), axis=1),
                         N_EXP - 1)
    tile_v = (tile_row < eend[-1]).astype(jnp.int32)
    tail_end = eend.at[N_EXP - 1].set(ntiles * MOE_TILE)
    tail_off = eend - epad + etot
    return dict(lpos=lpos, w12=w12, tile_e=tile_e, tile_v=tile_v, ntiles=ntiles,
                segs=(goff.reshape(-1), loff.reshape(-1), (pc // SEG_ALIGN).reshape(-1)),
                tails=(tail_off, (tail_end - tail_off) // SEG_ALIGN))


def _for_each_segment_group(goff_ref, loff_ref, ngrp_ref, blk, fn):
    per_piece = SEG_PIECE // SEG_ALIGN
    for e in range(N_EXP):
        idx = blk * N_EXP + e
        lo, go, ngrp = loff_ref[idx], goff_ref[idx], ngrp_ref[idx]
        nbig = ngrp // per_piece

        def big(i, carry):
            fn(pl.multiple_of(lo + i * SEG_PIECE, SEG_ALIGN), pl.multiple_of(go + i * SEG_PIECE, SEG_ALIGN),
               SEG_PIECE)
            return carry

        def small(i, carry):
            fn(pl.multiple_of(lo + i * SEG_ALIGN, SEG_ALIGN), pl.multiple_of(go + i * SEG_ALIGN, SEG_ALIGN),
               SEG_ALIGN)
            return carry

        lax.fori_loop(0, nbig, big, 0)
        lax.fori_loop(nbig * per_piece, ngrp, small, 0)


def _local_rows(lpos_ref):
    rows = lax.broadcasted_iota(jnp.int32, (LROWS, DISP_BLK), 0)
    return rows == lpos_ref[0, 0:1, :], rows == lpos_ref[0, 1:2, :]


def _gather_kernel(goff_ref, loff_ref, ngrp_ref, toff_ref, tgrp_ref, h_ref, lpos_ref,
                   xp_ref, xl_ref, zero_ref, sem):
    b = pl.program_id(0)
    nblk = pl.num_programs(0)
    slot = b % 2

    def copy(slot_, lrow, srow, nrows):
        return pltpu.make_async_copy(xl_ref.at[slot_, pl.ds(lrow, nrows), :],
                                     xp_ref.at[pl.ds(srow, nrows), :], sem.at[slot_])

    def each(blk, slot_, start):
        _for_each_segment_group(
            goff_ref, loff_ref, ngrp_ref, blk,
            lambda lrow, srow, n: copy(slot_, lrow, srow, n).start() if start else copy(slot_, lrow, srow, n).wait())

    def each_tail(fn):
        for e in range(N_EXP):
            def body(i, carry):
                row = pl.multiple_of(toff_ref[e] + i * SEG_ALIGN, SEG_ALIGN)
                fn(pltpu.make_async_copy(zero_ref, xp_ref.at[pl.ds(row, SEG_ALIGN), :], sem.at[2]))
                return carry

            lax.fori_loop(0, tgrp_ref[e], body, 0)

    @pl.when(b == 0)
    def _():
        zero_ref[...] = jnp.zeros_like(zero_ref)
        each_tail(lambda cp: cp.start())

    @pl.when(b >= 2)
    def _():
        each(b - 2, slot, start=False)

    hit1, hit2 = _local_rows(lpos_ref)
    sel = jnp.logical_or(hit1, hit2).astype(BF16)
    xl_ref[slot] = _dot(sel, h_ref[...]).astype(BF16)
    each(b, slot, start=True)

    @pl.when(b == nblk - 1)
    def _():
        each(b, slot, start=False)
        each_tail(lambda cp: cp.wait())

        @pl.when(b >= 1)
        def _():
            each(b - 1, 1 - slot, start=False)


def _gather_rows(plan, h):
    nrows = plan["ntiles"] * MOE_TILE
    nblk = h.shape[0] // DISP_BLK
    grid_spec = pltpu.PrefetchScalarGridSpec(
        num_scalar_prefetch=5,
        grid=(nblk,),
        in_specs=[
            pl.BlockSpec((DISP_BLK, D), lambda b, *_: (b, 0)),
            pl.BlockSpec((1, 2, DISP_BLK), lambda b, *_: (b, 0, 0)),
        ],
        out_specs=pl.BlockSpec(memory_space=pl.ANY),
        scratch_shapes=[pltpu.VMEM((2, LROWS, D), BF16), pltpu.VMEM((SEG_ALIGN, D), BF16),
                        pltpu.SemaphoreType.DMA((3,))],
    )
    return pl.pallas_call(
        _gather_kernel,
        grid_spec=grid_spec,
        out_shape=jax.ShapeDtypeStruct((nrows, D), BF16),
        compiler_params=_cparams(("arbitrary",)),
        name="moe_gather",
    )(*plan["segs"], *plan["tails"], h, plan["lpos"])


def _combine_kernel(goff_ref, loff_ref, ngrp_ref, yp_ref, lpos_ref, w_ref, x_ref, gt_ref, gf_ref,
                    op_ref, os_ref, yl_ref, sem, *, nb0):
    b = pl.program_id(0)
    nblk = pl.num_programs(0)
    slot = b % 2

    def copy(slot_, lrow, srow, nrows):
        return pltpu.make_async_copy(yp_ref.at[pl.ds(srow, nrows), :],
                                     yl_ref.at[slot_, pl.ds(lrow, nrows), :], sem.at[slot_])

    def each(blk, slot_, start):
        _for_each_segment_group(
            goff_ref, loff_ref, ngrp_ref, blk,
            lambda lrow, srow, n: copy(slot_, lrow, srow, n).start() if start else copy(slot_, lrow, srow, n).wait())

    @pl.when(b == 0)
    def _():
        yl_ref[...] = jnp.zeros_like(yl_ref)
        each(0, 0, start=True)

    @pl.when(b + 1 < nblk)
    def _():
        each(b + 1, 1 - slot, start=True)

    each(b, slot, start=False)

    hit1, hit2 = _local_rows(lpos_ref)
    wsel = jnp.where(hit1, w_ref[0, 0:1, :], 0.0) + jnp.where(hit2, w_ref[0, 1:2, :], 0.0)
    wh, wl = _split_bf16(wsel)
    tn = (((0,), (0,)), ((), ()))
    yl = yl_ref[slot]
    y = (lax.dot_general(wh, yl, tn, preferred_element_type=F32)
         + lax.dot_general(wl, yl, tn, preferred_element_type=F32))
    x2 = x_ref[...] + gt_ref[0] * y
    ms = jnp.mean(x2 * x2, axis=-1, keepdims=True)
    out = x2 * lax.rsqrt(ms + EPS) * gf_ref[...]

    @pl.when(b < nb0)
    def _():
        op_ref[...] = out

    @pl.when(b >= nb0)
    def _():
        os_ref[...] = out


def _combine(st, plan, yp, x, gt, gf):
    nblk = st.t // DISP_BLK
    nb0 = st.t0 // DISP_BLK
    blk3 = lambda b, *_: (b, 0, 0)
    grid_spec = pltpu.PrefetchScalarGridSpec(
        num_scalar_prefetch=3,
        grid=(nblk,),
        in_specs=[
            pl.BlockSpec(memory_space=pl.ANY),
            pl.BlockSpec((1, 2, DISP_BLK), blk3),
            pl.BlockSpec((1, 2, DISP_BLK), blk3),
            pl.BlockSpec((DISP_BLK, D), lambda b, *_: (b, 0)),
            pl.BlockSpec((1, 1, D), lambda b, *_: (st.batch_of_block(b, DISP_BLK), 0, 0)),
            pl.BlockSpec((1, D), lambda b, *_: (0, 0)),
        ],
        out_specs=[pl.BlockSpec((DISP_BLK, D), lambda b, *_: (jnp.minimum(b, nb0 - 1), 0)),
                   pl.BlockSpec((DISP_BLK, D), lambda b, *_: (jnp.maximum(b - nb0, 0), 0))],
        scratch_shapes=[pltpu.VMEM((2, LROWS, D), BF16), pltpu.SemaphoreType.DMA((2,))],
    )
    return pl.pallas_call(
        functools.partial(_combine_kernel, nb0=nb0),
        grid_spec=grid_spec,
        out_shape=[jax.ShapeDtypeStruct((st.t0, D), F32), jax.ShapeDtypeStruct((st.t - st.t0, D), F32)],
        compiler_params=_cparams(("arbitrary",)),
        name="moe_combine_norm",
    )(*plan["segs"], yp, plan["lpos"], plan["w12"], x, gt, gf)


def _pack_w_in(w):
    o = np.cumsum([0, GLA_DK, GLA_DK, GLA_DV, GLA_DV, RANK, RANK, FNET_W, D, D]).tolist()
    gates = jnp.pad(jnp.tile(w[:, o[4]:o[6]], (1, 3)), ((0, 0), (0, LANE - 6 * RANK)))
    return jnp.concatenate([w[:, o[0]:o[4]], gates, w[:, o[6]:o[9]]], axis=1).astype(BF16)


def _gate_weights(w_a2, offset):
    hi, lo = _split_bf16(w_a2)
    out = jnp.zeros((LANE, GLA_DK), BF16)
    for copy, part in enumerate((hi, hi, lo)):
        out = lax.dynamic_update_slice(out, part, (copy * 2 * RANK + offset, 0))
    return out


def kernel(x_prompt, x_sample, c_prompt, c_sample, norm1_g, norm2_g, w_ada, b_ada, w_in, w_af2, b_af, w_ab2, b_ab, gla_norm_g, w_o_gla, w_o_fnet, w_out, w_ff_gate, w_ff_up, w_ff_down, w_router, w_e_gate, w_e_up, w_e_down, final_norm_g):
    b, s, _ = x_prompt.shape
    db, ds, _ = x_sample.shape
    depth = w_in.shape[0]
    assert depth == 2 and s % max(GLA_BLK, TM, DISP_BLK, SUBLANE * N1) == 0
    assert ds % max(GLA_BLK, TM, DISP_BLK, SUBLANE * N1) == 0 and (b * s) % ds == 0
    st = _Stream(b, s, db, ds)

    xs = (x_prompt.reshape(b * s, D), x_sample.reshape(db * ds, D))
    c = jnp.concatenate([c_prompt, c_sample], axis=0)
    nb_pad = -(-st.nbatch // SUBLANE) * SUBLANE
    c = jnp.pad(c, ((0, nb_pad - st.nbatch), (0, 0)))
    mod = _modulation(c, w_ada, b_ada)

    def mod_piece(l, j):
        return mod[l, :, j * D:(j + 1) * D].reshape(nb_pad, 1, D)

    for l in range(depth):
        sh1, sc1, gt1, sh2, sc2, gt2 = [mod_piece(l, j) for j in range(N_MOD)]
        q, k, v, r, za, zf, ga, gb = _inproj(st, xs, norm1_g[l].reshape(1, D), sc1, sh1, _pack_w_in(w_in[l]))

        wa_f = _gate_weights(w_af2[l], 0)
        wa_b = _gate_weights(w_ab2[l], RANK)
        o_f = _gla_direction(st, False, q, k, v, za, wa_f, b_af[l].reshape(1, GLA_DK))
        o_gla = _gla_direction(st, True, q, k, v, za, wa_b, b_ab[l].reshape(1, GLA_DK),
                               extra=(o_f, r, gla_norm_g[l].reshape(1, DV)))

        yf = _fnet(st, zf)

        wog, wof, wout = w_o_gla[l].astype(BF16), w_o_fnet[l].astype(BF16), w_out[l].astype(BF16)
        g2 = norm2_g[l].reshape(1, D)
        if l % 2 == 0:
            x1, h2 = _merge(st, o_gla, yf, ga, gb, xs, gt1, g2, sc2, sh2, wog, wof, wout)
            xs = (_ffn_dense(st, h2, x1, gt2, w_ff_gate[l // 2].astype(BF16),
                             w_ff_up[l // 2].astype(BF16), w_ff_down[l // 2].astype(BF16)),)
        else:
            wr = jnp.pad(w_router[l // 2], ((0, 0), (0, LANE - N_EXP)))
            x1, h2, logits = _merge(st, o_gla, yf, ga, gb, xs, gt1, g2, sc2, sh2, wog, wof, wout, wr)
            info, cnt = _router(logits)
            plan = _dispatch_plan(info, cnt)
            xp = _gather_rows(plan, h2)
            yp = _ffn_experts(plan["tile_e"], plan["tile_v"], xp, w_e_gate[l // 2].astype(BF16),
                              w_e_up[l // 2].astype(BF16), w_e_down[l // 2].astype(BF16))
            y_prompt, y_sample = _combine(st, plan, yp, x1, gt2, final_norm_g.reshape(1, D))

    return y_prompt.reshape(b, s, D), y_sample.reshape(db, ds, D)
```

```python
import functools

import numpy as np
import jax
import jax.numpy as jnp
from jax import lax
from jax.experimental import pallas as pl
from jax.experimental.pallas import tpu as pltpu

D = 1024
HEADS = 4
DK = 128
DV = 256
GLA_DK = HEADS * DK
GLA_DV = HEADS * DV
RANK = 16
GATE_NORM = 16.0
CHUNK = 64
FN_GROUPS = 4
FN_GW = 128
FNET_W = FN_GROUPS * FN_GW
D_FF = 2816
N_EXP = 8
D_EXP = 3584
EPS = 1e-6
N_MOD = 6

LANE = 128
SUBLANE = 8
VMEM_LIMIT = 56 * 1024 * 1024
TM = 512
ROW_PART = 256
GLA_BLK = 1024
MOE_TILE = 512
DISP_BLK = 512
SEG_ALIGN = 16
SEG_PIECE = 128
LROWS = 2 * DISP_BLK + N_EXP * SEG_ALIGN
FF_CHUNK_DENSE = 256
FF_CHUNK_MOE = 256
N1 = 128
FN_ROWS = 16

F32 = jnp.float32
BF16 = jnp.bfloat16
_HI = lax.Precision.HIGHEST


def _cparams(sem):
    return pltpu.CompilerParams(dimension_semantics=sem, vmem_limit_bytes=VMEM_LIMIT)


def _dot(a, b):
    return jnp.dot(a, b, preferred_element_type=F32)


def _dot_hi(a, b):
    return jnp.dot(a, b, precision=_HI, preferred_element_type=F32)


def _split_bf16(x):
    hi = x.astype(BF16)
    lo = (x - hi.astype(F32)).astype(BF16)
    return hi, lo


def _dot3(a_hi, a_lo, b_hi, b_lo):
    return _dot(a_hi, b_hi) + (_dot(a_hi, b_lo) + _dot(a_lo, b_hi))


class _Stream:
    def __init__(self, b, s, db, ds):
        self.b, self.s, self.db, self.ds = b, s, db, ds
        self.t0 = b * s
        self.t = b * s + db * ds
        self.nbatch = b + db

    def batch_of_block(self, i, rows):
        nb0, p0, p1 = self.t0 // rows, self.s // rows, self.ds // rows
        return jnp.where(i < nb0, i // p0, self.b + (i - nb0) // p1)

    def block_in_seq(self, i, rows):
        nb0, p0, p1 = self.t0 // rows, self.s // rows, self.ds // rows
        return jnp.where(i < nb0, i % p0, (i - nb0) % p1), jnp.where(i < nb0, p0, p1)


def _mod_kernel(c_ref, w_ref, b_ref, o_ref):
    c = c_ref[...]
    cs = c * jax.nn.sigmoid(c)
    o_ref[0] = _dot_hi(cs, w_ref[0]) + b_ref[0]


def _modulation(c_pad, w_ada, b_ada):
    depth = w_ada.shape[0]
    nb = c_pad.shape[0]
    return pl.pallas_call(
        _mod_kernel,
        grid=(depth, N_MOD),
        in_specs=[
            pl.BlockSpec((nb, D), lambda l, j: (0, 0)),
            pl.BlockSpec((1, D, D), lambda l, j: (l, 0, j)),
            pl.BlockSpec((1, 1, D), lambda l, j: (l, 0, j)),
        ],
        out_specs=pl.BlockSpec((1, nb, D), lambda l, j: (l, 0, j)),
        out_shape=jax.ShapeDtypeStruct((depth, nb, N_MOD * D), F32),
        compiler_params=_cparams(("arbitrary", "arbitrary")),
        name="adaln_mod",
    )(c_pad, w_ada, b_ada.reshape(depth, 1, N_MOD * D))


def _rms_mod(x, g, sc, sh):
    ms = jnp.mean(x * x, axis=-1, keepdims=True)
    return (x * lax.rsqrt(ms + EPS) * g) * (1.0 + sc) + sh


_C_Q, _C_K, _C_V, _C_R, _C_A, _C_F, _C_GA, _C_GB, _C_END = np.cumsum(
    [0, GLA_DK, GLA_DK, GLA_DV, GLA_DV, LANE, FNET_W, D, D]).tolist()


def _stream_specs(st, xs, rows):
    width = xs[0].shape[1]
    if len(xs) == 1:
        return [pl.BlockSpec((rows, width), lambda i, *_: (i, 0))]
    nb0 = st.t0 // rows
    return [pl.BlockSpec((rows, width), lambda i, *_: (jnp.minimum(i, nb0 - 1), 0)),
            pl.BlockSpec((rows, width), lambda i, *_: (jnp.maximum(i - nb0, 0), 0))]


def _stream_block(st, x_refs, rows, part=slice(None)):
    if len(x_refs) == 1:
        return x_refs[0][part, :]
    return jnp.where(pl.program_id(0) < st.t0 // rows, x_refs[0][part, :], x_refs[1][part, :])


def _row_parts(rows):
    return [slice(p * ROW_PART, (p + 1) * ROW_PART) for p in range(rows // ROW_PART)]


def _inproj_kernel(*refs, st, nx):
    x_refs = refs[:nx]
    (g_ref, sc_ref, sh_ref, w_ref,
     q_ref, k_ref, v_ref, r_ref, a_ref, f_ref, ga_ref, gb_ref) = refs[nx:]
    for part in _row_parts(TM):
        hb = _rms_mod(_stream_block(st, x_refs, TM, part), g_ref[...], sc_ref[0], sh_ref[0]).astype(BF16)

        def proj(lo, hi):
            return _dot(hb, w_ref[:, lo:hi])

        q_ref[part, :] = (proj(_C_Q, _C_K) * (DK ** -0.5)).astype(BF16)
        k_ref[part, :] = proj(_C_K, _C_V).astype(BF16)
        v_ref[part, :] = proj(_C_V, _C_R).astype(BF16)
        zr = proj(_C_R, _C_A)
        r_ref[part, :] = (zr * jax.nn.sigmoid(zr)).astype(BF16)
        a_ref[part, :] = proj(_C_A, _C_F)
        f_ref[part, :] = proj(_C_F, _C_GA).astype(BF16)
        ga_ref[part, :] = jax.nn.sigmoid(proj(_C_GA, _C_GB)).astype(BF16)
        gb_ref[part, :] = jax.nn.sigmoid(proj(_C_GB, _C_END)).astype(BF16)


def _inproj(st, xs, g, sc, sh, w):
    t = st.t
    row = lambda i: (i, 0)
    bat = lambda i: (st.batch_of_block(i, TM), 0, 0)
    widths = [(GLA_DK, BF16), (GLA_DK, BF16), (GLA_DV, BF16), (GLA_DV, BF16),
              (LANE, F32), (FNET_W, BF16), (D, BF16), (D, BF16)]
    return pl.pallas_call(
        functools.partial(_inproj_kernel, st=st, nx=len(xs)),
        grid=(t // TM,),
        in_specs=_stream_specs(st, xs, TM) + [
            pl.BlockSpec((1, D), lambda i: (0, 0)),
            pl.BlockSpec((1, 1, D), bat),
            pl.BlockSpec((1, 1, D), bat),
            pl.BlockSpec((D, _C_END), lambda i: (0, 0)),
        ],
        out_specs=[pl.BlockSpec((TM, w_), row) for w_, _ in widths],
        out_shape=[jax.ShapeDtypeStruct((t, w_), dt) for w_, dt in widths],
        compiler_params=_cparams(("parallel",)),
        name="inproj",
    )(*xs, g, sc, sh, w)


def _log_sigmoid(x):
    return jnp.minimum(x, 0.0) - jnp.log(1.0 + jnp.exp(-jnp.abs(x)))


def _gla_kernel(*refs, bwd, st):
    if bwd:
        (q_ref, k_ref, v_ref, za_ref, wa_ref, ba_ref, of_ref, r_ref, ng_ref,
         o_ref, s_scr, qi_scr, ki_scr, qin_scr, kout_scr, att_scr, kv_scr, dec_scr) = refs
    else:
        (q_ref, k_ref, v_ref, za_ref, wa_ref, ba_ref,
         o_ref, s_scr, qi_scr, ki_scr, qin_scr, kout_scr, att_scr, kv_scr, dec_scr) = refs
    i = pl.program_id(0)
    nblk = pl.num_programs(0)
    blk = (nblk - 1 - i) if bwd else i
    local, per = st.block_in_seq(blk, GLA_BLK)
    at_boundary = (local == per - 1) if bwd else (local == 0)

    @pl.when(at_boundary)
    def _():
        s_scr[...] = jnp.zeros_like(s_scr)

    nch = GLA_BLK // CHUNK
    rr = lax.broadcasted_iota(jnp.int32, (CHUNK, CHUNK), 0)
    cc = lax.broadcasted_iota(jnp.int32, (CHUNK, CHUNK), 1)
    keep = (rr <= cc) if bwd else (rr >= cc)
    tri = keep.astype(BF16)
    last = 0 if bwd else CHUNK - 1
    mid = CHUNK // 2 if bwd else CHUNK // 2 - 1

    lane = lax.broadcasted_iota(jnp.int32, (CHUNK, LANE), 1)
    low_part_lanes = jnp.logical_and(lane >= 2 * RANK, lane < 4 * RANK)

    log_decay = {}

    def gate_chunk(c):
        rows = slice(c * CHUNK, (c + 1) * CHUNK)
        zh, zl = _split_bf16(za_ref[rows, :])
        pre = _dot(jnp.where(low_part_lanes, zl, zh), wa_ref[...]) + ba_ref[...]
        g = _log_sigmoid(pre) * (1.0 / GATE_NORM)
        log_decay[c] = _split_bf16(g)

    def scale_chunk(c):
        rows = slice(c * CHUNK, (c + 1) * CHUNK)
        g0, g1 = log_decay.pop(c)
        b = _dot(tri, g0) + _dot(tri, g1)
        b_last = b[last:last + 1, :]
        b_ref = b[mid:mid + 1, :]
        dec_scr[c] = jnp.broadcast_to(jnp.exp(b_last), (LANE, GLA_DK)).T
        q = q_ref[rows, :].astype(F32)
        k = k_ref[rows, :].astype(F32)
        qi_scr[rows, :] = (q * jnp.exp(b - b_ref)).astype(BF16)
        ki_scr[rows, :] = (k * jnp.exp(b_ref - b)).astype(BF16)
        qin_scr[rows, :] = (q * jnp.exp(b)).astype(BF16)
        kout_scr[rows, :] = (k * jnp.exp(b_last - b)).astype(BF16)

    def stateless_products(c):
        rows = slice(c * CHUNK, (c + 1) * CHUNK)
        for h in range(HEADS):
            ks = slice(h * DK, (h + 1) * DK)
            att = lax.dot_general(qi_scr[rows, ks], ki_scr[rows, ks], (((1,), (1,)), ((), ())),
                                  preferred_element_type=F32)
            att_scr[c * HEADS + h] = jnp.where(keep, att, 0.0).astype(BF16)
            kv_scr[c * HEADS + h] = lax.dot_general(
                kout_scr[rows, ks], v_ref[rows, h * DV:(h + 1) * DV], (((0,), (0,)), ((), ())),
                preferred_element_type=F32)

    state = [s_scr[h] for h in range(HEADS)]

    def scan_chunk(c):
        rows = slice(c * CHUNK, (c + 1) * CHUNK)
        for h in range(HEADS):
            ks = slice(h * DK, (h + 1) * DK)
            vs = slice(h * DV, (h + 1) * DV)
            o = _dot(jnp.concatenate([qin_scr[rows, ks], att_scr[c * HEADS + h]], axis=1),
                     jnp.concatenate([state[h].astype(BF16), v_ref[rows, vs]], axis=0))
            decay = dec_scr[c, ks, :]
            state[h] = state[h] * jnp.concatenate([decay] * (DV // LANE), axis=1) + kv_scr[c * HEADS + h]
            if bwd:
                o = o + of_ref[rows, vs]
                o = o * lax.rsqrt(jnp.mean(o * o, axis=-1, keepdims=True) + EPS) * ng_ref[...]
                o_ref[rows, vs] = (o * r_ref[rows, vs].astype(F32)).astype(o_ref.dtype)
            else:
                o_ref[rows, vs] = o

    order = list(reversed(range(nch))) if bwd else list(range(nch))
    stages = (gate_chunk, scale_chunk, stateless_products, scan_chunk)
    for step in range(nch + len(stages) - 1):
        for lag, stage in enumerate(stages):
            if 0 <= step - lag < nch:
                stage(order[step - lag])
    for h in range(HEADS):
        s_scr[h] = state[h]


def _gla_direction(st, bwd, q, k, v, za, wa, ba, extra=()):
    t = st.t
    nblk = t // GLA_BLK
    row = (lambda i: (nblk - 1 - i, 0)) if bwd else (lambda i: (i, 0))
    const = lambda i: (0, 0)
    in_specs = [
        pl.BlockSpec((GLA_BLK, GLA_DK), row),
        pl.BlockSpec((GLA_BLK, GLA_DK), row),
        pl.BlockSpec((GLA_BLK, GLA_DV), row),
        pl.BlockSpec((GLA_BLK, LANE), row),
        pl.BlockSpec((LANE, GLA_DK), const),
        pl.BlockSpec((1, GLA_DK), const),
    ]
    if bwd:
        in_specs += [pl.BlockSpec((GLA_BLK, GLA_DV), row), pl.BlockSpec((GLA_BLK, GLA_DV), row),
                     pl.BlockSpec((1, DV), const)]
    scaled = pltpu.VMEM((GLA_BLK, GLA_DK), BF16)
    nch = GLA_BLK // CHUNK
    return pl.pallas_call(
        functools.partial(_gla_kernel, bwd=bwd, st=st),
        grid=(nblk,),
        in_specs=in_specs,
        out_specs=pl.BlockSpec((GLA_BLK, GLA_DV), row),
        out_shape=jax.ShapeDtypeStruct((t, GLA_DV), BF16 if bwd else F32),
        scratch_shapes=[pltpu.VMEM((HEADS, DK, DV), F32), scaled, scaled, scaled, scaled,
                        pltpu.VMEM((nch * HEADS, CHUNK, CHUNK), BF16),
                        pltpu.VMEM((nch * HEADS, DK, DV), F32),
                        pltpu.VMEM((nch, GLA_DK, LANE), F32)],
        compiler_params=_cparams(("arbitrary",)),
        name="gla_bwd" if bwd else "gla_fwd",
    )(q, k, v, za, wa, ba, *extra)


def _dft_cos_sin(n):
    idx = np.arange(n, dtype=np.float64)
    ang = 2.0 * np.pi * np.outer(idx, idx) / n
    return np.cos(ang) / np.sqrt(n), np.sin(ang) / np.sqrt(n)


def _fnet_stage1_kernel(x_ref, m_ref, y_ref):
    n2 = x_ref.shape[0]
    xf = x_ref[...].astype(F32)
    for half in range(FN_ROWS // SUBLANE):
        rows = slice(half * SUBLANE, (half + 1) * SUBLANE)
        x = xf[:, rows, :].reshape(n2 * SUBLANE, FNET_W).astype(BF16)
        y = _dot(m_ref[...], x).reshape(2, n2, SUBLANE, FNET_W)
        y_ref[:, :, rows, :] = y.astype(BF16)


def _fnet_stage1(zf3, bn, n2, seq0, m1):
    return pl.pallas_call(
        _fnet_stage1_kernel,
        grid=(bn, N1 // FN_ROWS),
        in_specs=[
            pl.BlockSpec((n2, FN_ROWS, FNET_W), lambda b, j: (seq0 + b, j, 0)),
            pl.BlockSpec(m1.shape, lambda b, j: (0, 0)),
        ],
        out_specs=pl.BlockSpec((2, n2, FN_ROWS, FNET_W), lambda b, j: (0, b, j, 0)),
        out_shape=jax.ShapeDtypeStruct((2, bn * n2, N1, FNET_W), BF16),
        compiler_params=_cparams(("parallel", "parallel")),
        name="fnet_stage1",
    )(zf3, m1)


def _fnet_stage2_kernel(y_ref, tc_ref, ts_ref, m2_ref, m3_ref, o_ref, z_scr):
    for g in range(SUBLANE):
        rows = slice(g * N1, (g + 1) * N1)
        yr = y_ref[0, rows, :].astype(F32)
        yi = y_ref[1, rows, :].astype(F32)
        tc = tc_ref[g]
        ts = ts_ref[g]
        stack = jnp.concatenate([yr * tc + yi * ts, yi * tc - yr * ts], axis=0)
        z = _dot(m2_ref[...], stack.astype(BF16))
        z_scr[rows, 0:FNET_W] = z[:N1].astype(BF16)
        z_scr[rows, FNET_W:2 * FNET_W] = z[N1:].astype(BF16)
    out = _dot(z_scr[...], m3_ref[...])
    for g in range(SUBLANE):
        o_ref[:, g, :] = out[g * N1:(g + 1) * N1]


def _fnet_stage2(y2, bn, n2, tc, ts, m2, m3):
    rows = SUBLANE * N1
    per_seq = n2 // SUBLANE
    const = lambda b, j: (0, 0)
    return pl.pallas_call(
        _fnet_stage2_kernel,
        grid=(bn, per_seq),
        in_specs=[
            pl.BlockSpec((2, rows, FNET_W), lambda b, j: (0, b * per_seq + j, 0)),
            pl.BlockSpec((SUBLANE, N1, 1), lambda b, j: (j, 0, 0)),
            pl.BlockSpec((SUBLANE, N1, 1), lambda b, j: (j, 0, 0)),
            pl.BlockSpec((2 * N1, 2 * N1), const),
            pl.BlockSpec((2 * FNET_W, FNET_W), const),
        ],
        out_specs=pl.BlockSpec((N1, SUBLANE, FNET_W), lambda b, j: (b, j, 0)),
        out_shape=jax.ShapeDtypeStruct((bn * N1, n2, FNET_W), F32),
        scratch_shapes=[pltpu.VMEM((rows, 2 * FNET_W), BF16)],
        compiler_params=_cparams(("parallel", "parallel")),
        name="fnet_stage2",
    )(y2, tc, ts, m2, m3).reshape(bn * n2 * N1, FNET_W)


def _fnet(st, zf):
    as_bf16 = lambda m: jnp.asarray(np.asarray(m, np.float32).astype(BF16))
    c1, s1 = _dft_cos_sin(N1)
    m2 = as_bf16(np.block([[c1, s1], [-s1, c1]]))
    cc, sc = _dft_cos_sin(FN_GW)
    eye = np.eye(FN_GROUPS)
    m3 = as_bf16(np.concatenate([np.kron(eye, cc), np.kron(eye, sc)], axis=0))
    zf3 = zf.reshape(st.t // N1, N1, FNET_W)
    outs = []
    for bn, s, row0 in ((st.b, st.s, 0), (st.db, st.ds, st.t0)):
        n2 = s // N1
        c2, s2 = _dft_cos_sin(n2)
        m1 = as_bf16(np.kron(np.concatenate([c2, -s2], axis=0), np.eye(SUBLANE)))
        ang = 2.0 * np.pi * np.outer(np.arange(n2), np.arange(N1)) / s
        tc = jnp.asarray(np.cos(ang)[:, :, None], F32)
        ts = jnp.asarray(np.sin(ang)[:, :, None], F32)
        y = _fnet_stage1(zf3, bn, n2, row0 // s, m1)
        outs.append(_fnet_stage2(y.reshape(2, bn * s, FNET_W), bn, n2, tc, ts, m2, m3))
    return tuple(outs)


def _merge_kernel(*refs, st, nx, router):
    x_refs = refs[:nx]
    yf_refs = refs[nx:nx + 2]
    (og_ref, ga_ref, gb_ref, gt_ref, g2_ref, sc_ref, sh_ref,
     wog_ref, wof_ref, wout_ref) = refs[nx + 2:nx + 12]
    rest = refs[nx + 12:]
    if router:
        wrh_ref, wrl_ref, x1_ref, h2_ref, lg_ref = rest
    else:
        x1_ref, h2_ref = rest
    for part in _row_parts(TM):
        ya = _dot(og_ref[part, :], wog_ref[...])
        yb = _dot(_stream_block(st, yf_refs, TM, part).astype(BF16), wof_ref[...])
        m = ga_ref[part, :].astype(F32) * ya + gb_ref[part, :].astype(F32) * yb
        x1 = _stream_block(st, x_refs, TM, part) + gt_ref[0] * _dot(m.astype(BF16), wout_ref[...])
        h2 = _rms_mod(x1, g2_ref[...], sc_ref[0], sh_ref[0])
        if router:
            lg_ref[part, :] = _dot3(*_split_bf16(h2), wrh_ref[...], wrl_ref[...])
        x1_ref[part, :] = x1
        h2_ref[part, :] = h2.astype(BF16)


def _merge(st, og, yfs, ga, gb, xs, gt, g2, sc, sh, wog, wof, wout, wr=None):
    t = st.t
    row = lambda i: (i, 0)
    bat = lambda i: (st.batch_of_block(i, TM), 0, 0)
    const = lambda i: (0, 0)
    in_specs = _stream_specs(st, xs, TM) + _stream_specs(st, yfs, TM) + [
        pl.BlockSpec((TM, GLA_DV), row),
        pl.BlockSpec((TM, D), row), pl.BlockSpec((TM, D), row),
        pl.BlockSpec((1, 1, D), bat), pl.BlockSpec((1, D), const),
        pl.BlockSpec((1, 1, D), bat), pl.BlockSpec((1, 1, D), bat),
        pl.BlockSpec((GLA_DV, D), const), pl.BlockSpec((FNET_W, D), const), pl.BlockSpec((D, D), const),
    ]
    out_specs = [pl.BlockSpec((TM, D), row), pl.BlockSpec((TM, D), row)]
    out_shape = [jax.ShapeDtypeStruct((t, D), F32), jax.ShapeDtypeStruct((t, D), BF16)]
    args = [*xs, *yfs, og, ga, gb, gt, g2, sc, sh, wog, wof, wout]
    if wr is not None:
        in_specs += [pl.BlockSpec((D, LANE), const), pl.BlockSpec((D, LANE), const)]
        out_specs.append(pl.BlockSpec((TM, LANE), row))
        out_shape.append(jax.ShapeDtypeStruct((t, LANE), F32))
        args += list(_split_bf16(wr))
    return pl.pallas_call(
        functools.partial(_merge_kernel, st=st, nx=len(xs), router=wr is not None),
        grid=(t // TM,),
        in_specs=in_specs, out_specs=out_specs, out_shape=out_shape,
        compiler_params=_cparams(("parallel",)),
        name="merge_router" if wr is not None else "merge",
    )(*args)


def _swiglu_into(acc_ref, h, wg_ref, wu_ref, wd_ref, width, chunk, lead=()):
    for c in range(width // chunk):
        cols = slice(c * chunk, (c + 1) * chunk)
        g = _dot(h, wg_ref[lead + (slice(None), cols)])
        u = _dot(h, wu_ref[lead + (slice(None), cols)])
        a = (g * jax.nn.sigmoid(g) * u).astype(BF16)
        part = _dot(a, wd_ref[lead + (cols, slice(None))])
        if c == 0:
            acc_ref[...] = part
        else:
            acc_ref[...] += part


def _ffn_dense_kernel(h_ref, x_ref, gt_ref, wg_ref, wu_ref, wd_ref, o_ref, acc_ref):
    _swiglu_into(acc_ref, h_ref[...], wg_ref, wu_ref, wd_ref, D_FF, FF_CHUNK_DENSE)
    o_ref[...] = x_ref[...] + gt_ref[0] * acc_ref[...]


def _ffn_dense(st, h, x, gt, wg, wu, wd):
    t = st.t
    row = lambda i: (i, 0)
    const = lambda i: (0, 0)
    once = pl.Buffered(1)
    return pl.pallas_call(
        _ffn_dense_kernel,
        grid=(t // TM,),
        in_specs=[
            pl.BlockSpec((TM, D), row), pl.BlockSpec((TM, D), row),
            pl.BlockSpec((1, 1, D), lambda i: (st.batch_of_block(i, TM), 0, 0)),
            pl.BlockSpec((D, D_FF), const, pipeline_mode=once),
            pl.BlockSpec((D, D_FF), const, pipeline_mode=once),
            pl.BlockSpec((D_FF, D), const, pipeline_mode=once),
        ],
        out_specs=pl.BlockSpec((TM, D), row),
        out_shape=jax.ShapeDtypeStruct((t, D), F32),
        scratch_shapes=[pltpu.VMEM((TM, D), F32)],
        compiler_params=_cparams(("parallel",)),
        name="ffn_dense",
    )(h, x, gt, wg, wu, wd)


def _ffn_expert_kernel(te_ref, tv_ref, x_ref, wg_ref, wu_ref, wd_ref, o_ref, acc_ref):
    k = pl.program_id(0)

    @pl.when(tv_ref[k] == 1)
    def _():
        _swiglu_into(acc_ref, x_ref[...], wg_ref, wu_ref, wd_ref, D_EXP, FF_CHUNK_MOE, lead=(0,))
        o_ref[...] = acc_ref[...].astype(o_ref.dtype)

    @pl.when(tv_ref[k] == 0)
    def _():
        o_ref[...] = jnp.zeros_like(o_ref)


def _ffn_experts(tile_e, tile_v, xp, wg, wu, wd):
    ntiles = xp.shape[0] // MOE_TILE
    once = pl.Buffered(1)
    wmap = lambda k, te, tv: (te[k], 0, 0)
    xmap = lambda k, te, tv: (k * tv[k], 0)
    grid_spec = pltpu.PrefetchScalarGridSpec(
        num_scalar_prefetch=2,
        grid=(ntiles,),
        in_specs=[
            pl.BlockSpec((MOE_TILE, D), xmap),
            pl.BlockSpec((1, D, D_EXP), wmap, pipeline_mode=once),
            pl.BlockSpec((1, D, D_EXP), wmap, pipeline_mode=once),
            pl.BlockSpec((1, D_EXP, D), wmap, pipeline_mode=once),
        ],
        out_specs=pl.BlockSpec((MOE_TILE, D), lambda k, te, tv: (k, 0)),
        scratch_shapes=[pltpu.VMEM((MOE_TILE, D), F32)],
    )
    return pl.pallas_call(
        _ffn_expert_kernel,
        grid_spec=grid_spec,
        out_shape=jax.ShapeDtypeStruct(xp.shape, BF16),
        compiler_params=_cparams(("arbitrary",)),
        name="ffn_experts",
    )(tile_e, tile_v, xp, wg, wu, wd)


def _router_kernel(lg_ref, info_ref, cnt_ref):
    lane = lax.broadcasted_iota(jnp.int32, (DISP_BLK, LANE), 1).astype(F32)
    neg = jnp.float32(-jnp.inf)
    lg = jnp.where(lane < N_EXP, lg_ref[...], neg)
    v1 = jnp.max(lg, axis=1, keepdims=True)
    i1 = jnp.min(jnp.where(lg == v1, lane, float(LANE)), axis=1, keepdims=True)
    oh1 = lane == i1
    lg2 = jnp.where(oh1, neg, lg)
    v2 = jnp.max(lg2, axis=1, keepdims=True)
    i2 = jnp.min(jnp.where(lg2 == v2, lane, float(LANE)), axis=1, keepdims=True)
    oh2 = lane == i2
    e = jnp.exp(v2 - v1)
    w1 = 1.0 / (1.0 + e)
    w2 = e / (1.0 + e)
    oh = oh1.astype(F32) + oh2.astype(F32)
    rr = lax.broadcasted_iota(jnp.int32, (DISP_BLK, DISP_BLK), 0)
    cc = lax.broadcasted_iota(jnp.int32, (DISP_BLK, DISP_BLK), 1)
    earlier = (cc < rr).astype(BF16)
    before = _dot(earlier, oh.astype(BF16))
    pos1 = jnp.sum(jnp.where(oh1, before, 0.0), axis=1, keepdims=True)
    pos2 = jnp.sum(jnp.where(oh2, before, 0.0), axis=1, keepdims=True)
    blk_cnt = jnp.sum(oh, axis=0, keepdims=True)
    cnt_ref[0] = jnp.broadcast_to(blk_cnt, (SUBLANE, LANE))
    packed = jnp.where(lane == 0, i1,
             jnp.where(lane == 1, i2,
             jnp.where(lane == 2, w1,
             jnp.where(lane == 3, w2,
             jnp.where(lane == 4, pos1,
             jnp.where(lane == 5, pos2, 0.0))))))
    info_ref[0] = packed.T[0:SUBLANE, :]


def _router(logits):
    t = logits.shape[0]
    nblk = t // DISP_BLK
    return pl.pallas_call(
        _router_kernel,
        grid=(nblk,),
        in_specs=[pl.BlockSpec((DISP_BLK, LANE), lambda i: (i, 0))],
        out_specs=[pl.BlockSpec((1, SUBLANE, DISP_BLK), lambda i: (i, 0, 0)),
                   pl.BlockSpec((1, SUBLANE, LANE), lambda i: (i, 0, 0))],
        out_shape=[jax.ShapeDtypeStruct((nblk, SUBLANE, DISP_BLK), F32),
                   jax.ShapeDtypeStruct((nblk, SUBLANE, LANE), F32)],
        compiler_params=_cparams(("parallel",)),
        name="router",
    )(logits)


def _dispatch_plan(info, cnt):
    nblk = info.shape[0]
    t = nblk * DISP_BLK
    max_rows = 2 * t + nblk * N_EXP * (SEG_ALIGN - 1) + N_EXP * (MOE_TILE - 1)
    ntiles = -(-max_rows // MOE_TILE)
    e12 = info[:, 0:2, :].astype(jnp.int32)
    w12 = info[:, 2:4, :]
    rank12 = info[:, 4:6, :].astype(jnp.int32)
    bc = cnt[:, 0, :N_EXP].astype(jnp.int32)
    pc = (bc + SEG_ALIGN - 1) // SEG_ALIGN * SEG_ALIGN
    loff = jnp.cumsum(pc, axis=1) - pc
    etot = jnp.sum(pc, axis=0)
    epad = (etot + MOE_TILE - 1) // MOE_TILE * MOE_TILE
    eend = jnp.cumsum(epad)
    goff = (eend - epad)[None, :] + jnp.cumsum(pc, axis=0) - pc
    experts = jnp.arange(N_EXP, dtype=jnp.int32)
    lpos = jnp.sum(jnp.where(e12[..., None] == experts, loff[:, None, None, :], 0), axis=-1) + rank12
    tile_row = jnp.arange(ntiles, dtype=jnp.int32) * MOE_TILE
    tile_e = jnp.minimum(jnp.sum((eend[None, :] <= tile_row[:, None]).astype(jnp.int32), axis=1),
                         N_EXP - 1)
    tile_v = (tile_row < eend[-1]).astype(jnp.int32)
    tail_end = eend.at[N_EXP - 1].set(ntiles * MOE_TILE)
    tail_off = eend - epad + etot
    return dict(lpos=lpos, w12=w12, tile_e=tile_e, tile_v=tile_v, ntiles=ntiles,
                segs=(goff.reshape(-1), loff.reshape(-1), (pc // SEG_ALIGN).reshape(-1)),
                tails=(tail_off, (tail_end - tail_off) // SEG_ALIGN))


def _for_each_segment_group(goff_ref, loff_ref, ngrp_ref, blk, fn):
    per_piece = SEG_PIECE // SEG_ALIGN
    for e in range(N_EXP):
        idx = blk * N_EXP + e
        lo, go, ngrp = loff_ref[idx], goff_ref[idx], ngrp_ref[idx]
        nbig = ngrp // per_piece

        def big(i, carry):
            fn(pl.multiple_of(lo + i * SEG_PIECE, SEG_ALIGN), pl.multiple_of(go + i * SEG_PIECE, SEG_ALIGN),
               SEG_PIECE)
            return carry

        def small(i, carry):
            fn(pl.multiple_of(lo + i * SEG_ALIGN, SEG_ALIGN), pl.multiple_of(go + i * SEG_ALIGN, SEG_ALIGN),
               SEG_ALIGN)
            return carry

        lax.fori_loop(0, nbig, big, 0)
        lax.fori_loop(nbig * per_piece, ngrp, small, 0)


def _local_rows(lpos_ref):
    rows = lax.broadcasted_iota(jnp.int32, (LROWS, DISP_BLK), 0)
    return rows == lpos_ref[0, 0:1, :], rows == lpos_ref[0, 1:2, :]


def _gather_kernel(goff_ref, loff_ref, ngrp_ref, toff_ref, tgrp_ref, h_ref, lpos_ref,
                   xp_ref, xl_ref, zero_ref, sem):
    b = pl.program_id(0)
    nblk = pl.num_programs(0)
    slot = b % 2

    def copy(slot_, lrow, srow, nrows):
        return pltpu.make_async_copy(xl_ref.at[slot_, pl.ds(lrow, nrows), :],
                                     xp_ref.at[pl.ds(srow, nrows), :], sem.at[slot_])

    def each(blk, slot_, start):
        _for_each_segment_group(
            goff_ref, loff_ref, ngrp_ref, blk,
            lambda lrow, srow, n: copy(slot_, lrow, srow, n).start() if start else copy(slot_, lrow, srow, n).wait())

    def each_tail(fn):
        for e in range(N_EXP):
            def body(i, carry):
                row = pl.multiple_of(toff_ref[e] + i * SEG_ALIGN, SEG_ALIGN)
                fn(pltpu.make_async_copy(zero_ref, xp_ref.at[pl.ds(row, SEG_ALIGN), :], sem.at[2]))
                return carry

            lax.fori_loop(0, tgrp_ref[e], body, 0)

    @pl.when(b == 0)
    def _():
        zero_ref[...] = jnp.zeros_like(zero_ref)
        each_tail(lambda cp: cp.start())

    @pl.when(b >= 2)
    def _():
        each(b - 2, slot, start=False)

    hit1, hit2 = _local_rows(lpos_ref)
    sel = jnp.logical_or(hit1, hit2).astype(BF16)
    xl_ref[slot] = _dot(sel, h_ref[...]).astype(BF16)
    each(b, slot, start=True)

    @pl.when(b == nblk - 1)
    def _():
        each(b, slot, start=False)
        each_tail(lambda cp: cp.wait())

        @pl.when(b >= 1)
        def _():
            each(b - 1, 1 - slot, start=False)


def _gather_rows(plan, h):
    nrows = plan["ntiles"] * MOE_TILE
    nblk = h.shape[0] // DISP_BLK
    grid_spec = pltpu.PrefetchScalarGridSpec(
        num_scalar_prefetch=5,
        grid=(nblk,),
        in_specs=[
            pl.BlockSpec((DISP_BLK, D), lambda b, *_: (b, 0)),
            pl.BlockSpec((1, 2, DISP_BLK), lambda b, *_: (b, 0, 0)),
        ],
        out_specs=pl.BlockSpec(memory_space=pl.ANY),
        scratch_shapes=[pltpu.VMEM((2, LROWS, D), BF16), pltpu.VMEM((SEG_ALIGN, D), BF16),
                        pltpu.SemaphoreType.DMA((3,))],
    )
    return pl.pallas_call(
        _gather_kernel,
        grid_spec=grid_spec,
        out_shape=jax.ShapeDtypeStruct((nrows, D), BF16),
        compiler_params=_cparams(("arbitrary",)),
        name="moe_gather",
    )(*plan["segs"], *plan["tails"], h, plan["lpos"])


def _combine_kernel(goff_ref, loff_ref, ngrp_ref, yp_ref, lpos_ref, w_ref, x_ref, gt_ref, gf_ref,
                    op_ref, os_ref, yl_ref, sem, *, nb0):
    b = pl.program_id(0)
    nblk = pl.num_programs(0)
    slot = b % 2

    def copy(slot_, lrow, srow, nrows):
        return pltpu.make_async_copy(yp_ref.at[pl.ds(srow, nrows), :],
                                     yl_ref.at[slot_, pl.ds(lrow, nrows), :], sem.at[slot_])

    def each(blk, slot_, start):
        _for_each_segment_group(
            goff_ref, loff_ref, ngrp_ref, blk,
            lambda lrow, srow, n: copy(slot_, lrow, srow, n).start() if start else copy(slot_, lrow, srow, n).wait())

    @pl.when(b == 0)
    def _():
        yl_ref[...] = jnp.zeros_like(yl_ref)
        each(0, 0, start=True)

    @pl.when(b + 1 < nblk)
    def _():
        each(b + 1, 1 - slot, start=True)

    each(b, slot, start=False)

    hit1, hit2 = _local_rows(lpos_ref)
    wsel = jnp.where(hit1, w_ref[0, 0:1, :], 0.0) + jnp.where(hit2, w_ref[0, 1:2, :], 0.0)
    wh, wl = _split_bf16(wsel)
    tn = (((0,), (0,)), ((), ()))
    yl = yl_ref[slot]
    y = (lax.dot_general(wh, yl, tn, preferred_element_type=F32)
         + lax.dot_general(wl, yl, tn, preferred_element_type=F32))
    x2 = x_ref[...] + gt_ref[0] * y
    ms = jnp.mean(x2 * x2, axis=-1, keepdims=True)
    out = x2 * lax.rsqrt(ms + EPS) * gf_ref[...]

    @pl.when(b < nb0)
    def _():
        op_ref[...] = out

    @pl.when(b >= nb0)
    def _():
        os_ref[...] = out


def _combine(st, plan, yp, x, gt, gf):
    nblk = st.t // DISP_BLK
    nb0 = st.t0 // DISP_BLK
    blk3 = lambda b, *_: (b, 0, 0)
    grid_spec = pltpu.PrefetchScalarGridSpec(
        num_scalar_prefetch=3,
        grid=(nblk,),
        in_specs=[
            pl.BlockSpec(memory_space=pl.ANY),
            pl.BlockSpec((1, 2, DISP_BLK), blk3),
            pl.BlockSpec((1, 2, DISP_BLK), blk3),
            pl.BlockSpec((DISP_BLK, D), lambda b, *_: (b, 0)),
            pl.BlockSpec((1, 1, D), lambda b, *_: (st.batch_of_block(b, DISP_BLK), 0, 0)),
            pl.BlockSpec((1, D), lambda b, *_: (0, 0)),
        ],
        out_specs=[pl.BlockSpec((DISP_BLK, D), lambda b, *_: (jnp.minimum(b, nb0 - 1), 0)),
                   pl.BlockSpec((DISP_BLK, D), lambda b, *_: (jnp.maximum(b - nb0, 0), 0))],
        scratch_shapes=[pltpu.VMEM((2, LROWS, D), BF16), pltpu.SemaphoreType.DMA((2,))],
    )
    return pl.pallas_call(
        functools.partial(_combine_kernel, nb0=nb0),
        grid_spec=grid_spec,
        out_shape=[jax.ShapeDtypeStruct((st.t0, D), F32), jax.ShapeDtypeStruct((st.t - st.t0, D), F32)],
        compiler_params=_cparams(("arbitrary",)),
        name="moe_combine_norm",
    )(*plan["segs"], yp, plan["lpos"], plan["w12"], x, gt, gf)


def _pack_w_in(w):
    o = np.cumsum([0, GLA_DK, GLA_DK, GLA_DV, GLA_DV, RANK, RANK, FNET_W, D, D]).tolist()
    gates = jnp.pad(jnp.tile(w[:, o[4]:o[6]], (1, 3)), ((0, 0), (0, LANE - 6 * RANK)))
    return jnp.concatenate([w[:, o[0]:o[4]], gates, w[:, o[6]:o[9]]], axis=1).astype(BF16)


def _gate_weights(w_a2, offset):
    hi, lo = _split_bf16(w_a2)
    out = jnp.zeros((LANE, GLA_DK), BF16)
    for copy, part in enumerate((hi, hi, lo)):
        out = lax.dynamic_update_slice(out, part, (copy * 2 * RANK + offset, 0))
    return out


def kernel(x_prompt, x_sample, c_prompt, c_sample, norm1_g, norm2_g, w_ada, b_ada, w_in, w_af2, b_af, w_ab2, b_ab, gla_norm_g, w_o_gla, w_o_fnet, w_out, w_ff_gate, w_ff_up, w_ff_down, w_router, w_e_gate, w_e_up, w_e_down, final_norm_g):
    b, s, _ = x_prompt.shape
    db, ds, _ = x_sample.shape
    depth = w_in.shape[0]
    assert depth == 2 and s % max(GLA_BLK, TM, DISP_BLK, SUBLANE * N1) == 0
    assert ds % max(GLA_BLK, TM, DISP_BLK, SUBLANE * N1) == 0 and (b * s) % ds == 0
    st = _Stream(b, s, db, ds)

    xs = (x_prompt.reshape(b * s, D), x_sample.reshape(db * ds, D))
    c = jnp.concatenate([c_prompt, c_sample], axis=0)
    nb_pad = -(-st.nbatch // SUBLANE) * SUBLANE
    c = jnp.pad(c, ((0, nb_pad - st.nbatch), (0, 0)))
    mod = _modulation(c, w_ada, b_ada)

    def mod_piece(l, j):
        return mod[l, :, j * D:(j + 1) * D].reshape(nb_pad, 1, D)

    for l in range(depth):
        sh1, sc1, gt1, sh2, sc2, gt2 = [mod_piece(l, j) for j in range(N_MOD)]
        q, k, v, r, za, zf, ga, gb = _inproj(st, xs, norm1_g[l].reshape(1, D), sc1, sh1, _pack_w_in(w_in[l]))

        wa_f = _gate_weights(w_af2[l], 0)
        wa_b = _gate_weights(w_ab2[l], RANK)
        o_f = _gla_direction(st, False, q, k, v, za, wa_f, b_af[l].reshape(1, GLA_DK))
        o_gla = _gla_direction(st, True, q, k, v, za, wa_b, b_ab[l].reshape(1, GLA_DK),
                               extra=(o_f, r, gla_norm_g[l].reshape(1, DV)))

        yf = _fnet(st, zf)

        wog, wof, wout = w_o_gla[l].astype(BF16), w_o_fnet[l].astype(BF16), w_out[l].astype(BF16)
        g2 = norm2_g[l].reshape(1, D)
        if l % 2 == 0:
            x1, h2 = _merge(st, o_gla, yf, ga, gb, xs, gt1, g2, sc2, sh2, wog, wof, wout)
            xs = (_ffn_dense(st, h2, x1, gt2, w_ff_gate[l // 2].astype(BF16),
                             w_ff_up[l // 2].astype(BF16), w_ff_down[l // 2].astype(BF16)),)
        else:
            wr = jnp.pad(w_router[l // 2], ((0, 0), (0, LANE - N_EXP)))
            x1, h2, logits = _merge(st, o_gla, yf, ga, gb, xs, gt1, g2, sc2, sh2, wog, wof, wout, wr)
            info, cnt = _router(logits)
            plan = _dispatch_plan(info, cnt)
            xp = _gather_rows(plan, h2)
            yp = _ffn_experts(plan["tile_e"], plan["tile_v"], xp, w_e_gate[l // 2].astype(BF16),
                              w_e_up[l // 2].astype(BF16), w_e_down[l // 2].astype(BF16))
            y_prompt, y_sample = _combine(st, plan, yp, x1, gt2, final_norm_g.reshape(1, D))

    return y_prompt.reshape(b, s, D), y_sample.reshape(db, ds, D)
```

```python
import functools

import numpy as np
import jax
import jax.numpy as jnp
from jax import lax
from jax.experimental import pallas as pl
from jax.experimental.pallas import tpu as pltpu

D = 1024
HEADS = 4
DK = 128
DV = 256
GLA_DK = HEADS * DK
GLA_DV = HEADS * DV
RANK = 16
GATE_NORM = 16.0
CHUNK = 64
FN_GROUPS = 4
FN_GW = 128
FNET_W = FN_GROUPS * FN_GW
D_FF = 2816
N_EXP = 8
D_EXP = 3584
EPS = 1e-6
N_MOD = 6

LANE = 128
SUBLANE = 8
VMEM_LIMIT = 56 * 1024 * 1024
TM = 512
ROW_PART = 256
GLA_BLK = 1024
RING = 3
MOE_TILE = 512
DISP_BLK = 512
SEG_ALIGN = 16
SEG_PIECE = 128
LROWS = 2 * DISP_BLK + N_EXP * SEG_ALIGN
FF_CHUNK_DENSE = 256
FF_CHUNK_MOE = 256
N1 = 128
FN_ROWS = 16

F32 = jnp.float32
BF16 = jnp.bfloat16
_HI = lax.Precision.HIGHEST


def _cparams(sem):
    return pltpu.CompilerParams(dimension_semantics=sem, vmem_limit_bytes=VMEM_LIMIT)


def _dot(a, b):
    return jnp.dot(a, b, preferred_element_type=F32)


def _dot_hi(a, b):
    return jnp.dot(a, b, precision=_HI, preferred_element_type=F32)


def _split_bf16(x):
    hi = x.astype(BF16)
    lo = (x - hi.astype(F32)).astype(BF16)
    return hi, lo


def _dot3(a_hi, a_lo, b_hi, b_lo):
    return _dot(a_hi, b_hi) + (_dot(a_hi, b_lo) + _dot(a_lo, b_hi))


class _Stream:
    def __init__(self, b, s, db, ds):
        self.b, self.s, self.db, self.ds = b, s, db, ds
        self.t0 = b * s
        self.t = b * s + db * ds
        self.nbatch = b + db

    def batch_of_block(self, i, rows):
        nb0, p0, p1 = self.t0 // rows, self.s // rows, self.ds // rows
        return jnp.where(i < nb0, i // p0, self.b + (i - nb0) // p1)

    def block_in_seq(self, i, rows):
        nb0, p0, p1 = self.t0 // rows, self.s // rows, self.ds // rows
        return jnp.where(i < nb0, i % p0, (i - nb0) % p1), jnp.where(i < nb0, p0, p1)


def _mod_kernel(c_ref, w_ref, b_ref, o_ref):
    c = c_ref[...]
    cs = c * jax.nn.sigmoid(c)
    o_ref[0] = _dot_hi(cs, w_ref[0]) + b_ref[0]


def _modulation(c_pad, w_ada, b_ada):
    depth = w_ada.shape[0]
    nb = c_pad.shape[0]
    return pl.pallas_call(
        _mod_kernel,
        grid=(depth, N_MOD),
        in_specs=[
            pl.BlockSpec((nb, D), lambda l, j: (0, 0)),
            pl.BlockSpec((1, D, D), lambda l, j: (l, 0, j)),
            pl.BlockSpec((1, 1, D), lambda l, j: (l, 0, j)),
        ],
        out_specs=pl.BlockSpec((1, nb, D), lambda l, j: (l, 0, j)),
        out_shape=jax.ShapeDtypeStruct((depth, nb, N_MOD * D), F32),
        compiler_params=_cparams(("arbitrary", "arbitrary")),
        name="adaln_mod",
    )(c_pad, w_ada, b_ada.reshape(depth, 1, N_MOD * D))


def _rms_mod(x, g, sc, sh):
    ms = jnp.mean(x * x, axis=-1, keepdims=True)
    return (x * lax.rsqrt(ms + EPS) * g) * (1.0 + sc) + sh


_C_Q, _C_K, _C_V, _C_R, _C_A, _C_F, _C_GA, _C_GB, _C_END = np.cumsum(
    [0, GLA_DK, GLA_DK, GLA_DV, GLA_DV, LANE, FNET_W, D, D]).tolist()


def _stream_specs(st, xs, rows):
    width = xs[0].shape[1]
    if len(xs) == 1:
        return [pl.BlockSpec((rows, width), lambda i, *_: (i, 0))]
    nb0 = st.t0 // rows
    return [pl.BlockSpec((rows, width), lambda i, *_: (jnp.minimum(i, nb0 - 1), 0)),
            pl.BlockSpec((rows, width), lambda i, *_: (jnp.maximum(i - nb0, 0), 0))]


def _stream_block(st, x_refs, rows, part=slice(None)):
    if len(x_refs) == 1:
        return x_refs[0][part, :]
    return jnp.where(pl.program_id(0) < st.t0 // rows, x_refs[0][part, :], x_refs[1][part, :])


def _row_parts(rows):
    return [slice(p * ROW_PART, (p + 1) * ROW_PART) for p in range(rows // ROW_PART)]


def _inproj_kernel(*refs, st, nx):
    x_refs = refs[:nx]
    (g_ref, sc_ref, sh_ref, w_ref,
     q_ref, k_ref, v_ref, r_ref, a_ref, f_ref, ga_ref, gb_ref) = refs[nx:]
    for part in _row_parts(TM):
        hb = _rms_mod(_stream_block(st, x_refs, TM, part), g_ref[...], sc_ref[0], sh_ref[0]).astype(BF16)

        def proj(lo, hi):
            return _dot(hb, w_ref[:, lo:hi])

        q_ref[part, :] = (proj(_C_Q, _C_K) * (DK ** -0.5)).astype(BF16)
        k_ref[part, :] = proj(_C_K, _C_V).astype(BF16)
        v_ref[part, :] = proj(_C_V, _C_R).astype(BF16)
        zr = proj(_C_R, _C_A)
        r_ref[part, :] = (zr * jax.nn.sigmoid(zr)).astype(BF16)
        a_ref[part, :] = proj(_C_A, _C_F)
        f_ref[part, :] = proj(_C_F, _C_GA).astype(BF16)
        ga_ref[part, :] = jax.nn.sigmoid(proj(_C_GA, _C_GB)).astype(BF16)
        gb_ref[part, :] = jax.nn.sigmoid(proj(_C_GB, _C_END)).astype(BF16)


def _inproj(st, xs, g, sc, sh, w):
    t = st.t
    row = lambda i: (i, 0)
    bat = lambda i: (st.batch_of_block(i, TM), 0, 0)
    widths = [(GLA_DK, BF16), (GLA_DK, BF16), (GLA_DV, BF16), (GLA_DV, BF16),
              (LANE, F32), (FNET_W, BF16), (D, BF16), (D, BF16)]
    return pl.pallas_call(
        functools.partial(_inproj_kernel, st=st, nx=len(xs)),
        grid=(t // TM,),
        in_specs=_stream_specs(st, xs, TM) + [
            pl.BlockSpec((1, D), lambda i: (0, 0)),
            pl.BlockSpec((1, 1, D), bat),
            pl.BlockSpec((1, 1, D), bat),
            pl.BlockSpec((D, _C_END), lambda i: (0, 0)),
        ],
        out_specs=[pl.BlockSpec((TM, w_), row) for w_, _ in widths],
        out_shape=[jax.ShapeDtypeStruct((t, w_), dt) for w_, dt in widths],
        compiler_params=_cparams(("parallel",)),
        name="inproj",
    )(*xs, g, sc, sh, w)


def _log_sigmoid(x):
    return jnp.minimum(x, 0.0) - jnp.log(1.0 + jnp.exp(-jnp.abs(x)))


def _gla_kernel(*refs, bwd, st):
    if bwd:
        (q_hbm, k_hbm, v_hbm, za_hbm, wa_ref, ba_ref, of_ref, r_ref, ng_ref,
         o_ref, s_scr, qi_scr, ki_scr, qin_scr, kout_scr, att_scr, kv_scr, dec_scr,
         q_buf, k_buf, v_buf, za_buf, ring_sem) = refs
    else:
        (q_hbm, k_hbm, v_hbm, za_hbm, wa_ref, ba_ref,
         o_ref, s_scr, qi_scr, ki_scr, qin_scr, kout_scr, att_scr, kv_scr, dec_scr,
         q_buf, k_buf, v_buf, za_buf, ring_sem) = refs
    i = pl.program_id(0)
    nblk = pl.num_programs(0)
    blk = (nblk - 1 - i) if bwd else i

    streams = ((q_hbm, q_buf), (k_hbm, k_buf), (v_hbm, v_buf), (za_hbm, za_buf))

    def fetch(step, start):
        slot_ = step % RING
        row0 = pl.multiple_of(((nblk - 1 - step) if bwd else step) * GLA_BLK, GLA_BLK)
        for n, (hbm, buf) in enumerate(streams):
            cp = pltpu.make_async_copy(hbm.at[pl.ds(row0, GLA_BLK), :], buf.at[slot_],
                                       ring_sem.at[n * RING + slot_])
            if start:
                cp.start()
            else:
                cp.wait()

    @pl.when(i == 0)
    def _():
        fetch(0, True)

        @pl.when(nblk > 1)
        def _():
            fetch(1, True)

    @pl.when(i + 2 < nblk)
    def _():
        fetch(i + 2, True)

    fetch(i, False)
    q_ref, k_ref, v_ref, za_ref = (buf.at[i % RING] for _, buf in streams)
    local, per = st.block_in_seq(blk, GLA_BLK)
    at_boundary = (local == per - 1) if bwd else (local == 0)

    @pl.when(at_boundary)
    def _():
        s_scr[...] = jnp.zeros_like(s_scr)

    nch = GLA_BLK // CHUNK
    rr = lax.broadcasted_iota(jnp.int32, (CHUNK, CHUNK), 0)
    cc = lax.broadcasted_iota(jnp.int32, (CHUNK, CHUNK), 1)
    keep = (rr <= cc) if bwd else (rr >= cc)
    tri = keep.astype(BF16)
    last = 0 if bwd else CHUNK - 1
    mid = CHUNK // 2 if bwd else CHUNK // 2 - 1

    lane = lax.broadcasted_iota(jnp.int32, (CHUNK, LANE), 1)
    low_part_lanes = jnp.logical_and(lane >= 2 * RANK, lane < 4 * RANK)

    log_decay = {}

    def gate_chunk(c):
        rows = slice(c * CHUNK, (c + 1) * CHUNK)
        zh, zl = _split_bf16(za_ref[rows, :])
        pre = _dot(jnp.where(low_part_lanes, zl, zh), wa_ref[...]) + ba_ref[...]
        g = _log_sigmoid(pre) * (1.0 / GATE_NORM)
        log_decay[c] = _split_bf16(g)

    def scale_chunk(c):
        rows = slice(c * CHUNK, (c + 1) * CHUNK)
        g0, g1 = log_decay.pop(c)
        b = _dot(tri, g0) + _dot(tri, g1)
        b_last = b[last:last + 1, :]
        b_ref = b[mid:mid + 1, :]
        dec_scr[c] = jnp.broadcast_to(jnp.exp(b_last), (LANE, GLA_DK)).T
        q = q_ref[rows, :].astype(F32)
        k = k_ref[rows, :].astype(F32)
        qi_scr[rows, :] = (q * jnp.exp(b - b_ref)).astype(BF16)
        ki_scr[rows, :] = (k * jnp.exp(b_ref - b)).astype(BF16)
        qin_scr[rows, :] = (q * jnp.exp(b)).astype(BF16)
        kout_scr[rows, :] = (k * jnp.exp(b_last - b)).astype(BF16)

    def stateless_products(c):
        rows = slice(c * CHUNK, (c + 1) * CHUNK)
        for h in range(HEADS):
            ks = slice(h * DK, (h + 1) * DK)
            att = lax.dot_general(qi_scr[rows, ks], ki_scr[rows, ks], (((1,), (1,)), ((), ())),
                                  preferred_element_type=F32)
            att_scr[c * HEADS + h] = jnp.where(keep, att, 0.0).astype(BF16)
            kv_scr[c * HEADS + h] = lax.dot_general(
                kout_scr[rows, ks], v_ref[rows, h * DV:(h + 1) * DV], (((0,), (0,)), ((), ())),
                preferred_element_type=F32)

    state = [s_scr[h] for h in range(HEADS)]

    def scan_chunk(c):
        rows = slice(c * CHUNK, (c + 1) * CHUNK)
        for h in range(HEADS):
            ks = slice(h * DK, (h + 1) * DK)
            vs = slice(h * DV, (h + 1) * DV)
            o = _dot(jnp.concatenate([qin_scr[rows, ks], att_scr[c * HEADS + h]], axis=1),
                     jnp.concatenate([state[h].astype(BF16), v_ref[rows, vs]], axis=0))
            decay = dec_scr[c, ks, :]
            state[h] = state[h] * jnp.concatenate([decay] * (DV // LANE), axis=1) + kv_scr[c * HEADS + h]
            if bwd:
                o = o + of_ref[rows, vs]
                o = o * lax.rsqrt(jnp.mean(o * o, axis=-1, keepdims=True) + EPS) * ng_ref[...]
                o_ref[rows, vs] = (o * r_ref[rows, vs].astype(F32)).astype(o_ref.dtype)
            else:
                o_ref[rows, vs] = o

    order = list(reversed(range(nch))) if bwd else list(range(nch))
    stages = (gate_chunk, scale_chunk, stateless_products, scan_chunk)
    for step in range(nch + len(stages) - 1):
        for lag, stage in enumerate(stages):
            if 0 <= step - lag < nch:
                stage(order[step - lag])
    for h in range(HEADS):
        s_scr[h] = state[h]


def _gla_direction(st, bwd, q, k, v, za, wa, ba, extra=()):
    t = st.t
    nblk = t // GLA_BLK
    row = (lambda i: (nblk - 1 - i, 0)) if bwd else (lambda i: (i, 0))
    const = lambda i: (0, 0)
    in_specs = [pl.BlockSpec(memory_space=pl.ANY)] * 4 + [
        pl.BlockSpec((LANE, GLA_DK), const),
        pl.BlockSpec((1, GLA_DK), const),
    ]
    if bwd:
        in_specs += [pl.BlockSpec((GLA_BLK, GLA_DV), row), pl.BlockSpec((GLA_BLK, GLA_DV), row),
                     pl.BlockSpec((1, DV), const)]
    scaled = pltpu.VMEM((GLA_BLK, GLA_DK), BF16)
    nch = GLA_BLK // CHUNK
    return pl.pallas_call(
        functools.partial(_gla_kernel, bwd=bwd, st=st),
        grid=(nblk,),
        in_specs=in_specs,
        out_specs=pl.BlockSpec((GLA_BLK, GLA_DV), row),
        out_shape=jax.ShapeDtypeStruct((t, GLA_DV), BF16 if bwd else F32),
        scratch_shapes=[pltpu.VMEM((HEADS, DK, DV), F32), scaled, scaled, scaled, scaled,
                        pltpu.VMEM((nch * HEADS, CHUNK, CHUNK), BF16),
                        pltpu.VMEM((nch * HEADS, DK, DV), F32),
                        pltpu.VMEM((nch, GLA_DK, LANE), F32),
                        pltpu.VMEM((RING, GLA_BLK, GLA_DK), BF16), pltpu.VMEM((RING, GLA_BLK, GLA_DK), BF16),
                        pltpu.VMEM((RING, GLA_BLK, GLA_DV), BF16), pltpu.VMEM((RING, GLA_BLK, LANE), F32),
                        pltpu.SemaphoreType.DMA((4 * RING,))],
        compiler_params=_cparams(("arbitrary",)),
        name="gla_bwd" if bwd else "gla_fwd",
    )(q, k, v, za, wa, ba, *extra)


def _dft_cos_sin(n):
    idx = np.arange(n, dtype=np.float64)
    ang = 2.0 * np.pi * np.outer(idx, idx) / n
    return np.cos(ang) / np.sqrt(n), np.sin(ang) / np.sqrt(n)


def _fnet_stage1_kernel(x_ref, m_ref, y_ref):
    n2 = x_ref.shape[0]
    xf = x_ref[...].astype(F32)
    for half in range(FN_ROWS // SUBLANE):
        rows = slice(half * SUBLANE, (half + 1) * SUBLANE)
        x = xf[:, rows, :].reshape(n2 * SUBLANE, FNET_W).astype(BF16)
        y = _dot(m_ref[...], x).reshape(2, n2, SUBLANE, FNET_W)
        y_ref[:, :, rows, :] = y.astype(BF16)


def _fnet_stage1(zf3, bn, n2, seq0, m1):
    return pl.pallas_call(
        _fnet_stage1_kernel,
        grid=(bn, N1 // FN_ROWS),
        in_specs=[
            pl.BlockSpec((n2, FN_ROWS, FNET_W), lambda b, j: (seq0 + b, j, 0)),
            pl.BlockSpec(m1.shape, lambda b, j: (0, 0)),
        ],
        out_specs=pl.BlockSpec((2, n2, FN_ROWS, FNET_W), lambda b, j: (0, b, j, 0)),
        out_shape=jax.ShapeDtypeStruct((2, bn * n2, N1, FNET_W), BF16),
        compiler_params=_cparams(("parallel", "parallel")),
        name="fnet_stage1",
    )(zf3, m1)


def _fnet_stage2_kernel(y_ref, tc_ref, ts_ref, m2_ref, m3_ref, o_ref, z_scr):
    for g in range(SUBLANE):
        rows = slice(g * N1, (g + 1) * N1)
        yr = y_ref[0, rows, :].astype(F32)
        yi = y_ref[1, rows, :].astype(F32)
        tc = tc_ref[g]
        ts = ts_ref[g]
        stack = jnp.concatenate([yr * tc + yi * ts, yi * tc - yr * ts], axis=0)
        z = _dot(m2_ref[...], stack.astype(BF16))
        z_scr[rows, 0:FNET_W] = z[:N1].astype(BF16)
        z_scr[rows, FNET_W:2 * FNET_W] = z[N1:].astype(BF16)
    out = _dot(z_scr[...], m3_ref[...])
    for g in range(SUBLANE):
        o_ref[:, g, :] = out[g * N1:(g + 1) * N1]


def _fnet_stage2(y2, bn, n2, tc, ts, m2, m3):
    rows = SUBLANE * N1
    per_seq = n2 // SUBLANE
    const = lambda b, j: (0, 0)
    return pl.pallas_call(
        _fnet_stage2_kernel,
        grid=(bn, per_seq),
        in_specs=[
            pl.BlockSpec((2, rows, FNET_W), lambda b, j: (0, b * per_seq + j, 0)),
            pl.BlockSpec((SUBLANE, N1, 1), lambda b, j: (j, 0, 0)),
            pl.BlockSpec((SUBLANE, N1, 1), lambda b, j: (j, 0, 0)),
            pl.BlockSpec((2 * N1, 2 * N1), const),
            pl.BlockSpec((2 * FNET_W, FNET_W), const),
        ],
        out_specs=pl.BlockSpec((N1, SUBLANE, FNET_W), lambda b, j: (b, j, 0)),
        out_shape=jax.ShapeDtypeStruct((bn * N1, n2, FNET_W), F32),
        scratch_shapes=[pltpu.VMEM((rows, 2 * FNET_W), BF16)],
        compiler_params=_cparams(("parallel", "parallel")),
        name="fnet_stage2",
    )(y2, tc, ts, m2, m3).reshape(bn * n2 * N1, FNET_W)


def _fnet(st, zf):
    as_bf16 = lambda m: jnp.asarray(np.asarray(m, np.float32).astype(BF16))
    c1, s1 = _dft_cos_sin(N1)
    m2 = as_bf16(np.block([[c1, s1], [-s1, c1]]))
    cc, sc = _dft_cos_sin(FN_GW)
    eye = np.eye(FN_GROUPS)
    m3 = as_bf16(np.concatenate([np.kron(eye, cc), np.kron(eye, sc)], axis=0))
    zf3 = zf.reshape(st.t // N1, N1, FNET_W)
    outs = []
    for bn, s, row0 in ((st.b, st.s, 0), (st.db, st.ds, st.t0)):
        n2 = s // N1
        c2, s2 = _dft_cos_sin(n2)
        m1 = as_bf16(np.kron(np.concatenate([c2, -s2], axis=0), np.eye(SUBLANE)))
        ang = 2.0 * np.pi * np.outer(np.arange(n2), np.arange(N1)) / s
        tc = jnp.asarray(np.cos(ang)[:, :, None], F32)
        ts = jnp.asarray(np.sin(ang)[:, :, None], F32)
        y = _fnet_stage1(zf3, bn, n2, row0 // s, m1)
        outs.append(_fnet_stage2(y.reshape(2, bn * s, FNET_W), bn, n2, tc, ts, m2, m3))
    return tuple(outs)


def _merge_kernel(*refs, st, nx, router):
    x_refs = refs[:nx]
    yf_refs = refs[nx:nx + 2]
    (og_ref, ga_ref, gb_ref, gt_ref, g2_ref, sc_ref, sh_ref,
     wog_ref, wof_ref, wout_ref) = refs[nx + 2:nx + 12]
    rest = refs[nx + 12:]
    if router:
        wrh_ref, wrl_ref, x1_ref, h2_ref, lg_ref = rest
    else:
        x1_ref, h2_ref = rest
    for part in _row_parts(TM):
        ya = _dot(og_ref[part, :], wog_ref[...])
        yb = _dot(_stream_block(st, yf_refs, TM, part).astype(BF16), wof_ref[...])
        m = ga_ref[part, :].astype(F32) * ya + gb_ref[part, :].astype(F32) * yb
        x1 = _stream_block(st, x_refs, TM, part) + gt_ref[0] * _dot(m.astype(BF16), wout_ref[...])
        h2 = _rms_mod(x1, g2_ref[...], sc_ref[0], sh_ref[0])
        if router:
            lg_ref[part, :] = _dot3(*_split_bf16(h2), wrh_ref[...], wrl_ref[...])
        x1_ref[part, :] = x1
        h2_ref[part, :] = h2.astype(BF16)


def _merge(st, og, yfs, ga, gb, xs, gt, g2, sc, sh, wog, wof, wout, wr=None):
    t = st.t
    row = lambda i: (i, 0)
    bat = lambda i: (st.batch_of_block(i, TM), 0, 0)
    const = lambda i: (0, 0)
    in_specs = _stream_specs(st, xs, TM) + _stream_specs(st, yfs, TM) + [
        pl.BlockSpec((TM, GLA_DV), row),
        pl.BlockSpec((TM, D), row), pl.BlockSpec((TM, D), row),
        pl.BlockSpec((1, 1, D), bat), pl.BlockSpec((1, D), const),
        pl.BlockSpec((1, 1, D), bat), pl.BlockSpec((1, 1, D), bat),
        pl.BlockSpec((GLA_DV, D), const), pl.BlockSpec((FNET_W, D), const), pl.BlockSpec((D, D), const),
    ]
    out_specs = [pl.BlockSpec((TM, D), row), pl.BlockSpec((TM, D), row)]
    out_shape = [jax.ShapeDtypeStruct((t, D), F32), jax.ShapeDtypeStruct((t, D), BF16)]
    args = [*xs, *yfs, og, ga, gb, gt, g2, sc, sh, wog, wof, wout]
    if wr is not None:
        in_specs += [pl.BlockSpec((D, LANE), const), pl.BlockSpec((D, LANE), const)]
        out_specs.append(pl.BlockSpec((TM, LANE), row))
        out_shape.append(jax.ShapeDtypeStruct((t, LANE), F32))
        args += list(_split_bf16(wr))
    return pl.pallas_call(
        functools.partial(_merge_kernel, st=st, nx=len(xs), router=wr is not None),
        grid=(t // TM,),
        in_specs=in_specs, out_specs=out_specs, out_shape=out_shape,
        compiler_params=_cparams(("parallel",)),
        name="merge_router" if wr is not None else "merge",
    )(*args)


def _swiglu_into(acc_ref, h, wg_ref, wu_ref, wd_ref, width, chunk, lead=()):
    for c in range(width // chunk):
        cols = slice(c * chunk, (c + 1) * chunk)
        g = _dot(h, wg_ref[lead + (slice(None), cols)])
        u = _dot(h, wu_ref[lead + (slice(None), cols)])
        a = (g * jax.nn.sigmoid(g) * u).astype(BF16)
        part = _dot(a, wd_ref[lead + (cols, slice(None))])
        if c == 0:
            acc_ref[...] = part
        else:
            acc_ref[...] += part


def _ffn_dense_kernel(h_ref, x_ref, gt_ref, wg_ref, wu_ref, wd_ref, o_ref, acc_ref):
    _swiglu_into(acc_ref, h_ref[...], wg_ref, wu_ref, wd_ref, D_FF, FF_CHUNK_DENSE)
    o_ref[...] = x_ref[...] + gt_ref[0] * acc_ref[...]


def _ffn_dense(st, h, x, gt, wg, wu, wd):
    t = st.t
    row = lambda i: (i, 0)
    const = lambda i: (0, 0)
    once = pl.Buffered(1)
    return pl.pallas_call(
        _ffn_dense_kernel,
        grid=(t // TM,),
        in_specs=[
            pl.BlockSpec((TM, D), row), pl.BlockSpec((TM, D), row),
            pl.BlockSpec((1, 1, D), lambda i: (st.batch_of_block(i, TM), 0, 0)),
            pl.BlockSpec((D, D_FF), const, pipeline_mode=once),
            pl.BlockSpec((D, D_FF), const, pipeline_mode=once),
            pl.BlockSpec((D_FF, D), const, pipeline_mode=once),
        ],
        out_specs=pl.BlockSpec((TM, D), row),
        out_shape=jax.ShapeDtypeStruct((t, D), F32),
        scratch_shapes=[pltpu.VMEM((TM, D), F32)],
        compiler_params=_cparams(("parallel",)),
        name="ffn_dense",
    )(h, x, gt, wg, wu, wd)


def _ffn_expert_kernel(te_ref, tv_ref, x_ref, wg_ref, wu_ref, wd_ref, o_ref, acc_ref):
    k = pl.program_id(0)

    @pl.when(tv_ref[k] == 1)
    def _():
        _swiglu_into(acc_ref, x_ref[...], wg_ref, wu_ref, wd_ref, D_EXP, FF_CHUNK_MOE, lead=(0,))
        o_ref[...] = acc_ref[...].astype(o_ref.dtype)

    @pl.when(tv_ref[k] == 0)
    def _():
        o_ref[...] = jnp.zeros_like(o_ref)


def _ffn_experts(tile_e, tile_v, xp, wg, wu, wd):
    ntiles = xp.shape[0] // MOE_TILE
    once = pl.Buffered(1)
    wmap = lambda k, te, tv: (te[k], 0, 0)
    xmap = lambda k, te, tv: (k * tv[k], 0)
    grid_spec = pltpu.PrefetchScalarGridSpec(
        num_scalar_prefetch=2,
        grid=(ntiles,),
        in_specs=[
            pl.BlockSpec((MOE_TILE, D), xmap),
            pl.BlockSpec((1, D, D_EXP), wmap, pipeline_mode=once),
            pl.BlockSpec((1, D, D_EXP), wmap, pipeline_mode=once),
            pl.BlockSpec((1, D_EXP, D), wmap, pipeline_mode=once),
        ],
        out_specs=pl.BlockSpec((MOE_TILE, D), lambda k, te, tv: (k, 0)),
        scratch_shapes=[pltpu.VMEM((MOE_TILE, D), F32)],
    )
    return pl.pallas_call(
        _ffn_expert_kernel,
        grid_spec=grid_spec,
        out_shape=jax.ShapeDtypeStruct(xp.shape, BF16),
        compiler_params=_cparams(("arbitrary",)),
        name="ffn_experts",
    )(tile_e, tile_v, xp, wg, wu, wd)


def _router_kernel(lg_ref, info_ref, cnt_ref):
    lane = lax.broadcasted_iota(jnp.int32, (DISP_BLK, LANE), 1).astype(F32)
    neg = jnp.float32(-jnp.inf)
    lg = jnp.where(lane < N_EXP, lg_ref[...], neg)
    v1 = jnp.max(lg, axis=1, keepdims=True)
    i1 = jnp.min(jnp.where(lg == v1, lane, float(LANE)), axis=1, keepdims=True)
    oh1 = lane == i1
    lg2 = jnp.where(oh1, neg, lg)
    v2 = jnp.max(lg2, axis=1, keepdims=True)
    i2 = jnp.min(jnp.where(lg2 == v2, lane, float(LANE)), axis=1, keepdims=True)
    oh2 = lane == i2
    e = jnp.exp(v2 - v1)
    w1 = 1.0 / (1.0 + e)
    w2 = e / (1.0 + e)
    oh = oh1.astype(F32) + oh2.astype(F32)
    rr = lax.broadcasted_iota(jnp.int32, (DISP_BLK, DISP_BLK), 0)
    cc = lax.broadcasted_iota(jnp.int32, (DISP_BLK, DISP_BLK), 1)
    earlier = (cc < rr).astype(BF16)
    before = _dot(earlier, oh.astype(BF16))
    pos1 = jnp.sum(jnp.where(oh1, before, 0.0), axis=1, keepdims=True)
    pos2 = jnp.sum(jnp.where(oh2, before, 0.0), axis=1, keepdims=True)
    blk_cnt = jnp.sum(oh, axis=0, keepdims=True)
    cnt_ref[0] = jnp.broadcast_to(blk_cnt, (SUBLANE, LANE))
    packed = jnp.where(lane == 0, i1,
             jnp.where(lane == 1, i2,
             jnp.where(lane == 2, w1,
             jnp.where(lane == 3, w2,
             jnp.where(lane == 4, pos1,
             jnp.where(lane == 5, pos2, 0.0))))))
    info_ref[0] = packed.T[0:SUBLANE, :]


def _router(logits):
    t = logits.shape[0]
    nblk = t // DISP_BLK
    return pl.pallas_call(
        _router_kernel,
        grid=(nblk,),
        in_specs=[pl.BlockSpec((DISP_BLK, LANE), lambda i: (i, 0))],
        out_specs=[pl.BlockSpec((1, SUBLANE, DISP_BLK), lambda i: (i, 0, 0)),
                   pl.BlockSpec((1, SUBLANE, LANE), lambda i: (i, 0, 0))],
        out_shape=[jax.ShapeDtypeStruct((nblk, SUBLANE, DISP_BLK), F32),
                   jax.ShapeDtypeStruct((nblk, SUBLANE, LANE), F32)],
        compiler_params=_cparams(("parallel",)),
        name="router",
    )(logits)


def _dispatch_plan(info, cnt):
    nblk = info.shape[0]
    t = nblk * DISP_BLK
    max_rows = 2 * t + nblk * N_EXP * (SEG_ALIGN - 1) + N_EXP * (MOE_TILE - 1)
    ntiles = -(-max_rows // MOE_TILE)
    e12 = info[:, 0:2, :].astype(jnp.int32)
    w12 = info[:, 2:4, :]
    rank12 = info[:, 4:6, :].astype(jnp.int32)
    bc = cnt[:, 0, :N_EXP].astype(jnp.int32)
    pc = (bc + SEG_ALIGN - 1) // SEG_ALIGN * SEG_ALIGN
    loff = jnp.cumsum(pc, axis=1) - pc
    etot = jnp.sum(pc, axis=0)
    epad = (etot + MOE_TILE - 1) // MOE_TILE * MOE_TILE
    eend = jnp.cumsum(epad)
    goff = (eend - epad)[None, :] + jnp.cumsum(pc, axis=0) - pc
    experts = jnp.arange(N_EXP, dtype=jnp.int32)
    lpos = jnp.sum(jnp.where(e12[..., None] == experts, loff[:, None, None, :], 0), axis=-1) + rank12
    tile_row = jnp.arange(ntiles, dtype=jnp.int32) * MOE_TILE
    tile_e = jnp.minimum(jnp.sum((eend[None, :] <= tile_row[:, None]).astype(jnp.int32), axis=1),
                         N_EXP - 1)
    tile_v = (tile_row < eend[-1]).astype(jnp.int32)
    tail_end = eend.at[N_EXP - 1].set(ntiles * MOE_TILE)
    tail_off = eend - epad + etot
    return dict(lpos=lpos, w12=w12, tile_e=tile_e, tile_v=tile_v, ntiles=ntiles,
                segs=(goff.reshape(-1), loff.reshape(-1), (pc // SEG_ALIGN).reshape(-1)),
                tails=(tail_off, (tail_end - tail_off) // SEG_ALIGN))


def _for_each_segment_group(goff_ref, loff_ref, ngrp_ref, blk, fn):
    per_piece = SEG_PIECE // SEG_ALIGN
    for e in range(N_EXP):
        idx = blk * N_EXP + e
        lo, go, ngrp = loff_ref[idx], goff_ref[idx], ngrp_ref[idx]
        nbig = ngrp // per_piece

        def big(i, carry):
            fn(pl.multiple_of(lo + i * SEG_PIECE, SEG_ALIGN), pl.multiple_of(go + i * SEG_PIECE, SEG_ALIGN),
               SEG_PIECE)
            return carry

        def small(i, carry):
            fn(pl.multiple_of(lo + i * SEG_ALIGN, SEG_ALIGN), pl.multiple_of(go + i * SEG_ALIGN, SEG_ALIGN),
               SEG_ALIGN)
            return carry

        lax.fori_loop(0, nbig, big, 0)
        lax.fori_loop(nbig * per_piece, ngrp, small, 0)


def _local_rows(lpos_ref):
    rows = lax.broadcasted_iota(jnp.int32, (LROWS, DISP_BLK), 0)
    return rows == lpos_ref[0, 0:1, :], rows == lpos_ref[0, 1:2, :]


def _gather_kernel(goff_ref, loff_ref, ngrp_ref, toff_ref, tgrp_ref, h_ref, lpos_ref,
                   xp_ref, xl_ref, zero_ref, sem):
    b = pl.program_id(0)
    nblk = pl.num_programs(0)
    slot = b % 2

    def copy(slot_, lrow, srow, nrows):
        return pltpu.make_async_copy(xl_ref.at[slot_, pl.ds(lrow, nrows), :],
                                     xp_ref.at[pl.ds(srow, nrows), :], sem.at[slot_])

    def each(blk, slot_, start):
        _for_each_segment_group(
            goff_ref, loff_ref, ngrp_ref, blk,
            lambda lrow, srow, n: copy(slot_, lrow, srow, n).start() if start else copy(slot_, lrow, srow, n).wait())

    def each_tail(fn):
        for e in range(N_EXP):
            def body(i, carry):
                row = pl.multiple_of(toff_ref[e] + i * SEG_ALIGN, SEG_ALIGN)
                fn(pltpu.make_async_copy(zero_ref, xp_ref.at[pl.ds(row, SEG_ALIGN), :], sem.at[2]))
                return carry

            lax.fori_loop(0, tgrp_ref[e], body, 0)

    @pl.when(b == 0)
    def _():
        zero_ref[...] = jnp.zeros_like(zero_ref)
        each_tail(lambda cp: cp.start())

    @pl.when(b >= 2)
    def _():
        each(b - 2, slot, start=False)

    hit1, hit2 = _local_rows(lpos_ref)
    sel = jnp.logical_or(hit1, hit2).astype(BF16)
    xl_ref[slot] = _dot(sel, h_ref[...]).astype(BF16)
    each(b, slot, start=True)

    @pl.when(b == nblk - 1)
    def _():
        each(b, slot, start=False)
        each_tail(lambda cp: cp.wait())

        @pl.when(b >= 1)
        def _():
            each(b - 1, 1 - slot, start=False)


def _gather_rows(plan, h):
    nrows = plan["ntiles"] * MOE_TILE
    nblk = h.shape[0] // DISP_BLK
    grid_spec = pltpu.PrefetchScalarGridSpec(
        num_scalar_prefetch=5,
        grid=(nblk,),
        in_specs=[
            pl.BlockSpec((DISP_BLK, D), lambda b, *_: (b, 0)),
            pl.BlockSpec((1, 2, DISP_BLK), lambda b, *_: (b, 0, 0)),
        ],
        out_specs=pl.BlockSpec(memory_space=pl.ANY),
        scratch_shapes=[pltpu.VMEM((2, LROWS, D), BF16), pltpu.VMEM((SEG_ALIGN, D), BF16),
                        pltpu.SemaphoreType.DMA((3,))],
    )
    return pl.pallas_call(
        _gather_kernel,
        grid_spec=grid_spec,
        out_shape=jax.ShapeDtypeStruct((nrows, D), BF16),
        compiler_params=_cparams(("arbitrary",)),
        name="moe_gather",
    )(*plan["segs"], *plan["tails"], h, plan["lpos"])


def _combine_kernel(goff_ref, loff_ref, ngrp_ref, yp_ref, lpos_ref, w_ref, x_ref, gt_ref, gf_ref,
                    op_ref, os_ref, yl_ref, sem, *, nb0):
    b = pl.program_id(0)
    nblk = pl.num_programs(0)
    slot = b % 2

    def copy(slot_, lrow, srow, nrows):
        return pltpu.make_async_copy(yp_ref.at[pl.ds(srow, nrows), :],
                                     yl_ref.at[slot_, pl.ds(lrow, nrows), :], sem.at[slot_])

    def each(blk, slot_, start):
        _for_each_segment_group(
            goff_ref, loff_ref, ngrp_ref, blk,
            lambda lrow, srow, n: copy(slot_, lrow, srow, n).start() if start else copy(slot_, lrow, srow, n).wait())

    @pl.when(b == 0)
    def _():
        yl_ref[...] = jnp.zeros_like(yl_ref)
        each(0, 0, start=True)

    @pl.when(b + 1 < nblk)
    def _():
        each(b + 1, 1 - slot, start=True)

    each(b, slot, start=False)

    hit1, hit2 = _local_rows(lpos_ref)
    wsel = jnp.where(hit1, w_ref[0, 0:1, :], 0.0) + jnp.where(hit2, w_ref[0, 1:2, :], 0.0)
    wh, wl = _split_bf16(wsel)
    tn = (((0,), (0,)), ((), ()))
    yl = yl_ref[slot]
    y = (lax.dot_general(wh, yl, tn, preferred_element_type=F32)
         + lax.dot_general(wl, yl, tn, preferred_element_type=F32))
    x2 = x_ref[...] + gt_ref[0] * y
    ms = jnp.mean(x2 * x2, axis=-1, keepdims=True)
    out = x2 * lax.rsqrt(ms + EPS) * gf_ref[...]

    @pl.when(b < nb0)
    def _():
        op_ref[...] = out

    @pl.when(b >= nb0)
    def _():
        os_ref[...] = out


def _combine(st, plan, yp, x, gt, gf):
    nblk = st.t // DISP_BLK
    nb0 = st.t0 // DISP_BLK
    blk3 = lambda b, *_: (b, 0, 0)
    grid_spec = pltpu.PrefetchScalarGridSpec(
        num_scalar_prefetch=3,
        grid=(nblk,),
        in_specs=[
            pl.BlockSpec(memory_space=pl.ANY),
            pl.BlockSpec((1, 2, DISP_BLK), blk3),
            pl.BlockSpec((1, 2, DISP_BLK), blk3),
            pl.BlockSpec((DISP_BLK, D), lambda b, *_: (b, 0)),
            pl.BlockSpec((1, 1, D), lambda b, *_: (st.batch_of_block(b, DISP_BLK), 0, 0)),
            pl.BlockSpec((1, D), lambda b, *_: (0, 0)),
        ],
        out_specs=[pl.BlockSpec((DISP_BLK, D), lambda b, *_: (jnp.minimum(b, nb0 - 1), 0)),
                   pl.BlockSpec((DISP_BLK, D), lambda b, *_: (jnp.maximum(b - nb0, 0), 0))],
        scratch_shapes=[pltpu.VMEM((2, LROWS, D), BF16), pltpu.SemaphoreType.DMA((2,))],
    )
    return pl.pallas_call(
        functools.partial(_combine_kernel, nb0=nb0),
        grid_spec=grid_spec,
        out_shape=[jax.ShapeDtypeStruct((st.t0, D), F32), jax.ShapeDtypeStruct((st.t - st.t0, D), F32)],
        compiler_params=_cparams(("arbitrary",)),
        name="moe_combine_norm",
    )(*plan["segs"], yp, plan["lpos"], plan["w12"], x, gt, gf)


def _pack_w_in(w):
    o = np.cumsum([0, GLA_DK, GLA_DK, GLA_DV, GLA_DV, RANK, RANK, FNET_W, D, D]).tolist()
    gates = jnp.pad(jnp.tile(w[:, o[4]:o[6]], (1, 3)), ((0, 0), (0, LANE - 6 * RANK)))
    return jnp.concatenate([w[:, o[0]:o[4]], gates, w[:, o[6]:o[9]]], axis=1).astype(BF16)


def _gate_weights(w_a2, offset):
    hi, lo = _split_bf16(w_a2)
    out = jnp.zeros((LANE, GLA_DK), BF16)
    for copy, part in enumerate((hi, hi, lo)):
        out = lax.dynamic_update_slice(out, part, (copy * 2 * RANK + offset, 0))
    return out


def kernel(x_prompt, x_sample, c_prompt, c_sample, norm1_g, norm2_g, w_ada, b_ada, w_in, w_af2, b_af, w_ab2, b_ab, gla_norm_g, w_o_gla, w_o_fnet, w_out, w_ff_gate, w_ff_up, w_ff_down, w_router, w_e_gate, w_e_up, w_e_down, final_norm_g):
    b, s, _ = x_prompt.shape
    db, ds, _ = x_sample.shape
    depth = w_in.shape[0]
    assert depth == 2 and s % max(GLA_BLK, TM, DISP_BLK, SUBLANE * N1) == 0
    assert ds % max(GLA_BLK, TM, DISP_BLK, SUBLANE * N1) == 0 and (b * s) % ds == 0
    st = _Stream(b, s, db, ds)

    xs = (x_prompt.reshape(b * s, D), x_sample.reshape(db * ds, D))
    c = jnp.concatenate([c_prompt, c_sample], axis=0)
    nb_pad = -(-st.nbatch // SUBLANE) * SUBLANE
    c = jnp.pad(c, ((0, nb_pad - st.nbatch), (0, 0)))
    mod = _modulation(c, w_ada, b_ada)

    def mod_piece(l, j):
        return mod[l, :, j * D:(j + 1) * D].reshape(nb_pad, 1, D)

    for l in range(depth):
        sh1, sc1, gt1, sh2, sc2, gt2 = [mod_piece(l, j) for j in range(N_MOD)]
        q, k, v, r, za, zf, ga, gb = _inproj(st, xs, norm1_g[l].reshape(1, D), sc1, sh1, _pack_w_in(w_in[l]))

        wa_f = _gate_weights(w_af2[l], 0)
        wa_b = _gate_weights(w_ab2[l], RANK)
        o_f = _gla_direction(st, False, q, k, v, za, wa_f, b_af[l].reshape(1, GLA_DK))
        o_gla = _gla_direction(st, True, q, k, v, za, wa_b, b_ab[l].reshape(1, GLA_DK),
                               extra=(o_f, r, gla_norm_g[l].reshape(1, DV)))

        yf = _fnet(st, zf)

        wog, wof, wout = w_o_gla[l].astype(BF16), w_o_fnet[l].astype(BF16), w_out[l].astype(BF16)
        g2 = norm2_g[l].reshape(1, D)
        if l % 2 == 0:
            x1, h2 = _merge(st, o_gla, yf, ga, gb, xs, gt1, g2, sc2, sh2, wog, wof, wout)
            xs = (_ffn_dense(st, h2, x1, gt2, w_ff_gate[l // 2].astype(BF16),
                             w_ff_up[l // 2].astype(BF16), w_ff_down[l // 2].astype(BF16)),)
        else:
            wr = jnp.pad(w_router[l // 2], ((0, 0), (0, LANE - N_EXP)))
            x1, h2, logits = _merge(st, o_gla, yf, ga, gb, xs, gt1, g2, sc2, sh2, wog, wof, wout, wr)
            info, cnt = _router(logits)
            plan = _dispatch_plan(info, cnt)
            xp = _gather_rows(plan, h2)
            yp = _ffn_experts(plan["tile_e"], plan["tile_v"], xp, w_e_gate[l // 2].astype(BF16),
                              w_e_up[l // 2].astype(BF16), w_e_down[l // 2].astype(BF16))
            y_prompt, y_sample = _combine(st, plan, yp, x1, gt2, final_norm_g.reshape(1, D))

    return y_prompt.reshape(b, s, D), y_sample.reshape(db, ds, D)
```

```python
import functools

import numpy as np
import jax
import jax.numpy as jnp
from jax import lax
from jax.experimental import pallas as pl
from jax.experimental.pallas import tpu as pltpu

D = 1024
HEADS = 4
DK = 128
DV = 256
GLA_DK = HEADS * DK
GLA_DV = HEADS * DV
RANK = 16
GATE_NORM = 16.0
CHUNK = 64
FN_GROUPS = 4
FN_GW = 128
FNET_W = FN_GROUPS * FN_GW
D_FF = 2816
N_EXP = 8
D_EXP = 3584
EPS = 1e-6
N_MOD = 6

LANE = 128
SUBLANE = 8
VMEM_LIMIT = 56 * 1024 * 1024
TM = 512
ROW_PART = 256
GLA_BLK = 1024
MOE_TILE = 512
DISP_BLK = 512
SEG_ALIGN = 16
SEG_PIECE = 128
LROWS = 2 * DISP_BLK + N_EXP * SEG_ALIGN
FF_CHUNK_DENSE = 256
FF_CHUNK_MOE = 256
N1 = 128
FN_ROWS = 16

F32 = jnp.float32
BF16 = jnp.bfloat16
_HI = lax.Precision.HIGHEST


def _cparams(sem):
    return pltpu.CompilerParams(dimension_semantics=sem, vmem_limit_bytes=VMEM_LIMIT)


def _dot(a, b):
    return jnp.dot(a, b, preferred_element_type=F32)


def _dot_hi(a, b):
    return jnp.dot(a, b, precision=_HI, preferred_element_type=F32)


def _split_bf16(x):
    hi = x.astype(BF16)
    lo = (x - hi.astype(F32)).astype(BF16)
    return hi, lo


def _dot3(a_hi, a_lo, b_hi, b_lo):
    return _dot(a_hi, b_hi) + (_dot(a_hi, b_lo) + _dot(a_lo, b_hi))


class _Stream:
    def __init__(self, b, s, db, ds):
        self.b, self.s, self.db, self.ds = b, s, db, ds
        self.t0 = b * s
        self.t = b * s + db * ds
        self.nbatch = b + db

    def batch_of_block(self, i, rows):
        nb0, p0, p1 = self.t0 // rows, self.s // rows, self.ds // rows
        return jnp.where(i < nb0, i // p0, self.b + (i - nb0) // p1)

    def block_in_seq(self, i, rows):
        nb0, p0, p1 = self.t0 // rows, self.s // rows, self.ds // rows
        return jnp.where(i < nb0, i % p0, (i - nb0) % p1), jnp.where(i < nb0, p0, p1)


def _mod_kernel(c_ref, w_ref, b_ref, o_ref):
    c = c_ref[...]
    cs = c * jax.nn.sigmoid(c)
    o_ref[0] = _dot_hi(cs, w_ref[0]) + b_ref[0]


def _modulation(c_pad, w_ada, b_ada):
    depth = w_ada.shape[0]
    nb = c_pad.shape[0]
    return pl.pallas_call(
        _mod_kernel,
        grid=(depth, N_MOD),
        in_specs=[
            pl.BlockSpec((nb, D), lambda l, j: (0, 0)),
            pl.BlockSpec((1, D, D), lambda l, j: (l, 0, j)),
            pl.BlockSpec((1, 1, D), lambda l, j: (l, 0, j)),
        ],
        out_specs=pl.BlockSpec((1, nb, D), lambda l, j: (l, 0, j)),
        out_shape=jax.ShapeDtypeStruct((depth, nb, N_MOD * D), F32),
        compiler_params=_cparams(("arbitrary", "arbitrary")),
        name="adaln_mod",
    )(c_pad, w_ada, b_ada.reshape(depth, 1, N_MOD * D))


def _rms_mod(x, g, sc, sh):
    ms = jnp.mean(x * x, axis=-1, keepdims=True)
    return (x * lax.rsqrt(ms + EPS) * g) * (1.0 + sc) + sh


_C_Q, _C_K, _C_V, _C_R, _C_A, _C_F, _C_GA, _C_GB, _C_END = np.cumsum(
    [0, GLA_DK, GLA_DK, GLA_DV, GLA_DV, LANE, FNET_W, D, D]).tolist()


def _stream_specs(st, xs, rows):
    width = xs[0].shape[1]
    if len(xs) == 1:
        return [pl.BlockSpec((rows, width), lambda i, *_: (i, 0))]
    nb0 = st.t0 // rows
    return [pl.BlockSpec((rows, width), lambda i, *_: (jnp.minimum(i, nb0 - 1), 0)),
            pl.BlockSpec((rows, width), lambda i, *_: (jnp.maximum(i - nb0, 0), 0))]


def _stream_block(st, x_refs, rows, part=slice(None)):
    if len(x_refs) == 1:
        return x_refs[0][part, :]
    return jnp.where(pl.program_id(0) < st.t0 // rows, x_refs[0][part, :], x_refs[1][part, :])


def _row_parts(rows):
    return [slice(p * ROW_PART, (p + 1) * ROW_PART) for p in range(rows // ROW_PART)]


def _inproj_kernel(*refs, st, nx):
    x_refs = refs[:nx]
    (g_ref, sc_ref, sh_ref, w_ref,
     q_ref, k_ref, v_ref, r_ref, a_ref, f_ref, ga_ref, gb_ref) = refs[nx:]
    for part in _row_parts(TM):
        hb = _rms_mod(_stream_block(st, x_refs, TM, part), g_ref[...], sc_ref[0], sh_ref[0]).astype(BF16)

        def proj(lo, hi):
            return _dot(hb, w_ref[:, lo:hi])

        q_ref[part, :] = (proj(_C_Q, _C_K) * (DK ** -0.5)).astype(BF16)
        k_ref[part, :] = proj(_C_K, _C_V).astype(BF16)
        v_ref[part, :] = proj(_C_V, _C_R).astype(BF16)
        zr = proj(_C_R, _C_A)
        r_ref[part, :] = (zr * jax.nn.sigmoid(zr)).astype(BF16)
        a_ref[part, :] = proj(_C_A, _C_F)
        f_ref[part, :] = proj(_C_F, _C_GA).astype(BF16)
        ga_ref[part, :] = jax.nn.sigmoid(proj(_C_GA, _C_GB)).astype(BF16)
        gb_ref[part, :] = jax.nn.sigmoid(proj(_C_GB, _C_END)).astype(BF16)


def _inproj(st, xs, g, sc, sh, w):
    t = st.t
    row = lambda i: (i, 0)
    bat = lambda i: (st.batch_of_block(i, TM), 0, 0)
    widths = [(GLA_DK, BF16), (GLA_DK, BF16), (GLA_DV, BF16), (GLA_DV, BF16),
              (LANE, F32), (FNET_W, BF16), (D, BF16), (D, BF16)]
    return pl.pallas_call(
        functools.partial(_inproj_kernel, st=st, nx=len(xs)),
        grid=(t // TM,),
        in_specs=_stream_specs(st, xs, TM) + [
            pl.BlockSpec((1, D), lambda i: (0, 0)),
            pl.BlockSpec((1, 1, D), bat),
            pl.BlockSpec((1, 1, D), bat),
            pl.BlockSpec((D, _C_END), lambda i: (0, 0)),
        ],
        out_specs=[pl.BlockSpec((TM, w_), row) for w_, _ in widths],
        out_shape=[jax.ShapeDtypeStruct((t, w_), dt) for w_, dt in widths],
        compiler_params=_cparams(("parallel",)),
        name="inproj",
    )(*xs, g, sc, sh, w)


def _log_sigmoid(x):
    return jnp.minimum(x, 0.0) - jnp.log(1.0 + jnp.exp(-jnp.abs(x)))


def _gla_kernel(*refs, bwd, st):
    if bwd:
        (q_ref, k_ref, v_ref, za_ref, wa_ref, ba_ref, of_ref, r_ref, ng_ref,
         o_ref, s_scr, qi_scr, ki_scr, qin_scr, kout_scr, att_scr, kv_scr, dec_scr) = refs
    else:
        (q_ref, k_ref, v_ref, za_ref, wa_ref, ba_ref,
         o_ref, s_scr, qi_scr, ki_scr, qin_scr, kout_scr, att_scr, kv_scr, dec_scr) = refs
    i = pl.program_id(0)
    nblk = pl.num_programs(0)
    blk = (nblk - 1 - i) if bwd else i
    local, per = st.block_in_seq(blk, GLA_BLK)
    at_boundary = (local == per - 1) if bwd else (local == 0)

    @pl.when(at_boundary)
    def _():
        s_scr[...] = jnp.zeros_like(s_scr)

    nch = GLA_BLK // CHUNK
    rr = lax.broadcasted_iota(jnp.int32, (CHUNK, CHUNK), 0)
    cc = lax.broadcasted_iota(jnp.int32, (CHUNK, CHUNK), 1)
    keep = (rr <= cc) if bwd else (rr >= cc)
    tri = keep.astype(BF16)
    last = 0 if bwd else CHUNK - 1
    mid = CHUNK // 2 if bwd else CHUNK // 2 - 1

    lane = lax.broadcasted_iota(jnp.int32, (CHUNK, LANE), 1)
    low_part_lanes = jnp.logical_and(lane >= 2 * RANK, lane < 4 * RANK)

    log_decay = {}

    def gate_chunk(c):
        rows = slice(c * CHUNK, (c + 1) * CHUNK)
        zh, zl = _split_bf16(za_ref[rows, :])
        pre = _dot(jnp.where(low_part_lanes, zl, zh), wa_ref[...]) + ba_ref[...]
        g = _log_sigmoid(pre) * (1.0 / GATE_NORM)
        log_decay[c] = _split_bf16(g)

    def scale_chunk(c):
        rows = slice(c * CHUNK, (c + 1) * CHUNK)
        g0, g1 = log_decay.pop(c)
        b = _dot(tri, g0) + _dot(tri, g1)
        b_last = b[last:last + 1, :]
        b_ref = b[mid:mid + 1, :]
        dec_scr[c] = jnp.broadcast_to(jnp.exp(b_last), (LANE, GLA_DK)).T
        q = q_ref[rows, :].astype(F32)
        k = k_ref[rows, :].astype(F32)
        qi_scr[rows, :] = (q * jnp.exp(b - b_ref)).astype(BF16)
        ki_scr[rows, :] = (k * jnp.exp(b_ref - b)).astype(BF16)
        qin_scr[rows, :] = (q * jnp.exp(b)).astype(BF16)
        kout_scr[rows, :] = (k * jnp.exp(b_last - b)).astype(BF16)

    def stateless_products(c):
        rows = slice(c * CHUNK, (c + 1) * CHUNK)
        for h in range(HEADS):
            ks = slice(h * DK, (h + 1) * DK)
            att = lax.dot_general(qi_scr[rows, ks], ki_scr[rows, ks], (((1,), (1,)), ((), ())),
                                  preferred_element_type=F32)
            att_scr[c * HEADS + h] = jnp.where(keep, att, 0.0).astype(BF16)
            kv_scr[c * HEADS + h] = lax.dot_general(
                kout_scr[rows, ks], v_ref[rows, h * DV:(h + 1) * DV], (((0,), (0,)), ((), ())),
                preferred_element_type=F32)

    state = [s_scr[h] for h in range(HEADS)]

    def scan_chunk(c):
        rows = slice(c * CHUNK, (c + 1) * CHUNK)
        for h in range(HEADS):
            ks = slice(h * DK, (h + 1) * DK)
            vs = slice(h * DV, (h + 1) * DV)
            o = _dot(jnp.concatenate([qin_scr[rows, ks], att_scr[c * HEADS + h]], axis=1),
                     jnp.concatenate([state[h].astype(BF16), v_ref[rows, vs]], axis=0))
            decay = dec_scr[c, ks, :]
            state[h] = state[h] * jnp.concatenate([decay] * (DV // LANE), axis=1) + kv_scr[c * HEADS + h]
            if bwd:
                o = o + of_ref[rows, vs]
                o = o * lax.rsqrt(jnp.mean(o * o, axis=-1, keepdims=True) + EPS) * ng_ref[...]
                o_ref[rows, vs] = (o * r_ref[rows, vs].astype(F32)).astype(o_ref.dtype)
            else:
                o_ref[rows, vs] = o

    order = list(reversed(range(nch))) if bwd else list(range(nch))
    stages = (gate_chunk, scale_chunk, stateless_products, scan_chunk)
    for step in range(nch + len(stages) - 1):
        for lag, stage in enumerate(stages):
            if 0 <= step - lag < nch:
                stage(order[step - lag])
    for h in range(HEADS):
        s_scr[h] = state[h]


def _gla_direction(st, bwd, q, k, v, za, wa, ba, extra=()):
    t = st.t
    nblk = t // GLA_BLK
    row = (lambda i: (nblk - 1 - i, 0)) if bwd else (lambda i: (i, 0))
    const = lambda i: (0, 0)
    in_specs = [
        pl.BlockSpec((GLA_BLK, GLA_DK), row),
        pl.BlockSpec((GLA_BLK, GLA_DK), row),
        pl.BlockSpec((GLA_BLK, GLA_DV), row),
        pl.BlockSpec((GLA_BLK, LANE), row),
        pl.BlockSpec((LANE, GLA_DK), const),
        pl.BlockSpec((1, GLA_DK), const),
    ]
    if bwd:
        in_specs += [pl.BlockSpec((GLA_BLK, GLA_DV), row), pl.BlockSpec((GLA_BLK, GLA_DV), row),
                     pl.BlockSpec((1, DV), const)]
    scaled = pltpu.VMEM((GLA_BLK, GLA_DK), BF16)
    nch = GLA_BLK // CHUNK
    return pl.pallas_call(
        functools.partial(_gla_kernel, bwd=bwd, st=st),
        grid=(nblk,),
        in_specs=in_specs,
        out_specs=pl.BlockSpec((GLA_BLK, GLA_DV), row),
        out_shape=jax.ShapeDtypeStruct((t, GLA_DV), BF16 if bwd else F32),
        scratch_shapes=[pltpu.VMEM((HEADS, DK, DV), F32), scaled, scaled, scaled, scaled,
                        pltpu.VMEM((nch * HEADS, CHUNK, CHUNK), BF16),
                        pltpu.VMEM((nch * HEADS, DK, DV), F32),
                        pltpu.VMEM((nch, GLA_DK, LANE), F32)],
        compiler_params=_cparams(("arbitrary",)),
        name="gla_bwd" if bwd else "gla_fwd",
    )(q, k, v, za, wa, ba, *extra)


def _dft_cos_sin(n):
    idx = np.arange(n, dtype=np.float64)
    ang = 2.0 * np.pi * np.outer(idx, idx) / n
    return np.cos(ang) / np.sqrt(n), np.sin(ang) / np.sqrt(n)


def _fnet_stage1_kernel(x_ref, m_ref, y_ref):
    n2 = x_ref.shape[0]
    xf = x_ref[...].astype(F32)
    for half in range(FN_ROWS // SUBLANE):
        rows = slice(half * SUBLANE, (half + 1) * SUBLANE)
        x = xf[:, rows, :].reshape(n2 * SUBLANE, FNET_W).astype(BF16)
        y = _dot(m_ref[...], x).reshape(2, n2, SUBLANE, FNET_W)
        y_ref[:, :, rows, :] = y.astype(BF16)


def _fnet_stage1(zf3, bn, n2, seq0, m1):
    return pl.pallas_call(
        _fnet_stage1_kernel,
        grid=(bn, N1 // FN_ROWS),
        in_specs=[
            pl.BlockSpec((n2, FN_ROWS, FNET_W), lambda b, j: (seq0 + b, j, 0)),
            pl.BlockSpec(m1.shape, lambda b, j: (0, 0)),
        ],
        out_specs=pl.BlockSpec((2, n2, FN_ROWS, FNET_W), lambda b, j: (0, b, j, 0)),
        out_shape=jax.ShapeDtypeStruct((2, bn * n2, N1, FNET_W), BF16),
        compiler_params=_cparams(("parallel", "parallel")),
        name="fnet_stage1",
    )(zf3, m1)


def _fnet_stage2_kernel(y_ref, tc_ref, ts_ref, m2_ref, m3_ref, o_ref, z_scr):
    for g in range(SUBLANE):
        rows = slice(g * N1, (g + 1) * N1)
        yr = y_ref[0, rows, :].astype(F32)
        yi = y_ref[1, rows, :].astype(F32)
        tc = tc_ref[g]
        ts = ts_ref[g]
        stack = jnp.concatenate([yr * tc + yi * ts, yi * tc - yr * ts], axis=0)
        z = _dot(m2_ref[...], stack.astype(BF16))
        z_scr[rows, 0:FNET_W] = z[:N1].astype(BF16)
        z_scr[rows, FNET_W:2 * FNET_W] = z[N1:].astype(BF16)
    out = _dot(z_scr[...], m3_ref[...])
    for g in range(SUBLANE):
        o_ref[:, g, :] = out[g * N1:(g + 1) * N1]


def _fnet_stage2(y2, bn, n2, tc, ts, m2, m3):
    rows = SUBLANE * N1
    per_seq = n2 // SUBLANE
    const = lambda b, j: (0, 0)
    return pl.pallas_call(
        _fnet_stage2_kernel,
        grid=(bn, per_seq),
        in_specs=[
            pl.BlockSpec((2, rows, FNET_W), lambda b, j: (0, b * per_seq + j, 0)),
            pl.BlockSpec((SUBLANE, N1, 1), lambda b, j: (j, 0, 0)),
            pl.BlockSpec((SUBLANE, N1, 1), lambda b, j: (j, 0, 0)),
            pl.BlockSpec((2 * N1, 2 * N1), const),
            pl.BlockSpec((2 * FNET_W, FNET_W), const),
        ],
        out_specs=pl.BlockSpec((N1, SUBLANE, FNET_W), lambda b, j: (b, j, 0)),
        out_shape=jax.ShapeDtypeStruct((bn * N1, n2, FNET_W), F32),
        scratch_shapes=[pltpu.VMEM((rows, 2 * FNET_W), BF16)],
        compiler_params=_cparams(("parallel", "parallel")),
        name="fnet_stage2",
    )(y2, tc, ts, m2, m3).reshape(bn * n2 * N1, FNET_W)


def _fnet(st, zf):
    as_bf16 = lambda m: jnp.asarray(np.asarray(m, np.float32).astype(BF16))
    c1, s1 = _dft_cos_sin(N1)
    m2 = as_bf16(np.block([[c1, s1], [-s1, c1]]))
    cc, sc = _dft_cos_sin(FN_GW)
    eye = np.eye(FN_GROUPS)
    m3 = as_bf16(np.concatenate([np.kron(eye, cc), np.kron(eye, sc)], axis=0))
    zf3 = zf.reshape(st.t // N1, N1, FNET_W)
    outs = []
    for bn, s, row0 in ((st.b, st.s, 0), (st.db, st.ds, st.t0)):
        n2 = s // N1
        c2, s2 = _dft_cos_sin(n2)
        m1 = as_bf16(np.kron(np.concatenate([c2, -s2], axis=0), np.eye(SUBLANE)))
        ang = 2.0 * np.pi * np.outer(np.arange(n2), np.arange(N1)) / s
        tc = jnp.asarray(np.cos(ang)[:, :, None], F32)
        ts = jnp.asarray(np.sin(ang)[:, :, None], F32)
        y = _fnet_stage1(zf3, bn, n2, row0 // s, m1)
        outs.append(_fnet_stage2(y.reshape(2, bn * s, FNET_W), bn, n2, tc, ts, m2, m3))
    return tuple(outs)


def _merge_kernel(*refs, st, nx, router):
    x_refs = refs[:nx]
    yf_refs = refs[nx:nx + 2]
    (og_ref, ga_ref, gb_ref, gt_ref, g2_ref, sc_ref, sh_ref,
     wog_ref, wof_ref, wout_ref) = refs[nx + 2:nx + 12]
    rest = refs[nx + 12:]
    if router:
        wrh_ref, wrl_ref, x1_ref, h2_ref, lg_ref = rest
    else:
        x1_ref, h2_ref = rest
    for part in _row_parts(TM):
        ya = _dot(og_ref[part, :], wog_ref[...])
        yb = _dot(_stream_block(st, yf_refs, TM, part).astype(BF16), wof_ref[...])
        m = ga_ref[part, :].astype(F32) * ya + gb_ref[part, :].astype(F32) * yb
        x1 = _stream_block(st, x_refs, TM, part) + gt_ref[0] * _dot(m.astype(BF16), wout_ref[...])
        h2 = _rms_mod(x1, g2_ref[...], sc_ref[0], sh_ref[0])
        if router:
            lg_ref[part, :] = _dot3(*_split_bf16(h2), wrh_ref[...], wrl_ref[...])
        x1_ref[part, :] = x1
        h2_ref[part, :] = h2.astype(BF16)


def _merge(st, og, yfs, ga, gb, xs, gt, g2, sc, sh, wog, wof, wout, wr=None):
    t = st.t
    row = lambda i: (i, 0)
    bat = lambda i: (st.batch_of_block(i, TM), 0, 0)
    const = lambda i: (0, 0)
    in_specs = _stream_specs(st, xs, TM) + _stream_specs(st, yfs, TM) + [
        pl.BlockSpec((TM, GLA_DV), row),
        pl.BlockSpec((TM, D), row), pl.BlockSpec((TM, D), row),
        pl.BlockSpec((1, 1, D), bat), pl.BlockSpec((1, D), const),
        pl.BlockSpec((1, 1, D), bat), pl.BlockSpec((1, 1, D), bat),
        pl.BlockSpec((GLA_DV, D), const), pl.BlockSpec((FNET_W, D), const), pl.BlockSpec((D, D), const),
    ]
    out_specs = [pl.BlockSpec((TM, D), row), pl.BlockSpec((TM, D), row)]
    out_shape = [jax.ShapeDtypeStruct((t, D), F32), jax.ShapeDtypeStruct((t, D), BF16)]
    args = [*xs, *yfs, og, ga, gb, gt, g2, sc, sh, wog, wof, wout]
    if wr is not None:
        in_specs += [pl.BlockSpec((D, LANE), const), pl.BlockSpec((D, LANE), const)]
        out_specs.append(pl.BlockSpec((TM, LANE), row))
        out_shape.append(jax.ShapeDtypeStruct((t, LANE), F32))
        args += list(_split_bf16(wr))
    return pl.pallas_call(
        functools.partial(_merge_kernel, st=st, nx=len(xs), router=wr is not None),
        grid=(t // TM,),
        in_specs=in_specs, out_specs=out_specs, out_shape=out_shape,
        compiler_params=_cparams(("parallel",)),
        name="merge_router" if wr is not None else "merge",
    )(*args)


def _swiglu_into(acc_ref, h, wg_ref, wu_ref, wd_ref, width, chunk, lead=()):
    for c in range(width // chunk):
        cols = slice(c * chunk, (c + 1) * chunk)
        g = _dot(h, wg_ref[lead + (slice(None), cols)])
        u = _dot(h, wu_ref[lead + (slice(None), cols)])
        a = (g * jax.nn.sigmoid(g) * u).astype(BF16)
        part = _dot(a, wd_ref[lead + (cols, slice(None))])
        if c == 0:
            acc_ref[...] = part
        else:
            acc_ref[...] += part


def _ffn_dense_kernel(h_ref, x_ref, gt_ref, wg_ref, wu_ref, wd_ref, o_ref, acc_ref):
    _swiglu_into(acc_ref, h_ref[...], wg_ref, wu_ref, wd_ref, D_FF, FF_CHUNK_DENSE)
    o_ref[...] = x_ref[...] + gt_ref[0] * acc_ref[...]


def _ffn_dense(st, h, x, gt, wg, wu, wd):
    t = st.t
    row = lambda i: (i, 0)
    const = lambda i: (0, 0)
    once = pl.Buffered(1)
    return pl.pallas_call(
        _ffn_dense_kernel,
        grid=(t // TM,),
        in_specs=[
            pl.BlockSpec((TM, D), row), pl.BlockSpec((TM, D), row),
            pl.BlockSpec((1, 1, D), lambda i: (st.batch_of_block(i, TM), 0, 0)),
            pl.BlockSpec((D, D_FF), const, pipeline_mode=once),
            pl.BlockSpec((D, D_FF), const, pipeline_mode=once),
            pl.BlockSpec((D_FF, D), const, pipeline_mode=once),
        ],
        out_specs=pl.BlockSpec((TM, D), row),
        out_shape=jax.ShapeDtypeStruct((t, D), F32),
        scratch_shapes=[pltpu.VMEM((TM, D), F32)],
        compiler_params=_cparams(("parallel",)),
        name="ffn_dense",
    )(h, x, gt, wg, wu, wd)


def _ffn_expert_kernel(te_ref, tv_ref, x_ref, wg_ref, wu_ref, wd_ref, o_ref, acc_ref):
    k = pl.program_id(0)

    @pl.when(tv_ref[k] == 1)
    def _():
        _swiglu_into(acc_ref, x_ref[...], wg_ref, wu_ref, wd_ref, D_EXP, FF_CHUNK_MOE, lead=(0,))
        o_ref[...] = acc_ref[...].astype(o_ref.dtype)

    @pl.when(tv_ref[k] == 0)
    def _():
        o_ref[...] = jnp.zeros_like(o_ref)


def _ffn_experts(tile_e, tile_v, xp, wg, wu, wd):
    ntiles = xp.shape[0] // MOE_TILE
    once = pl.Buffered(1)
    wmap = lambda k, te, tv: (te[k], 0, 0)
    xmap = lambda k, te, tv: (k * tv[k], 0)
    grid_spec = pltpu.PrefetchScalarGridSpec(
        num_scalar_prefetch=2,
        grid=(ntiles,),
        in_specs=[
            pl.BlockSpec((MOE_TILE, D), xmap),
            pl.BlockSpec((1, D, D_EXP), wmap, pipeline_mode=once),
            pl.BlockSpec((1, D, D_EXP), wmap, pipeline_mode=once),
            pl.BlockSpec((1, D_EXP, D), wmap, pipeline_mode=once),
        ],
        out_specs=pl.BlockSpec((MOE_TILE, D), lambda k, te, tv: (k, 0)),
        scratch_shapes=[pltpu.VMEM((MOE_TILE, D), F32)],
    )
    return pl.pallas_call(
        _ffn_expert_kernel,
        grid_spec=grid_spec,
        out_shape=jax.ShapeDtypeStruct(xp.shape, BF16),
        compiler_params=_cparams(("arbitrary",)),
        name="ffn_experts",
    )(tile_e, tile_v, xp, wg, wu, wd)


def _router_kernel(lg_ref, info_ref, cnt_ref):
    lane = lax.broadcasted_iota(jnp.int32, (DISP_BLK, LANE), 1).astype(F32)
    neg = jnp.float32(-jnp.inf)
    lg = jnp.where(lane < N_EXP, lg_ref[...], neg)
    v1 = jnp.max(lg, axis=1, keepdims=True)
    i1 = jnp.min(jnp.where(lg == v1, lane, float(LANE)), axis=1, keepdims=True)
    oh1 = lane == i1
    lg2 = jnp.where(oh1, neg, lg)
    v2 = jnp.max(lg2, axis=1, keepdims=True)
    i2 = jnp.min(jnp.where(lg2 == v2, lane, float(LANE)), axis=1, keepdims=True)
    oh2 = lane == i2
    e = jnp.exp(v2 - v1)
    w1 = 1.0 / (1.0 + e)
    w2 = e / (1.0 + e)
    oh = oh1.astype(F32) + oh2.astype(F32)
    rr = lax.broadcasted_iota(jnp.int32, (DISP_BLK, DISP_BLK), 0)
    cc = lax.broadcasted_iota(jnp.int32, (DISP_BLK, DISP_BLK), 1)
    earlier = (cc < rr).astype(BF16)
    before = _dot(earlier, oh.astype(BF16))
    pos1 = jnp.sum(jnp.where(oh1, before, 0.0), axis=1, keepdims=True)
    pos2 = jnp.sum(jnp.where(oh2, before, 0.0), axis=1, keepdims=True)
    blk_cnt = jnp.sum(oh, axis=0, keepdims=True)
    cnt_ref[0] = jnp.broadcast_to(blk_cnt, (SUBLANE, LANE))
    packed = jnp.where(lane == 0, i1,
             jnp.where(lane == 1, i2,
             jnp.where(lane == 2, w1,
             jnp.where(lane == 3, w2,
             jnp.where(lane == 4, pos1,
             jnp.where(lane == 5, pos2, 0.0))))))
    info_ref[0] = packed.T[0:SUBLANE, :]


def _router(logits):
    t = logits.shape[0]
    nblk = t // DISP_BLK
    return pl.pallas_call(
        _router_kernel,
        grid=(nblk,),
        in_specs=[pl.BlockSpec((DISP_BLK, LANE), lambda i: (i, 0))],
        out_specs=[pl.BlockSpec((1, SUBLANE, DISP_BLK), lambda i: (i, 0, 0)),
                   pl.BlockSpec((1, SUBLANE, LANE), lambda i: (i, 0, 0))],
        out_shape=[jax.ShapeDtypeStruct((nblk, SUBLANE, DISP_BLK), F32),
                   jax.ShapeDtypeStruct((nblk, SUBLANE, LANE), F32)],
        compiler_params=_cparams(("parallel",)),
        name="router",
    )(logits)


def _dispatch_plan(info, cnt):
    nblk = info.shape[0]
    t = nblk * DISP_BLK
    max_rows = 2 * t + nblk * N_EXP * (SEG_ALIGN - 1) + N_EXP * (MOE_TILE - 1)
    ntiles = -(-max_rows // MOE_TILE)
    e12 = info[:, 0:2, :].astype(jnp.int32)
    w12 = info[:, 2:4, :]
    rank12 = info[:, 4:6, :].astype(jnp.int32)
    bc = cnt[:, 0, :N_EXP].astype(jnp.int32)
    pc = (bc + SEG_ALIGN - 1) // SEG_ALIGN * SEG_ALIGN
    loff = jnp.cumsum(pc, axis=1) - pc
    etot = jnp.sum(pc, axis=0)
    epad = (etot + MOE_TILE - 1) // MOE_TILE * MOE_TILE
    eend = jnp.cumsum(epad)
    goff = (eend - epad)[None, :] + jnp.cumsum(pc, axis=0) - pc
    experts = jnp.arange(N_EXP, dtype=jnp.int32)
    lpos = jnp.sum(jnp.where(e12[..., None] == experts, loff[:, None, None, :], 0), axis=-1) + rank12
    tile_row = jnp.arange(ntiles, dtype=jnp.int32) * MOE_TILE
    tile_e = jnp.minimum(jnp.sum((eend[None, :] <= tile_row[:, None]).astype(jnp.int32), axis=1),
                         N_EXP - 1)
    tile_v = (tile_row < eend[-1]).astype(jnp.int32)
    tail_end = eend.at[N_EXP - 1].set(ntiles * MOE_TILE)
    tail_off = eend - epad + etot
    return dict(lpos=lpos, w12=w12, tile_e=tile_e, tile_v=tile_v, ntiles=ntiles,
                segs=(goff.reshape(-1), loff.reshape(-1), (pc // SEG_ALIGN).reshape(-1)),
                tails=(tail_off, (tail_end - tail_off) // SEG_ALIGN))


def _for_each_segment_group(goff_ref, loff_ref, ngrp_ref, blk, fn):
    per_piece = SEG_PIECE // SEG_ALIGN
    for e in range(N_EXP):
        idx = blk * N_EXP + e
        lo, go, ngrp = loff_ref[idx], goff_ref[idx], ngrp_ref[idx]
        nbig = ngrp // per_piece

        def big(i, carry, e=e):
            fn(pl.multiple_of(lo + i * SEG_PIECE, SEG_ALIGN), pl.multiple_of(go + i * SEG_PIECE, SEG_ALIGN),
               SEG_PIECE, e % 2)
            return carry

        def small(i, carry, e=e):
            fn(pl.multiple_of(lo + i * SEG_ALIGN, SEG_ALIGN), pl.multiple_of(go + i * SEG_ALIGN, SEG_ALIGN),
               SEG_ALIGN, e % 2)
            return carry

        lax.fori_loop(0, nbig, big, 0)
        lax.fori_loop(nbig * per_piece, ngrp, small, 0)


def _local_rows(lpos_ref):
    rows = lax.broadcasted_iota(jnp.int32, (LROWS, DISP_BLK), 0)
    return rows == lpos_ref[0, 0:1, :], rows == lpos_ref[0, 1:2, :]


def _gather_kernel(goff_ref, loff_ref, ngrp_ref, toff_ref, tgrp_ref, h_ref, lpos_ref,
                   xp_ref, xl_ref, zero_ref, sem):
    b = pl.program_id(0)
    nblk = pl.num_programs(0)
    slot = b % 2

    def copy(slot_, lrow, srow, nrows):
        return pltpu.make_async_copy(xl_ref.at[slot_, pl.ds(lrow, nrows), :],
                                     xp_ref.at[pl.ds(srow, nrows), :], sem.at[slot_])

    def each(blk, slot_, start):
        _for_each_segment_group(
            goff_ref, loff_ref, ngrp_ref, blk,
            lambda lrow, srow, n, thread: (copy(slot_, lrow, srow, n).start(priority=thread) if start
                                           else copy(slot_, lrow, srow, n).wait()))

    def each_tail(fn):
        for e in range(N_EXP):
            def body(i, carry):
                row = pl.multiple_of(toff_ref[e] + i * SEG_ALIGN, SEG_ALIGN)
                fn(pltpu.make_async_copy(zero_ref, xp_ref.at[pl.ds(row, SEG_ALIGN), :], sem.at[2]))
                return carry

            lax.fori_loop(0, tgrp_ref[e], body, 0)

    @pl.when(b == 0)
    def _():
        zero_ref[...] = jnp.zeros_like(zero_ref)
        each_tail(lambda cp: cp.start())

    @pl.when(b >= 2)
    def _():
        each(b - 2, slot, start=False)

    hit1, hit2 = _local_rows(lpos_ref)
    sel = jnp.logical_or(hit1, hit2).astype(BF16)
    xl_ref[slot] = _dot(sel, h_ref[...]).astype(BF16)
    each(b, slot, start=True)

    @pl.when(b == nblk - 1)
    def _():
        each(b, slot, start=False)
        each_tail(lambda cp: cp.wait())

        @pl.when(b >= 1)
        def _():
            each(b - 1, 1 - slot, start=False)


def _gather_rows(plan, h):
    nrows = plan["ntiles"] * MOE_TILE
    nblk = h.shape[0] // DISP_BLK
    grid_spec = pltpu.PrefetchScalarGridSpec(
        num_scalar_prefetch=5,
        grid=(nblk,),
        in_specs=[
            pl.BlockSpec((DISP_BLK, D), lambda b, *_: (b, 0)),
            pl.BlockSpec((1, 2, DISP_BLK), lambda b, *_: (b, 0, 0)),
        ],
        out_specs=pl.BlockSpec(memory_space=pl.ANY),
        scratch_shapes=[pltpu.VMEM((2, LROWS, D), BF16), pltpu.VMEM((SEG_ALIGN, D), BF16),
                        pltpu.SemaphoreType.DMA((3,))],
    )
    return pl.pallas_call(
        _gather_kernel,
        grid_spec=grid_spec,
        out_shape=jax.ShapeDtypeStruct((nrows, D), BF16),
        compiler_params=_cparams(("arbitrary",)),
        name="moe_gather",
    )(*plan["segs"], *plan["tails"], h, plan["lpos"])


def _combine_kernel(goff_ref, loff_ref, ngrp_ref, yp_ref, lpos_ref, w_ref, x_ref, gt_ref, gf_ref,
                    op_ref, os_ref, yl_ref, sem, *, nb0):
    b = pl.program_id(0)
    nblk = pl.num_programs(0)
    slot = b % 2

    def copy(slot_, lrow, srow, nrows):
        return pltpu.make_async_copy(yp_ref.at[pl.ds(srow, nrows), :],
                                     yl_ref.at[slot_, pl.ds(lrow, nrows), :], sem.at[slot_])

    def each(blk, slot_, start):
        _for_each_segment_group(
            goff_ref, loff_ref, ngrp_ref, blk,
            lambda lrow, srow, n, thread: (copy(slot_, lrow, srow, n).start(priority=thread) if start
                                           else copy(slot_, lrow, srow, n).wait()))

    @pl.when(b == 0)
    def _():
        yl_ref[...] = jnp.zeros_like(yl_ref)
        each(0, 0, start=True)

    @pl.when(b + 1 < nblk)
    def _():
        each(b + 1, 1 - slot, start=True)

    each(b, slot, start=False)

    hit1, hit2 = _local_rows(lpos_ref)
    wsel = jnp.where(hit1, w_ref[0, 0:1, :], 0.0) + jnp.where(hit2, w_ref[0, 1:2, :], 0.0)
    wh, wl = _split_bf16(wsel)
    tn = (((0,), (0,)), ((), ()))
    yl = yl_ref[slot]
    y = (lax.dot_general(wh, yl, tn, preferred_element_type=F32)
         + lax.dot_general(wl, yl, tn, preferred_element_type=F32))
    x2 = x_ref[...] + gt_ref[0] * y
    ms = jnp.mean(x2 * x2, axis=-1, keepdims=True)
    out = x2 * lax.rsqrt(ms + EPS) * gf_ref[...]

    @pl.when(b < nb0)
    def _():
        op_ref[...] = out

    @pl.when(b >= nb0)
    def _():
        os_ref[...] = out


def _combine(st, plan, yp, x, gt, gf):
    nblk = st.t // DISP_BLK
    nb0 = st.t0 // DISP_BLK
    blk3 = lambda b, *_: (b, 0, 0)
    grid_spec = pltpu.PrefetchScalarGridSpec(
        num_scalar_prefetch=3,
        grid=(nblk,),
        in_specs=[
            pl.BlockSpec(memory_space=pl.ANY),
            pl.BlockSpec((1, 2, DISP_BLK), blk3),
            pl.BlockSpec((1, 2, DISP_BLK), blk3),
            pl.BlockSpec((DISP_BLK, D), lambda b, *_: (b, 0)),
            pl.BlockSpec((1, 1, D), lambda b, *_: (st.batch_of_block(b, DISP_BLK), 0, 0)),
            pl.BlockSpec((1, D), lambda b, *_: (0, 0)),
        ],
        out_specs=[pl.BlockSpec((DISP_BLK, D), lambda b, *_: (jnp.minimum(b, nb0 - 1), 0)),
                   pl.BlockSpec((DISP_BLK, D), lambda b, *_: (jnp.maximum(b - nb0, 0), 0))],
        scratch_shapes=[pltpu.VMEM((2, LROWS, D), BF16), pltpu.SemaphoreType.DMA((2,))],
    )
    return pl.pallas_call(
        functools.partial(_combine_kernel, nb0=nb0),
        grid_spec=grid_spec,
        out_shape=[jax.ShapeDtypeStruct((st.t0, D), F32), jax.ShapeDtypeStruct((st.t - st.t0, D), F32)],
        compiler_params=_cparams(("arbitrary",)),
        name="moe_combine_norm",
    )(*plan["segs"], yp, plan["lpos"], plan["w12"], x, gt, gf)


def _pack_w_in(w):
    o = np.cumsum([0, GLA_DK, GLA_DK, GLA_DV, GLA_DV, RANK, RANK, FNET_W, D, D]).tolist()
    gates = jnp.pad(jnp.tile(w[:, o[4]:o[6]], (1, 3)), ((0, 0), (0, LANE - 6 * RANK)))
    return jnp.concatenate([w[:, o[0]:o[4]], gates, w[:, o[6]:o[9]]], axis=1).astype(BF16)


def _gate_weights(w_a2, offset):
    hi, lo = _split_bf16(w_a2)
    out = jnp.zeros((LANE, GLA_DK), BF16)
    for copy, part in enumerate((hi, hi, lo)):
        out = lax.dynamic_update_slice(out, part, (copy * 2 * RANK + offset, 0))
    return out


def kernel(x_prompt, x_sample, c_prompt, c_sample, norm1_g, norm2_g, w_ada, b_ada, w_in, w_af2, b_af, w_ab2, b_ab, gla_norm_g, w_o_gla, w_o_fnet, w_out, w_ff_gate, w_ff_up, w_ff_down, w_router, w_e_gate, w_e_up, w_e_down, final_norm_g):
    b, s, _ = x_prompt.shape
    db, ds, _ = x_sample.shape
    depth = w_in.shape[0]
    assert depth == 2 and s % max(GLA_BLK, TM, DISP_BLK, SUBLANE * N1) == 0
    assert ds % max(GLA_BLK, TM, DISP_BLK, SUBLANE * N1) == 0 and (b * s) % ds == 0
    st = _Stream(b, s, db, ds)

    xs = (x_prompt.reshape(b * s, D), x_sample.reshape(db * ds, D))
    c = jnp.concatenate([c_prompt, c_sample], axis=0)
    nb_pad = -(-st.nbatch // SUBLANE) * SUBLANE
    c = jnp.pad(c, ((0, nb_pad - st.nbatch), (0, 0)))
    mod = _modulation(c, w_ada, b_ada)

    def mod_piece(l, j):
        return mod[l, :, j * D:(j + 1) * D].reshape(nb_pad, 1, D)

    for l in range(depth):
        sh1, sc1, gt1, sh2, sc2, gt2 = [mod_piece(l, j) for j in range(N_MOD)]
        q, k, v, r, za, zf, ga, gb = _inproj(st, xs, norm1_g[l].reshape(1, D), sc1, sh1, _pack_w_in(w_in[l]))

        wa_f = _gate_weights(w_af2[l], 0)
        wa_b = _gate_weights(w_ab2[l], RANK)
        o_f = _gla_direction(st, False, q, k, v, za, wa_f, b_af[l].reshape(1, GLA_DK))
        o_gla = _gla_direction(st, True, q, k, v, za, wa_b, b_ab[l].reshape(1, GLA_DK),
                               extra=(o_f, r, gla_norm_g[l].reshape(1, DV)))

        yf = _fnet(st, zf)

        wog, wof, wout = w_o_gla[l].astype(BF16), w_o_fnet[l].astype(BF16), w_out[l].astype(BF16)
        g2 = norm2_g[l].reshape(1, D)
        if l % 2 == 0:
            x1, h2 = _merge(st, o_gla, yf, ga, gb, xs, gt1, g2, sc2, sh2, wog, wof, wout)
            xs = (_ffn_dense(st, h2, x1, gt2, w_ff_gate[l // 2].astype(BF16),
                             w_ff_up[l // 2].astype(BF16), w_ff_down[l // 2].astype(BF16)),)
        else:
            wr = jnp.pad(w_router[l // 2], ((0, 0), (0, LANE - N_EXP)))
            x1, h2, logits = _merge(st, o_gla, yf, ga, gb, xs, gt1, g2, sc2, sh2, wog, wof, wout, wr)
            info, cnt = _router(logits)
            plan = _dispatch_plan(info, cnt)
            xp = _gather_rows(plan, h2)
            yp = _ffn_experts(plan["tile_e"], plan["tile_v"], xp, w_e_gate[l // 2].astype(BF16),
                              w_e_up[l // 2].astype(BF16), w_e_down[l // 2].astype(BF16))
            y_prompt, y_sample = _combine(st, plan, yp, x1, gt2, final_norm_g.reshape(1, D))

    return y_prompt.reshape(b, s, D), y_sample.reshape(db, ds, D)
```
